```python
import math
import jax, jax.numpy as jnp
from jax import lax
import numpy as np


D_MODEL = 1024
BATCH = 4
SEQ = 8192
DEPTH = 2
DEC_BATCH = 32
DEC_SEQ = 1
PAST_LEN = 16384
PAGE_SIZE = 128

N_MIXERS = 4
GROUP_W = D_MODEL // N_MIXERS
MIX_W = N_MIXERS * GROUP_W
HEAD_DIM = 64
N_HEADS = GROUP_W // HEAD_DIM
POOL_WINDOWS = (2, 4, 8, 16)
POOL_GROUP = GROUP_W // len(POOL_WINDOWS)
POOL_BUF = max(POOL_WINDOWS) - 1
SGU_CHUNK = 128
MLSTM_CHUNK = 64
MOBA_BLOCK = 256
MOBA_TOPK = 3
MOBA_QBLOCK = 128
ROPE_THETA = 500000.0
ROT_DIM = HEAD_DIM // 4
PROJ_W = 10 * GROUP_W + 2 * N_HEADS
D_FF = 2816
N_EXPERTS = 8
TOP_K = 2
N_DENSE = (DEPTH + 1) // 2
N_MOE = DEPTH // 2
NORM_EPS = 1e-6
NEG = -1e30

kernel_name = 'hymba_pool_sgu_mlstm_moba_step'

F32 = jnp.float32


def _proj_splits():
    sizes = [GROUP_W] * 7 + [N_HEADS, N_HEADS] + [GROUP_W] * 3
    return np.cumsum(sizes)[:-1].tolist()


def rmsnorm(x, w):
    xf = x.astype(F32)
    y = xf * lax.rsqrt(jnp.mean(xf * xf, axis=-1, keepdims=True) + NORM_EPS)
    return (y * w.astype(F32)).astype(x.dtype)


def rope(x, pos):
    half = ROT_DIM // 2
    inv = 1.0 / (ROPE_THETA ** (jnp.arange(half, dtype=F32) / half))
    ang = pos.astype(F32)[:, None] * inv[None, :]
    cos = jnp.cos(ang)[None, :, None, :]
    sin = jnp.sin(ang)[None, :, None, :]
    xr = x[..., :ROT_DIM].astype(F32)
    x1, x2 = xr[..., :half], xr[..., half:]
    rot = jnp.concatenate([x1 * cos - x2 * sin, x1 * sin + x2 * cos], axis=-1).astype(x.dtype)
    return jnp.concatenate([rot, x[..., ROT_DIM:]], axis=-1)


def pool_mixer(p, prefix, pos, pool_w, pool_scale):
    T = p.shape[1]
    ext = jnp.concatenate([prefix.astype(p.dtype), p], axis=1)
    cs = jnp.cumsum(ext.astype(F32), axis=1)
    cs = jnp.concatenate([jnp.zeros_like(cs[:, :1]), cs], axis=1)
    end = cs[:, POOL_BUF + 1:]
    outs = []
    for g, win in enumerate(POOL_WINDOWS):
        sl = slice(g * POOL_GROUP, (g + 1) * POOL_GROUP)
        start = cs[:, POOL_BUF + 1 - win: POOL_BUF + 1 - win + T, sl]
        cnt = jnp.minimum(pos + 1, win).astype(F32)[None, :, None]
        d = (end[..., sl] - start) / cnt - p[..., sl].astype(F32)
        outs.append(jnp.einsum('btc,cd->btd', d, pool_w[g].astype(F32)))
    y = jnp.concatenate(outs, axis=-1) * pool_scale.astype(F32)
    return y.astype(p.dtype), ext[:, -POOL_BUF:]


def sgu_mixer(u, v, sgu_w, sgu_b):
    B, T, C = u.shape
    Tp = -(-T // SGU_CHUNK) * SGU_CHUNK
    vp = jnp.pad(v, ((0, 0), (0, Tp - T), (0, 0))).reshape(B, Tp // SGU_CHUNK, SGU_CHUNK, N_HEADS, HEAD_DIM)
    mask = jnp.tril(jnp.ones((SGU_CHUNK, SGU_CHUNK), dtype=bool))
    ws = jnp.where(mask[None], sgu_w, 0)
    mixed = jnp.einsum('hij,bcjhd->bcihd', ws, vp) + jnp.transpose(sgu_b)[None, None, :, :, None]
    mixed = mixed.reshape(B, Tp, C)[:, :T]
    return u * mixed.astype(u.dtype)


def mlstm_mixer(q, k, v, ig, fg, state):
    B, T, H, D = q.shape
    L = math.gcd(T, MLSTM_CHUNK)
    NC = T // L
    to_c = lambda t: t.astype(F32).reshape(B, NC, L, H, D).transpose(1, 0, 3, 2, 4)
    qc, kc, vc = to_c(q), to_c(k) * (D ** -0.5), to_c(v)
    ic = ig.astype(F32).reshape(B, NC, L, H).transpose(1, 0, 3, 2)
    lfc = jax.nn.log_sigmoid(fg.astype(F32)).reshape(B, NC, L, H).transpose(1, 0, 3, 2)
    causal = jnp.tril(jnp.ones((L, L), dtype=bool))

    def step(carry, xs):
        C, n, m = carry
        qt, kt, vt, it, lft = xs
        b = jnp.cumsum(lft, axis=-1)
        a = b + m[..., None]
        dmat = jnp.where(causal, b[..., :, None] - b[..., None, :] + it[..., None, :], NEG)
        mt = jnp.maximum(a, jnp.max(dmat, axis=-1))
        inter = jnp.exp(a - mt)
        s = jnp.einsum('bhtd,bhsd->bhts', qt, kt) * jnp.exp(dmat - mt[..., None])
        num = jnp.einsum('bhts,bhsd->bhtd', s, vt) + inter[..., None] * jnp.einsum('bhvk,bhtk->bhtv', C, qt)
        den = jnp.sum(s, axis=-1) + inter * jnp.einsum('bhk,bhtk->bht', n, qt)
        h = num / jnp.maximum(jnp.abs(den), jnp.exp(-mt))[..., None]
        m_new = mt[..., -1]
        dec = jnp.exp(a[..., -1] - m_new)
        wsrc = jnp.exp(b[..., -1:] - b + it - m_new[..., None])
        C_new = dec[..., None, None] * C + jnp.einsum('bhs,bhsv,bhsk->bhvk', wsrc, vt, kt)
        n_new = dec[..., None] * n + jnp.einsum('bhs,bhsk->bhk', wsrc, kt)
        return (C_new, n_new, m_new), h

    init = (state[0].astype(F32), state[1].astype(F32), state[2].astype(F32))
    final, hs = lax.scan(step, init, (qc, kc, vc, ic, lfc))
    h = hs.transpose(1, 0, 3, 2, 4).reshape(B, T, H, D)
    return h, final


def head_rmsnorm(h, w):
    B, T, H, D = h.shape
    y = h * lax.rsqrt(jnp.mean(h * h, axis=-1, keepdims=True) + NORM_EPS)
    return y.reshape(B, T, H * D) * w.astype(F32)


def moba_attend(q, kb, vb, kmean, q_pos):
    B, Q, H, D = q.shape
    NB = kb.shape[2]
    qf = q.astype(F32)
    cur = q_pos // MOBA_BLOCK
    s_blk = jnp.einsum('bqhd,bhnd->bhqn', qf, kmean)
    past = jnp.arange(NB)[None, :] < cur[:, None]
    s_blk = jnp.where(past[None, None], s_blk, NEG)
    k_eff = min(MOBA_TOPK, NB)
    _, top_i = lax.top_k(s_blk, k_eff)
    sel_ok = jnp.arange(k_eff)[None, :] < cur[:, None]
    idx = jnp.concatenate([top_i.astype(jnp.int32),
                           jnp.broadcast_to(cur.astype(jnp.int32)[None, None, :, None], (B, H, Q, 1))], axis=-1)
    bi = jnp.arange(B)[:, None, None, None]
    hi = jnp.arange(H)[None, :, None, None]
    k_sel = kb[bi, hi, idx].astype(F32)
    v_sel = vb[bi, hi, idx].astype(F32)
    kpos = idx[..., None] * MOBA_BLOCK + jnp.arange(MOBA_BLOCK)
    is_own = jnp.arange(k_eff + 1) == k_eff
    slot_ok = jnp.concatenate([sel_ok, jnp.ones((Q, 1), dtype=bool)], axis=-1)
    valid = jnp.where(is_own[None, None, None, :, None],
                      kpos <= q_pos[None, None, :, None, None],
                      slot_ok[None, None, :, :, None])
    logits = jnp.einsum('bqhd,bhqskd->bhqsk', qf, k_sel) * (D ** -0.5)
    logits = jnp.where(valid, logits, NEG)
    pr = jax.nn.softmax(logits.reshape(B, H, Q, -1), axis=-1).reshape(logits.shape)
    out = jnp.einsum('bhqsk,bhqskd->bqhd', pr, v_sel)
    return out.astype(q.dtype)


def moba_mixer(q, k_buf, v_buf, q_pos):
    B, L, H, D = k_buf.shape
    Lp = -(-L // MOBA_BLOCK) * MOBA_BLOCK
    pad = ((0, 0), (0, Lp - L), (0, 0), (0, 0))
    kb = jnp.pad(k_buf, pad).reshape(B, Lp // MOBA_BLOCK, MOBA_BLOCK, H, D).transpose(0, 3, 1, 2, 4)
    vb = jnp.pad(v_buf, pad).reshape(B, Lp // MOBA_BLOCK, MOBA_BLOCK, H, D).transpose(0, 3, 1, 2, 4)
    kmean = jnp.mean(kb.astype(F32), axis=3)
    Q = q.shape[1]
    qb = math.gcd(Q, MOBA_QBLOCK)
    nq = Q // qb
    qs = q.reshape(B, nq, qb, H, D).transpose(1, 0, 2, 3, 4)
    ps = q_pos.reshape(nq, qb)
    out = lax.map(lambda a: moba_attend(a[0], kb, vb, kmean, a[1]), (qs, ps))
    return out.transpose(1, 0, 2, 3, 4).reshape(B, Q, H * D)


def token_mixers(h, pos0, pool_prefix, ml_state, kv_past, w_in, w_out, pool_w, pool_scale,
                 sgu_w, sgu_b, ml_gate_b, ml_norm_w):
    B, T, _ = h.shape
    z = jnp.einsum('btd,dn->btn', h, w_in)
    p, su, sv, mq, mk, mv, mo, mi, mf, aq, ak, av = jnp.split(z, _proj_splits(), axis=-1)
    pos = pos0 + jnp.arange(T, dtype=jnp.int32)
    heads = lambda t: t.reshape(B, T, N_HEADS, HEAD_DIM)
    y_pool, pool_buf = pool_mixer(p, pool_prefix, pos, pool_w, pool_scale)
    y_sgu = sgu_mixer(su, sv, sgu_w, sgu_b)
    h_ml, ml_new = mlstm_mixer(heads(mq), heads(mk), heads(mv),
                               mi + ml_gate_b[0], mf + ml_gate_b[1], ml_state)
    y_ml = (jax.nn.sigmoid(mo.astype(F32)) * head_rmsnorm(h_ml, ml_norm_w)).astype(h.dtype)
    q = rope(heads(aq), pos)
    k = rope(heads(ak), pos)
    v = heads(av)
    if kv_past is None:
        k_buf, v_buf = k, v
    else:
        k_buf = jnp.concatenate([kv_past[0].astype(k.dtype), k], axis=1)
        v_buf = jnp.concatenate([kv_past[1].astype(v.dtype), v], axis=1)
    y_at = moba_mixer(q, k_buf, v_buf, pos)
    mixed = jnp.concatenate([y_pool, y_sgu, y_ml, y_at], axis=-1)
    out = jnp.einsum('btm,md->btd', mixed, w_out)
    return out, (k, v, pool_buf, sv, ml_new[0], ml_new[1], ml_new[2])


def swiglu(x, w1, w3, w2):
    return jnp.einsum('...f,fd->...d', jax.nn.silu(x @ w1) * (x @ w3), w2)


def moe_ffn(x, router_w, router_b, w1, w3, w2):
    logits = (x @ router_w).astype(F32) + router_b.astype(F32)
    top_v, top_i = lax.top_k(logits, TOP_K)
    gates = jax.nn.softmax(top_v, axis=-1)
    dense_gate = jnp.sum(jax.nn.one_hot(top_i, N_EXPERTS, dtype=F32) * gates[..., None], axis=-2)
    y = jnp.zeros(x.shape, F32)
    for e in range(N_EXPERTS):
        y = y + dense_gate[..., e:e + 1] * swiglu(x, w1[e], w3[e], w2[e]).astype(F32)
    return y.astype(x.dtype)


def setup_inputs(seed: int = 0) -> dict:
    key = jax.random.key(seed)
    keys = iter(jax.random.split(key, 40))
    nrm = lambda shape, scale: jax.random.normal(next(keys), shape, F32) * scale
    n_pages = PAST_LEN // PAGE_SIZE
    n_used = DEC_BATCH * n_pages
    n_pool = n_used + n_used // 4
    x_prompt = nrm((BATCH, SEQ, D_MODEL), 1.0)
    x_sample = nrm((DEC_BATCH, DEC_SEQ, D_MODEL), 1.0)
    cache_k = nrm((DEPTH, n_pool, PAGE_SIZE, N_HEADS, HEAD_DIM), 1.0)
    cache_v = nrm((DEPTH, n_pool, PAGE_SIZE, N_HEADS, HEAD_DIM), 1.0)
    perm = jax.random.permutation(next(keys), n_pool)
    page_table = perm[:n_used].reshape(DEC_BATCH, n_pages).astype(jnp.int32)
    state_pool = nrm((DEPTH, DEC_BATCH, POOL_BUF, GROUP_W), 1.0)
    state_mlstm_c = nrm((DEPTH, DEC_BATCH, N_HEADS, HEAD_DIM, HEAD_DIM), 0.1)
    state_mlstm_n = jnp.abs(nrm((DEPTH, DEC_BATCH, N_HEADS, HEAD_DIM), 0.5))
    state_mlstm_m = nrm((DEPTH, DEC_BATCH, N_HEADS), 0.5)
    norm_mix_w = 1.0 + nrm((DEPTH, D_MODEL), 0.02)
    norm_ffn_w = 1.0 + nrm((DEPTH, D_MODEL), 0.02)
    final_norm_w = 1.0 + nrm((D_MODEL,), 0.02)
    w_in = nrm((DEPTH, D_MODEL, PROJ_W), D_MODEL ** -0.5)
    w_out = nrm((DEPTH, MIX_W, D_MODEL), MIX_W ** -0.5)
    pool_w = nrm((DEPTH, len(POOL_WINDOWS), POOL_GROUP, POOL_GROUP), POOL_GROUP ** -0.5)
    pool_scale = 1.0 + nrm((DEPTH, GROUP_W), 0.1)
    sgu_w = nrm((DEPTH, N_HEADS, SGU_CHUNK, SGU_CHUNK), SGU_CHUNK ** -0.5)
    sgu_b = 1.0 + nrm((DEPTH, N_HEADS, SGU_CHUNK), 0.02)
    f_bias = jnp.linspace(3.0, 6.0, N_HEADS, dtype=F32)[None, :] + nrm((DEPTH, N_HEADS), 0.1)
    mlstm_gate_b = jnp.stack([nrm((DEPTH, N_HEADS), 0.1), f_bias], axis=1)
    mlstm_norm_w = 1.0 + nrm((DEPTH, GROUP_W), 0.02)
    ffn_w1 = nrm((N_DENSE, D_MODEL, D_FF), D_MODEL ** -0.5)
    ffn_w3 = nrm((N_DENSE, D_MODEL, D_FF), D_MODEL ** -0.5)
    ffn_w2 = nrm((N_DENSE, D_FF, D_MODEL), D_FF ** -0.5)
    router_w = nrm((N_MOE, D_MODEL, N_EXPERTS), D_MODEL ** -0.5)
    router_b = nrm((N_MOE, N_EXPERTS), 0.01)
    moe_w1 = nrm((N_MOE, N_EXPERTS, D_MODEL, D_FF), D_MODEL ** -0.5)
    moe_w3 = nrm((N_MOE, N_EXPERTS, D_MODEL, D_FF), D_MODEL ** -0.5)
    moe_w2 = nrm((N_MOE, N_EXPERTS, D_FF, D_MODEL), D_FF ** -0.5)
    return {'x_prompt': x_prompt, 'x_sample': x_sample, 'cache_k': cache_k, 'cache_v': cache_v,
            'page_table': page_table, 'state_pool': state_pool, 'state_mlstm_c': state_mlstm_c,
            'state_mlstm_n': state_mlstm_n, 'state_mlstm_m': state_mlstm_m,
            'norm_mix_w': norm_mix_w, 'norm_ffn_w': norm_ffn_w, 'final_norm_w': final_norm_w,
            'w_in': w_in, 'w_out': w_out, 'pool_w': pool_w, 'pool_scale': pool_scale,
            'sgu_w': sgu_w, 'sgu_b': sgu_b, 'mlstm_gate_b': mlstm_gate_b, 'mlstm_norm_w': mlstm_norm_w,
            'ffn_w1': ffn_w1, 'ffn_w3': ffn_w3, 'ffn_w2': ffn_w2, 'router_w': router_w,
            'router_b': router_b, 'moe_w1': moe_w1, 'moe_w3': moe_w3, 'moe_w2': moe_w2}


def reference(x_prompt, x_sample, cache_k, cache_v, page_table, state_pool, state_mlstm_c,
              state_mlstm_n, state_mlstm_m, norm_mix_w, norm_ffn_w, final_norm_w, w_in, w_out,
              pool_w, pool_scale, sgu_w, sgu_b, mlstm_gate_b, mlstm_norm_w, ffn_w1, ffn_w3, ffn_w2,
              router_w, router_b, moe_w1, moe_w3, moe_w2):
    Bp = x_prompt.shape[0]
    Bs = x_sample.shape[0]
    past_len = page_table.shape[1] * PAGE_SIZE
    hp, hs = x_prompt, x_sample
    kp_l, vp_l, ks_l, vs_l, poolp_l, pools_l, sgus_l = [], [], [], [], [], [], []
    cp_l, np_l, mp_l, cs_l, ns_l, ms_l = [], [], [], [], [], []
    for l in range(DEPTH):
        lw = (w_in[l], w_out[l], pool_w[l], pool_scale[l], sgu_w[l], sgu_b[l], mlstm_gate_b[l], mlstm_norm_w[l])
        zero_state = (jnp.zeros((Bp, N_HEADS, HEAD_DIM, HEAD_DIM), F32),
                      jnp.zeros((Bp, N_HEADS, HEAD_DIM), F32), jnp.zeros((Bp, N_HEADS), F32))
        yp, stp = token_mixers(rmsnorm(hp, norm_mix_w[l]), 0,
                               jnp.zeros((Bp, POOL_BUF, GROUP_W), hp.dtype), zero_state, None, *lw)
        hp = hp + yp
        k_past = cache_k[l][page_table].reshape(Bs, past_len, N_HEADS, HEAD_DIM)
        v_past = cache_v[l][page_table].reshape(Bs, past_len, N_HEADS, HEAD_DIM)
        ys, sts = token_mixers(rmsnorm(hs, norm_mix_w[l]), past_len, state_pool[l],
                               (state_mlstm_c[l], state_mlstm_n[l], state_mlstm_m[l]),
                               (k_past, v_past), *lw)
        hs = hs + ys
        if l % 2 == 0:
            j = l // 2
            hp = hp + swiglu(rmsnorm(hp, norm_ffn_w[l]), ffn_w1[j], ffn_w3[j], ffn_w2[j])
            hs = hs + swiglu(rmsnorm(hs, norm_ffn_w[l]), ffn_w1[j], ffn_w3[j], ffn_w2[j])
        else:
            j = l // 2
            hp = hp + moe_ffn(rmsnorm(hp, norm_ffn_w[l]), router_w[j], router_b[j], moe_w1[j], moe_w3[j], moe_w2[j])
            hs = hs + moe_ffn(rmsnorm(hs, norm_ffn_w[l]), router_w[j], router_b[j], moe_w1[j], moe_w3[j], moe_w2[j])
        kp_l.append(stp[0]); vp_l.append(stp[1]); poolp_l.append(stp[2])
        cp_l.append(stp[4]); np_l.append(stp[5]); mp_l.append(stp[6])
        ks_l.append(sts[0]); vs_l.append(sts[1]); pools_l.append(sts[2]); sgus_l.append(sts[3])
        cs_l.append(sts[4]); ns_l.append(sts[5]); ms_l.append(sts[6])
    y_prompt = rmsnorm(hp, final_norm_w)
    y_sample = rmsnorm(hs, final_norm_w)
    return (y_prompt, y_sample,
            jnp.stack(kp_l), jnp.stack(vp_l), jnp.stack(ks_l), jnp.stack(vs_l),
            jnp.stack(poolp_l), jnp.stack(pools_l), jnp.stack(sgus_l),
            jnp.stack(cp_l), jnp.stack(np_l), jnp.stack(mp_l),
            jnp.stack(cs_l), jnp.stack(ns_l), jnp.stack(ms_l))
```

```python
import functools

import numpy as np
import jax
import jax.numpy as jnp
from jax import lax
from jax.experimental import pallas as pl
from jax.experimental.pallas import tpu as pltpu

F32 = jnp.float32
BF16 = jnp.bfloat16
I32 = jnp.int32
HIGHEST = lax.Precision.HIGHEST

N_MIXERS = 4
HEAD_DIM = 64
N_HEADS = 4
GROUP_W = N_HEADS * HEAD_DIM
POOL_WINDOWS = (2, 4, 8, 16)
POOL_GROUP = GROUP_W // len(POOL_WINDOWS)
POOL_BUF = max(POOL_WINDOWS) - 1
HALO = POOL_BUF + 1
SGU_CHUNK = 128
MLSTM_CHUNK = 64
MOBA_BLOCK = 256
MOBA_TOPK = 3
PAGE_SIZE = 128
ROPE_THETA = 500000.0
ROT_DIM = HEAD_DIM // 4
TOP_K = 2
NORM_EPS = 1e-6
NEG = -1e30
NEG_PICKED = -3e38

LANE = 128
GATE_W = LANE
Z_W = 10 * GROUP_W + GATE_W
Z_TN = 7 * LANE
COL_P, COL_SU, COL_SV, COL_MQ, COL_MK, COL_MV, COL_MO, COL_AQ, COL_AK, COL_AV = range(10)
COL_GATE = (10 * GROUP_W) // GATE_W
VMEM_LIMIT = 56 * 1024 * 1024


def _params(*sem):
    return pltpu.CompilerParams(dimension_semantics=sem, vmem_limit_bytes=VMEM_LIMIT)


def _row_tile(n, pref):
    return pref if n % pref == 0 else n


def _lane_head(shape, dim):
    return lax.broadcasted_iota(I32, shape, dim) // HEAD_DIM


def _by_head(vals, lh):
    out = vals[N_HEADS - 1]
    for h in range(N_HEADS - 2, -1, -1):
        out = jnp.where(lh == h, vals[h], out)
    return out


def _dot_nt(a, b, precision=None):
    return lax.dot_general(a, b, (((1,), (1,)), ((), ())), precision=precision,
                           preferred_element_type=F32)


def _logsig(x):
    return jnp.minimum(x, 0.0) - jnp.log1p(jnp.exp(-jnp.abs(x)))


def _inproj_kernel(x_ref, nw_ref, w_ref, cos_ref, sa_ref, sb_ref, z_ref, xn_ref):
    j = pl.program_id(1)

    @pl.when(j == 0)
    def _():
        x = x_ref[...]
        ms = jnp.mean(x * x, axis=-1, keepdims=True)
        xn_ref[...] = (x * lax.rsqrt(ms + NORM_EPS) * nw_ref[...]).astype(BF16)

    z = jnp.dot(xn_ref[...], w_ref[...], preferred_element_type=F32)

    @pl.when(j < 2)
    def _():
        z_ref[...] = z

    @pl.when(j == 2)
    def _():
        c, sa, sb = cos_ref[...], sa_ref[...], sb_ref[...]
        half = ROT_DIM // 2
        for o in (0, GROUP_W):
            t = z[:, o:o + GROUP_W]
            z_ref[:, o:o + GROUP_W] = (t * c + pltpu.roll(t, GROUP_W - half, 1) * sa
                                       + pltpu.roll(t, half, 1) * sb)
        z_ref[:, 2 * GROUP_W:] = z[:, 2 * GROUP_W:]


def _inproj(x, nw, w, rope_tabs):
    n, d = x.shape
    tm = _row_tile(n, 512)
    tab_tiles = rope_tabs[0].shape[0] // tm
    tab_spec = pl.BlockSpec((tm, GROUP_W), lambda i, j: (i % tab_tiles, 0))
    return pl.pallas_call(
        _inproj_kernel,
        grid=(n // tm, Z_W // Z_TN),
        in_specs=[pl.BlockSpec((tm, d), lambda i, j: (i, 0)),
                  pl.BlockSpec((1, d), lambda i, j: (0, 0)),
                  pl.BlockSpec((d, Z_TN), lambda i, j: (0, j)),
                  tab_spec, tab_spec, tab_spec],
        out_specs=pl.BlockSpec((tm, Z_TN), lambda i, j: (i, j)),
        out_shape=jax.ShapeDtypeStruct((n, Z_W), F32),
        scratch_shapes=[pltpu.VMEM((tm, d), BF16)],
        compiler_params=_params("parallel", "arbitrary"),
        name="inproj",
    )(x, nw, w, *rope_tabs)


def _rope_tables(pos):
    half = ROT_DIM // 2
    inv = 1.0 / (ROPE_THETA ** (jnp.arange(half, dtype=F32) / half))
    ang = pos.astype(F32)[:, None] * inv[None, :]
    cos, sin = jnp.cos(ang), jnp.sin(ang)
    t = pos.shape[0]
    rest = HEAD_DIM - ROT_DIM
    one = jnp.ones((t, rest), F32)
    zero = jnp.zeros((t, rest), F32)
    zh = jnp.zeros((t, half), F32)
    c = jnp.concatenate([cos, cos, one], axis=1)
    sa = jnp.concatenate([-sin, zh, zero], axis=1)
    sb = jnp.concatenate([zh, sin, zero], axis=1)
    tile = lambda a: jnp.tile(a, (1, N_HEADS))
    return tile(c), tile(sa), tile(sb)


def _window_of_lane(shape, dim, lo):
    g = (lax.broadcasted_iota(I32, shape, dim) + lo) // POOL_GROUP
    w = jnp.full(shape, POOL_WINDOWS[-1], I32)
    for i in range(len(POOL_WINDOWS) - 2, -1, -1):
        w = jnp.where(g == i, POOL_WINDOWS[i], w)
    return w


def _poolsgu_kernel(p_ref, su_ref, sv_ref, pw_ref, ps_ref, sw_ref, sb_ref, o_ref, ebuf):
    t = pl.program_id(1)
    tt = p_ref.shape[0]

    @pl.when(t == 0)
    def _():
        ebuf[0:HALO, :] = jnp.zeros((HALO, GROUP_W), F32)

    @pl.when(t > 0)
    def _():
        ebuf[0:HALO, :] = ebuf[tt:tt + HALO, :]

    ebuf[HALO:HALO + tt, :] = p_ref[...]

    pos1 = lax.broadcasted_iota(I32, (tt, LANE), 0) + t * tt + 1
    halves = []
    for half in range(2):
        lo = half * LANE
        wsmall, wbig = POOL_WINDOWS[2 * half], POOL_WINDOWS[2 * half + 1]
        e0 = ebuf[HALO:HALO + tt, lo:lo + LANE]
        acc = e0
        for off in range(1, wsmall):
            acc = acc + ebuf[HALO - off:HALO - off + tt, lo:lo + LANE]
        small = acc
        for off in range(wsmall, wbig):
            acc = acc + ebuf[HALO - off:HALO - off + tt, lo:lo + LANE]
        first = lax.broadcasted_iota(I32, (tt, LANE), 1) < POOL_GROUP
        wsum = jnp.where(first, small, acc)
        cnt = jnp.minimum(pos1, _window_of_lane((tt, LANE), 1, lo)).astype(F32)
        halves.append(wsum / cnt - e0)
    d = jnp.concatenate(halves, axis=1).astype(BF16)
    y_pool = jnp.dot(d, pw_ref[...], preferred_element_type=F32) * ps_ref[...]
    o_ref[:, 0:GROUP_W] = y_pool.astype(o_ref.dtype)

    row = lax.broadcasted_iota(I32, (SGU_CHUNK, SGU_CHUNK), 0)
    col = lax.broadcasted_iota(I32, (SGU_CHUNK, SGU_CHUNK), 1)
    ws = [jnp.where(row >= col, sw_ref[h], 0.0).astype(BF16) for h in range(N_HEADS)]
    lh = _lane_head((SGU_CHUNK, GROUP_W), 1)
    for c in range(tt // SGU_CHUNK):
        rows = slice(c * SGU_CHUNK, (c + 1) * SGU_CHUNK)
        v = sv_ref[rows, :].astype(BF16)
        mixed = _by_head([jnp.dot(ws[h], v, preferred_element_type=F32) for h in range(N_HEADS)], lh)
        y = su_ref[rows, :] * (mixed + sb_ref[...])
        o_ref[rows, GROUP_W:2 * GROUP_W] = y.astype(o_ref.dtype)


def _poolsgu(z, batch, seq, pool_wbd, pool_scale, sgu_w, sgu_bias):
    tt = _row_tile(seq, 512)
    tps = seq // tt
    zspec = lambda c: pl.BlockSpec((tt, GROUP_W), lambda b, t, c=c: (b * tps + t, c))
    return pl.pallas_call(
        _poolsgu_kernel,
        grid=(batch, tps),
        in_specs=[zspec(COL_P), zspec(COL_SU), zspec(COL_SV),
                  pl.BlockSpec((GROUP_W, GROUP_W), lambda b, t: (0, 0)),
                  pl.BlockSpec((1, GROUP_W), lambda b, t: (0, 0)),
                  pl.BlockSpec((N_HEADS, SGU_CHUNK, SGU_CHUNK), lambda b, t: (0, 0, 0)),
                  pl.BlockSpec((SGU_CHUNK, GROUP_W), lambda b, t: (0, 0))],
        out_specs=pl.BlockSpec((tt, 2 * GROUP_W), lambda b, t: (b * tps + t, 0)),
        out_shape=jax.ShapeDtypeStruct((batch * seq, 2 * GROUP_W), BF16),
        scratch_shapes=[pltpu.VMEM((HALO + tt, GROUP_W), F32)],
        compiler_params=_params("parallel", "arbitrary"),
        name="pool_sgu",
    )(z, z, z, pool_wbd, pool_scale, sgu_w, sgu_bias)


def _mlstm_kernel(q_ref, k_ref, v_ref, o_ref, g_ref, gt_ref, gbc_ref, gbr_ref, nw_ref,
                  y_ref, ct_ref, n_ref, m_ref):
    L = MLSTM_CHUNK

    @pl.when(pl.program_id(1) == 0)
    def _():
        ct_ref[...] = jnp.zeros(ct_ref.shape, F32)
        n_ref[...] = jnp.zeros(n_ref.shape, F32)
        m_ref[...] = jnp.zeros(m_ref.shape, F32)

    r_i = lax.broadcasted_iota(I32, (L, L), 0)
    c_i = lax.broadcasted_iota(I32, (L, L), 1)
    causal = r_i >= c_i
    tri_l = causal.astype(F32)
    tri_u = (r_i <= c_i).astype(F32)
    lh = _lane_head((L, GROUP_W), 1)
    bd = _lane_head((GROUP_W, GROUP_W), 0) == _lane_head((GROUP_W, GROUP_W), 1)
    same_head = bd.astype(F32)

    for c in range(q_ref.shape[0] // L):
        rows = slice(c * L, (c + 1) * L)
        q = q_ref[rows, :]
        ks = k_ref[rows, :] * (HEAD_DIM ** -0.5)
        v = v_ref[rows, :]
        g = g_ref[rows, :] + gbc_ref[...]
        gt = gt_ref[:, rows] + gbr_ref[...]
        bcol = jnp.dot(tri_l, _logsig(g), precision=HIGHEST, preferred_element_type=F32)
        brow = jnp.dot(_logsig(gt), tri_u, precision=HIGHEST, preferred_element_type=F32)
        qb, kb, vb = q.astype(BF16), ks.astype(BF16), v.astype(BF16)

        s_l, inter_l, deni_l, emt_l, wsrc_l, dec_l, mnew_l = [], [], [], [], [], [], []
        for h in range(N_HEADS):
            bc = bcol[:, N_HEADS + h:N_HEADS + h + 1]
            ic = g[:, h:h + 1]
            br = brow[N_HEADS + h:N_HEADS + h + 1, :]
            ir = gt[h:h + 1, :]
            dm = jnp.where(causal, bc - br + ir, NEG)
            a = bc + m_ref[h:h + 1, 0:1]
            mt = jnp.maximum(a, jnp.max(dm, axis=1, keepdims=True))
            qm = jnp.where(lh == h, q, 0.0).astype(BF16)
            s = _dot_nt(qm, kb) * jnp.exp(dm - mt)
            m_new = mt[L - 1:L, :]
            s_l.append(s)
            inter_l.append(jnp.exp(a - mt))
            deni_l.append(jnp.sum(s, axis=1, keepdims=True))
            emt_l.append(jnp.exp(-mt))
            wsrc_l.append(jnp.exp(bc[L - 1:L, :] - bc + ic - m_new))
            dec_l.append(jnp.exp(a[L - 1:L, :] - m_new))
            mnew_l.append(m_new)

        ct = ct_ref[...]
        nrow = n_ref[...]
        r = jnp.dot(jnp.concatenate(s_l, axis=0).astype(BF16), vb, preferred_element_type=F32)
        num_i = _by_head([r[h * L:(h + 1) * L, :] for h in range(N_HEADS)], lh)
        q_c = jnp.dot(qb, ct.astype(BF16), preferred_element_type=F32)
        q_n = jnp.dot(q * nrow, same_head, precision=HIGHEST, preferred_element_type=F32)
        inter = _by_head(inter_l, lh)
        num = num_i + inter * q_c
        den = _by_head(deni_l, lh) + inter * q_n
        hval = num / jnp.maximum(jnp.abs(den), _by_head(emt_l, lh))
        ms = jnp.dot(hval * hval, same_head, precision=HIGHEST,
                     preferred_element_type=F32) * (1.0 / HEAD_DIM)
        y = hval * lax.rsqrt(ms + NORM_EPS) * nw_ref[...]
        y_ref[rows, :] = (jax.nn.sigmoid(o_ref[rows, :]) * y).astype(y_ref.dtype)

        kw = ks * _by_head(wsrc_l, lh)
        dec = _by_head(dec_l, lh[0:1, :])
        upd = jnp.dot(kw.T.astype(BF16), vb, preferred_element_type=F32)
        ct_ref[...] = ct * dec + jnp.where(bd, upd, 0.0)
        n_ref[...] = nrow * dec + jnp.sum(kw, axis=0, keepdims=True)
        for h in range(N_HEADS):
            m_ref[h:h + 1, :] = jnp.broadcast_to(mnew_l[h], (1, m_ref.shape[1]))


def _mlstm(z, gates_t, batch, seq, gate_b, norm_w):
    rows = _row_tile(seq, 4 * MLSTM_CHUNK)
    tps = seq // rows
    zspec = lambda c: pl.BlockSpec((rows, GROUP_W), lambda b, t, c=c: (b * tps + t, c))
    gbc = jnp.zeros((1, GATE_W), F32).at[0, :2 * N_HEADS].set(gate_b.reshape(-1))
    gbr = gate_b.reshape(2 * N_HEADS, 1)
    state = lambda r, c: (pl.BlockSpec((None, r, c), lambda b, t: (b, 0, 0)),
                          jax.ShapeDtypeStruct((batch, r, c), F32))
    (cs, csh), (ns, nsh), (msp, msh) = state(GROUP_W, GROUP_W), state(1, GROUP_W), state(8, LANE)
    return pl.pallas_call(
        _mlstm_kernel,
        grid=(batch, tps),
        in_specs=[zspec(COL_MQ), zspec(COL_MK), zspec(COL_MV), zspec(COL_MO),
                  pl.BlockSpec((rows, GATE_W), lambda b, t: (b * tps + t, COL_GATE)),
                  pl.BlockSpec((2 * N_HEADS, rows), lambda b, t: (0, b * tps + t)),
                  pl.BlockSpec((1, GATE_W), lambda b, t: (0, 0)),
                  pl.BlockSpec((2 * N_HEADS, 1), lambda b, t: (0, 0)),
                  pl.BlockSpec((1, GROUP_W), lambda b, t: (0, 0))],
        out_specs=[pl.BlockSpec((rows, GROUP_W), lambda b, t: (b * tps + t, 0)), cs, ns, msp],
        out_shape=[jax.ShapeDtypeStruct((batch * seq, GROUP_W), BF16), csh, nsh, msh],
        compiler_params=_params("parallel", "arbitrary"),
        name="mlstm",
    )(z, z, z, z, z, gates_t, gbc, gbr, norm_w)


def _kmean_kernel(k_ref, o_ref):
    nb = o_ref.shape[0]
    k = k_ref[...].reshape(nb, MOBA_BLOCK, GROUP_W)
    o_ref[...] = jnp.sum(k, axis=1) * (1.0 / MOBA_BLOCK)


def _kmean(z, nblocks):
    nb = _row_tile(nblocks, 8)
    return pl.pallas_call(
        _kmean_kernel,
        grid=(nblocks // nb,),
        in_specs=[pl.BlockSpec((nb * MOBA_BLOCK, GROUP_W), lambda i: (i, COL_AK))],
        out_specs=pl.BlockSpec((nb, GROUP_W), lambda i: (i, 0)),
        out_shape=jax.ShapeDtypeStruct((nblocks, GROUP_W), F32),
        compiler_params=_params("parallel"),
        name="kmean",
    )(z)


def _pick_top_blocks(s, n_valid_mask):
    blk = lax.broadcasted_iota(I32, s.shape, 0)
    s = jnp.where(n_valid_mask, s, NEG)
    bias = jnp.full(s.shape, NEG, F32)
    for _ in range(MOBA_TOPK):
        mx = jnp.max(s, axis=0, keepdims=True)
        idx = jnp.min(jnp.where(s == mx, blk, s.shape[0]), axis=0, keepdims=True)
        pick = blk == idx
        bias = jnp.where(jnp.logical_and(pick, mx > 0.5 * NEG), 0.0, bias)
        s = jnp.where(pick, NEG_PICKED, s)
    return bias


def _moba_kernel(q_ref, kb_ref, vt_ref, km_ref, o_ref, qm_sc, sel_sc, m_sc, l_sc, acc_sc):
    i = pl.program_id(1)
    nb = km_ref.shape[0]
    nbp = sel_sc.shape[0] // N_HEADS
    B = MOBA_BLOCK
    scale = HEAD_DIM ** -0.5
    q = q_ref[...]
    km = km_ref[...]
    lh = _lane_head((B, GROUP_W), 1)
    past = lax.broadcasted_iota(I32, (nb, B), 0) < i
    key_le_query = lax.broadcasted_iota(I32, (B, B), 0) <= lax.broadcasted_iota(I32, (B, B), 1)

    k_own = kb_ref[i]
    vt_own = vt_ref[i]
    for h in range(N_HEADS):
        qm = jnp.where(lh == h, q, 0.0)
        s_blk = _dot_nt(km, qm, precision=HIGHEST)
        sel_sc[h * nbp:h * nbp + nb, :] = _pick_top_blocks(s_blk, past)
        qmb = qm.astype(BF16)
        qm_sc[h] = qmb
        lt = jnp.where(key_le_query, _dot_nt(k_own, qmb) * scale, NEG)
        m = jnp.max(lt, axis=0, keepdims=True)
        p = jnp.exp(lt - m)
        m_sc[h:h + 1, :] = m
        l_sc[h:h + 1, :] = jnp.sum(p, axis=0, keepdims=True)
        hs = slice(h * HEAD_DIM, (h + 1) * HEAD_DIM)
        acc_sc[hs, :] = jnp.dot(vt_own[hs, :], p.astype(BF16), preferred_element_type=F32)

    def body(n, carry):
        k_n = kb_ref[n]
        vt_n = vt_ref[n]
        for h in range(N_HEADS):
            lt = _dot_nt(k_n, qm_sc[h]) * scale + sel_sc[pl.ds(h * nbp + n, 1), :]
            m_old = m_sc[h:h + 1, :]
            m_new = jnp.maximum(m_old, jnp.max(lt, axis=0, keepdims=True))
            alpha = jnp.exp(m_old - m_new)
            p = jnp.exp(lt - m_new)
            l_sc[h:h + 1, :] = alpha * l_sc[h:h + 1, :] + jnp.sum(p, axis=0, keepdims=True)
            m_sc[h:h + 1, :] = m_new
            hs = slice(h * HEAD_DIM, (h + 1) * HEAD_DIM)
            acc_sc[hs, :] = alpha * acc_sc[hs, :] + jnp.dot(vt_n[hs, :], p.astype(BF16),
                                                            preferred_element_type=F32)
        return carry

    lax.fori_loop(0, i, body, 0)

    for h in range(N_HEADS):
        hs = slice(h * HEAD_DIM, (h + 1) * HEAD_DIM)
        acc_sc[hs, :] = acc_sc[hs, :] / l_sc[h:h + 1, :]
    o_ref[...] = acc_sc[...].T.astype(o_ref.dtype)


def _moba(z, kb4, vt4, kmean, batch, nb):
    nbp = -(-nb // 8) * 8
    B = MOBA_BLOCK
    blk4 = pl.BlockSpec((None, nb, B, GROUP_W), lambda b, i: (b, 0, 0, 0))
    return pl.pallas_call(
        _moba_kernel,
        grid=(batch, nb),
        in_specs=[pl.BlockSpec((B, GROUP_W), lambda b, i: (b * nb + i, COL_AQ)),
                  blk4, blk4,
                  pl.BlockSpec((None, nb, GROUP_W), lambda b, i: (b, 0, 0))],
        out_specs=pl.BlockSpec((B, GROUP_W), lambda b, i: (b * nb + i, 0)),
        out_shape=jax.ShapeDtypeStruct((batch * nb * B, GROUP_W), BF16),
        scratch_shapes=[pltpu.VMEM((N_HEADS, B, GROUP_W), BF16),
                        pltpu.VMEM((N_HEADS * nbp, B), F32),
                        pltpu.VMEM((8, B), F32), pltpu.VMEM((8, B), F32),
                        pltpu.VMEM((GROUP_W, B), F32)],
        compiler_params=_params("parallel", "arbitrary"),
        name="moba",
    )(z, kb4, vt4, kmean)


def _outproj_kernel(x_ref, ma_ref, mc_ref, md_ref, w_ref, o_ref):
    g2 = 2 * GROUP_W
    acc = jnp.dot(ma_ref[...], w_ref[0:g2, :], preferred_element_type=F32)
    acc += jnp.dot(mc_ref[...], w_ref[g2:g2 + GROUP_W, :], preferred_element_type=F32)
    acc += jnp.dot(md_ref[...], w_ref[g2 + GROUP_W:, :], preferred_element_type=F32)
    o_ref[...] = x_ref[...] + acc


def _outproj(x, ma, mc, md, w):
    n, d = x.shape
    tm = _row_tile(n, 512)
    rows = lambda c: pl.BlockSpec((tm, c), lambda i: (i, 0))
    return pl.pallas_call(
        _outproj_kernel,
        grid=(n // tm,),
        in_specs=[rows(d), rows(2 * GROUP_W), rows(GROUP_W), rows(GROUP_W),
                  pl.BlockSpec(w.shape, lambda i: (0, 0))],
        out_specs=rows(d),
        out_shape=jax.ShapeDtypeStruct((n, d), F32),
        compiler_params=_params("parallel"),
        name="outproj",
    )(x, ma, mc, md, w)


def _ff_tile(d_ff):
    return d_ff // 2 if (d_ff // 2) % LANE == 0 else d_ff


def _swiglu_partial(xn, w1_ref, w3_ref, w2_ref):
    a = jnp.dot(xn, w1_ref[...], preferred_element_type=F32)
    b = jnp.dot(xn, w3_ref[...], preferred_element_type=F32)
    hmid = (a * jax.nn.sigmoid(a) * b).astype(BF16)
    return jnp.dot(hmid, w2_ref[...], preferred_element_type=F32)


def _ffn_kernel(x_ref, nw_ref, w1_ref, w3_ref, w2_ref, o_ref, xn_ref, acc_ref):
    j = pl.program_id(1)

    @pl.when(j == 0)
    def _():
        x = x_ref[...]
        ms = jnp.mean(x * x, axis=-1, keepdims=True)
        xn_ref[...] = (x * lax.rsqrt(ms + NORM_EPS) * nw_ref[...]).astype(BF16)
        acc_ref[...] = jnp.zeros(acc_ref.shape, F32)

    acc_ref[...] += _swiglu_partial(xn_ref[...], w1_ref, w3_ref, w2_ref)

    @pl.when(j == pl.num_programs(1) - 1)
    def _():
        o_ref[...] = x_ref[...] + acc_ref[...]


def _ffn(x, nw, w1, w3, w2):
    n, d = x.shape
    d_ff = w1.shape[1]
    tm = _row_tile(n, 512)
    tf = _ff_tile(d_ff)
    return pl.pallas_call(
        _ffn_kernel,
        grid=(n // tm, d_ff // tf),
        in_specs=[pl.BlockSpec((tm, d), lambda i, j: (i, 0)),
                  pl.BlockSpec((1, d), lambda i, j: (0, 0)),
                  pl.BlockSpec((d, tf), lambda i, j: (0, j)),
                  pl.BlockSpec((d, tf), lambda i, j: (0, j)),
                  pl.BlockSpec((tf, d), lambda i, j: (j, 0))],
        out_specs=pl.BlockSpec((tm, d), lambda i, j: (i, 0)),
        out_shape=jax.ShapeDtypeStruct((n, d), F32),
        scratch_shapes=[pltpu.VMEM((tm, d), BF16), pltpu.VMEM((tm, d), F32)],
        compiler_params=_params("parallel", "arbitrary"),
        name="ffn_dense",
    )(x, nw, w1, w3, w2)


def _router_kernel(x_ref, nw_ref, rw_ref, rb_ref, xp_ref, g_ref, e_ref):
    x = x_ref[...]
    ms = jnp.mean(x * x, axis=-1, keepdims=True)
    xn = x * lax.rsqrt(ms + NORM_EPS) * nw_ref[...]
    half = x.shape[1] // 2
    bits = pltpu.bitcast(xn.astype(BF16).astype(F32), jnp.uint32)
    xp_ref[...] = jnp.bitwise_or(jnp.right_shift(bits[:, :half], jnp.uint32(16)), bits[:, half:])

    logits = jnp.dot(xn, rw_ref[...], precision=HIGHEST, preferred_element_type=F32) + rb_ref[...]
    lane = lax.broadcasted_iota(I32, logits.shape, 1)
    mx1 = jnp.max(logits, axis=1, keepdims=True)
    i1 = jnp.min(jnp.where(logits == mx1, lane, LANE), axis=1, keepdims=True)
    rest = jnp.where(lane == i1, -jnp.inf, logits)
    mx2 = jnp.max(rest, axis=1, keepdims=True)
    i2 = jnp.min(jnp.where(rest == mx2, lane, LANE), axis=1, keepdims=True)
    e2 = jnp.exp(mx2 - mx1)
    g1 = 1.0 / (1.0 + e2)
    g_ref[...] = jnp.where(lane == 0, g1, jnp.where(lane == 1, e2 * g1, 0.0))
    e_ref[...] = jnp.where(lane == 0, i1, jnp.where(lane == 1, i2, 0))


def _router(x, nw, rw, rb):
    n, d = x.shape
    tm = _row_tile(n, 512)
    rows = lambda c: pl.BlockSpec((tm, c), lambda i: (i, 0))
    full = lambda a: pl.BlockSpec(a.shape, lambda i: (0, 0))
    return pl.pallas_call(
        _router_kernel,
        grid=(n // tm,),
        in_specs=[rows(d), full(nw), full(rw), full(rb)],
        out_specs=[rows(d // 2), rows(LANE), rows(LANE)],
        out_shape=[jax.ShapeDtypeStruct((n, d // 2), jnp.uint32),
                   jax.ShapeDtypeStruct((n, LANE), F32),
                   jax.ShapeDtypeStruct((n, LANE), I32)],
        compiler_params=_params("parallel"),
        name="router",
    )(x, nw, rw, rb)


def _unpack_rows(words):
    lo = pltpu.bitcast(jnp.left_shift(words, jnp.uint32(16)), F32)
    hi = pltpu.bitcast(jnp.bitwise_and(words, jnp.uint32(0xFFFF0000)), F32)
    return jnp.concatenate([lo, hi], axis=1).astype(BF16)


def _gmm_kernel(te_ref, src_ref, nused_ref, xp_hbm, w1_ref, w3_ref, w2_ref, o_ref,
                xbuf, sem, xn_ref, acc_ref):
    i = pl.program_id(0)
    j = pl.program_id(1)
    ntiles = pl.num_programs(0)
    tm = xbuf.shape[1]
    slot = i % 2

    def row_copy(tile, sl, r):
        tok = src_ref[tile * tm + r]
        return pltpu.make_async_copy(xp_hbm.at[pl.ds(tok, 1), :], xbuf.at[sl, pl.ds(r, 1), :], sem.at[sl])

    def issue(tile, sl):
        def body(r, carry):
            row_copy(tile, sl, r).start()
            return carry
        lax.fori_loop(0, tm, body, 0, unroll=8)

    @pl.when(j == 0)
    def _():
        @pl.when(i == 0)
        def _():
            issue(0, 0)

        @pl.when(i + 1 < ntiles)
        def _():
            issue(i + 1, 1 - slot)

        pltpu.make_async_copy(xbuf.at[slot], xbuf.at[slot], sem.at[slot]).wait()
        xn_ref[...] = _unpack_rows(xbuf[slot])
        acc_ref[...] = jnp.zeros(acc_ref.shape, F32)

    @pl.when(i < nused_ref[0])
    def _():
        acc_ref[...] += _swiglu_partial(xn_ref[...], w1_ref, w3_ref, w2_ref)

    @pl.when(j == pl.num_programs(1) - 1)
    def _():
        o_ref[...] = acc_ref[...]


def _gmm(xp, tile_expert, src, nused, w1, w3, w2, tm):
    ncap = src.shape[0]
    half = xp.shape[1]
    d = 2 * half
    d_ff = w1.shape[2]
    tf = _ff_tile(d_ff)
    grid_spec = pltpu.PrefetchScalarGridSpec(
        num_scalar_prefetch=3,
        grid=(ncap // tm, d_ff // tf),
        in_specs=[pl.BlockSpec(memory_space=pl.ANY),
                  pl.BlockSpec((None, d, tf), lambda i, j, te, s, nu: (te[i], 0, j)),
                  pl.BlockSpec((None, d, tf), lambda i, j, te, s, nu: (te[i], 0, j)),
                  pl.BlockSpec((None, tf, d), lambda i, j, te, s, nu: (te[i], j, 0))],
        out_specs=pl.BlockSpec((tm, d), lambda i, j, te, s, nu: (i, 0)),
        scratch_shapes=[pltpu.VMEM((2, tm, half), jnp.uint32),
                        pltpu.SemaphoreType.DMA((2,)),
                        pltpu.VMEM((tm, d), BF16),
                        pltpu.VMEM((tm, d), F32)])
    return pl.pallas_call(
        _gmm_kernel,
        grid_spec=grid_spec,
        out_shape=jax.ShapeDtypeStruct((ncap, d), F32),
        compiler_params=_params("arbitrary", "arbitrary"),
        name="moe_gmm",
    )(tile_expert, src, nused, xp, w1, w3, w2)


def _dispatch_plan(e_idx, n_experts, tm):
    n = e_idx.shape[0]
    ef = e_idx[:, :TOP_K].T.reshape(-1)
    onehot = (ef[:, None] == jnp.arange(n_experts, dtype=I32)[None, :]).astype(I32)
    rank = jnp.sum((jnp.cumsum(onehot, axis=0) - 1) * onehot, axis=1)
    counts = jnp.sum(onehot, axis=0)
    padded = ((counts + tm - 1) // tm) * tm
    ends = jnp.cumsum(padded)
    pos = (ends - padded)[ef] + rank
    ncap = TOP_K * n + n_experts * tm
    src = jnp.zeros((ncap,), I32).at[pos].set(jnp.arange(TOP_K * n, dtype=I32) % n)
    tile_start = jnp.arange(ncap // tm, dtype=I32) * tm
    tile_expert = jnp.minimum(jnp.searchsorted(ends, tile_start, side="right"), n_experts - 1).astype(I32)
    nused = (ends[-1] // tm).astype(I32).reshape(1)
    return pos.astype(I32), src, tile_expert, nused


def _combine_kernel(pos_ref, h_ref, g_ref, o_hbm, fw_ref, y_ref, obuf, sem, *, final_norm):
    i = pl.program_id(0)
    ntiles = pl.num_programs(0)
    tc = h_ref.shape[0]
    n = ntiles * tc
    slot = i % 2

    def issue(tile, sl):
        def body(r, carry):
            for k in range(TOP_K):
                p = pos_ref[k * n + tile * tc + r]
                pltpu.make_async_copy(o_hbm.at[pl.ds(p, 1), :], obuf.at[sl, k, pl.ds(r, 1), :],
                                      sem.at[sl]).start()
            return carry
        lax.fori_loop(0, tc, body, 0, unroll=8)

    @pl.when(i == 0)
    def _():
        issue(0, 0)

    @pl.when(i + 1 < ntiles)
    def _():
        issue(i + 1, 1 - slot)

    pltpu.make_async_copy(obuf.at[slot], obuf.at[slot], sem.at[slot]).wait()
    g = g_ref[...]
    y = h_ref[...] + g[:, 0:1] * obuf[slot, 0] + g[:, 1:2] * obuf[slot, 1]
    if final_norm:
        ms = jnp.mean(y * y, axis=-1, keepdims=True)
        y = y * lax.rsqrt(ms + NORM_EPS) * fw_ref[...]
    y_ref[...] = y


def _combine(pos, h, gates, o_sorted, fw, final_norm):
    n, d = h.shape
    tc = _row_tile(n, 256)
    grid_spec = pltpu.PrefetchScalarGridSpec(
        num_scalar_prefetch=1,
        grid=(n // tc,),
        in_specs=[pl.BlockSpec((tc, d), lambda i, p: (i, 0)),
                  pl.BlockSpec((tc, LANE), lambda i, p: (i, 0)),
                  pl.BlockSpec(memory_space=pl.ANY),
                  pl.BlockSpec((1, d), lambda i, p: (0, 0))],
        out_specs=pl.BlockSpec((tc, d), lambda i, p: (i, 0)),
        scratch_shapes=[pltpu.VMEM((2, TOP_K, tc, d), F32), pltpu.SemaphoreType.DMA((2,))])
    return pl.pallas_call(
        functools.partial(_combine_kernel, final_norm=final_norm),
        grid_spec=grid_spec,
        out_shape=jax.ShapeDtypeStruct((n, d), F32),
        compiler_params=_params("arbitrary"),
        name="moe_combine",
    )(pos, h, gates, o_sorted, fw)


def _moe_small_kernel(x_ref, nw_ref, g_ref, e_ref, w1_ref, w3_ref, w2_ref, fw_ref, y_ref,
                      xn_ref, acc_ref, *, final_norm):
    e = pl.program_id(0)
    j = pl.program_id(1)

    @pl.when(jnp.logical_and(e == 0, j == 0))
    def _():
        x = x_ref[...]
        ms = jnp.mean(x * x, axis=-1, keepdims=True)
        xn_ref[...] = (x * lax.rsqrt(ms + NORM_EPS) * nw_ref[...]).astype(BF16)
        acc_ref[...] = jnp.zeros(acc_ref.shape, F32)

    g = g_ref[...]
    idx = e_ref[...]
    gate = (jnp.where(idx[:, 0:1] == e, g[:, 0:1], 0.0) + jnp.where(idx[:, 1:2] == e, g[:, 1:2], 0.0))
    acc_ref[...] += gate * _swiglu_partial(xn_ref[...], w1_ref, w3_ref, w2_ref)

    @pl.when(jnp.logical_and(e == pl.num_programs(0) - 1, j == pl.num_programs(1) - 1))
    def _():
        y = x_ref[...] + acc_ref[...]
        if final_norm:
            ms = jnp.mean(y * y, axis=-1, keepdims=True)
            y = y * lax.rsqrt(ms + NORM_EPS) * fw_ref[...]
        y_ref[...] = y


def _moe_small(x, nw, gates, e_idx, w1, w3, w2, fw, final_norm):
    n, d = x.shape
    n_experts, _, d_ff = w1.shape
    tf = _ff_tile(d_ff)
    full = lambda a: pl.BlockSpec(a.shape, lambda e, j: (0, 0))
    return pl.pallas_call(
        functools.partial(_moe_small_kernel, final_norm=final_norm),
        grid=(n_experts, d_ff // tf),
        in_specs=[full(x), full(nw), full(gates), full(e_idx),
                  pl.BlockSpec((None, d, tf), lambda e, j: (e, 0, j)),
                  pl.BlockSpec((None, d, tf), lambda e, j: (e, 0, j)),
                  pl.BlockSpec((None, tf, d), lambda e, j: (e, j, 0)),
                  full(fw)],
        out_specs=full(x),
        out_shape=jax.ShapeDtypeStruct((n, d), F32),
        scratch_shapes=[pltpu.VMEM((n, d), BF16), pltpu.VMEM((n, d), F32)],
        compiler_params=_params("arbitrary", "arbitrary"),
        name="moe_small",
    )(x, nw, gates, e_idx, w1, w3, w2, fw)


def _norm_kernel(x_ref, w_ref, o_ref):
    x = x_ref[...]
    ms = jnp.mean(x * x, axis=-1, keepdims=True)
    o_ref[...] = x * lax.rsqrt(ms + NORM_EPS) * w_ref[...]


def _final_norm(x, w):
    n, d = x.shape
    tm = _row_tile(n, 512)
    return pl.pallas_call(
        _norm_kernel,
        grid=(n // tm,),
        in_specs=[pl.BlockSpec((tm, d), lambda i: (i, 0)), pl.BlockSpec((1, d), lambda i: (0, 0))],
        out_specs=pl.BlockSpec((tm, d), lambda i: (i, 0)),
        out_shape=jax.ShapeDtypeStruct((n, d), F32),
        compiler_params=_params("parallel"),
        name="final_norm",
    )(x, w)


def _smix_kernel(p_ref, pf_ref, pw_ref, ps_ref, su_ref, sv_ref, sw0_ref, sb0_ref,
                 q_ref, k_ref, v_ref, ig_ref, fg_ref, c_ref, n_ref, m_ref, mo_ref, nw_ref,
                 ypool_ref, ysgu_ref, yml_ref, cn_ref, nn_ref, mn_ref, *, cnt):
    p = p_ref[...]
    lane_w = _window_of_lane(p.shape, 1, 0)
    acc = p
    wsum = jnp.zeros(p.shape, F32)
    for off in range(1, POOL_WINDOWS[-1] + 1):
        if off in POOL_WINDOWS:
            wsum = jnp.where(lane_w == off, acc, wsum)
        if off <= POOL_BUF:
            acc = acc + pf_ref[HALO - off]
    count = jnp.minimum(lane_w, cnt).astype(F32)
    d = (wsum / count - p).astype(BF16)
    ypool_ref[...] = jnp.dot(d, pw_ref[...], preferred_element_type=F32) * ps_ref[...]

    ysgu_ref[...] = su_ref[...] * (sw0_ref[...] * sv_ref[...] + sb0_ref[...])

    q = q_ref[...]
    ks = k_ref[...] * (HEAD_DIM ** -0.5)
    v = v_ref[...]
    ig = ig_ref[...]
    c = c_ref[...]
    nrow = n_ref[...]
    a = _logsig(fg_ref[...]) + m_ref[...]
    mt = jnp.maximum(a, ig)
    inter = jnp.exp(a - mt)
    e_i = jnp.exp(ig - mt)
    s = jnp.sum(q * ks, axis=-1, keepdims=True) * e_i
    c_q = jnp.sum(c * q, axis=-1, keepdims=True)
    n_q = jnp.sum(nrow * q, axis=-1, keepdims=True)
    num = s * v + inter * c_q
    den = s + inter * n_q
    hval = num / jnp.maximum(jnp.abs(den), jnp.exp(-mt))
    cn_ref[...] = inter * c + (e_i * v) * ks
    nn_ref[...] = inter * nrow + e_i * ks
    mn_ref[...] = mt
    ms = jnp.mean(hval * hval, axis=1, keepdims=True)
    y = hval * lax.rsqrt(ms + NORM_EPS) * nw_ref[...]
    yml_ref[...] = jax.nn.sigmoid(mo_ref[...]) * y


def _smix(args, out_shapes, cnt):
    return pl.pallas_call(
        functools.partial(_smix_kernel, cnt=cnt),
        out_shape=out_shapes,
        compiler_params=pltpu.CompilerParams(vmem_limit_bytes=VMEM_LIMIT),
        name="sample_mixers",
    )(*args)


PAGES_PER_STEP = 16
PAGES_PER_BLOCK = MOBA_BLOCK // PAGE_SIZE


def _paged_kmean_kernel(pt_ref, *refs):
    o_ref = refs[-1]
    for bb in range(PAGES_PER_STEP // PAGES_PER_BLOCK):
        s = jnp.zeros((1, GROUP_W), F32)
        for pg in range(PAGES_PER_BLOCK):
            s = s + jnp.sum(refs[bb * PAGES_PER_BLOCK + pg][...], axis=0, keepdims=True)
        o_ref[bb:bb + 1, :] = s * (1.0 / MOBA_BLOCK)


def _paged_kmean(cache, page_table, page_base):
    bs, n_pages = page_table.shape
    page = lambda s: pl.BlockSpec((None, PAGE_SIZE, GROUP_W),
                                  lambda b, j, pt, s=s: (page_base + pt[b, j * PAGES_PER_STEP + s], 0, 0))
    nblk = PAGES_PER_STEP // PAGES_PER_BLOCK
    grid_spec = pltpu.PrefetchScalarGridSpec(
        num_scalar_prefetch=1,
        grid=(bs, n_pages // PAGES_PER_STEP),
        in_specs=[page(s) for s in range(PAGES_PER_STEP)],
        out_specs=pl.BlockSpec((None, nblk, GROUP_W), lambda b, j, pt: (b, j, 0)))
    return pl.pallas_call(
        _paged_kmean_kernel,
        grid_spec=grid_spec,
        out_shape=jax.ShapeDtypeStruct((bs, n_pages // PAGES_PER_BLOCK, GROUP_W), F32),
        compiler_params=_params("parallel", "arbitrary"),
        name="paged_kmean",
    )(page_table, *([cache] * PAGES_PER_STEP))


def _sample_select_kernel(q_ref, km_ref, o_ref):
    prod = km_ref[...] * q_ref[...]
    head_cols = (_lane_head((GROUP_W, LANE), 0) == lax.broadcasted_iota(I32, (GROUP_W, LANE), 1)).astype(F32)
    s = jnp.dot(prod, head_cols, precision=HIGHEST, preferred_element_type=F32)
    blk = lax.broadcasted_iota(I32, s.shape, 0)
    o_ref[...] = jnp.zeros(o_ref.shape, I32)
    for r in range(MOBA_TOPK):
        mx = jnp.max(s, axis=0, keepdims=True)
        idx = jnp.min(jnp.where(s == mx, blk, s.shape[0]), axis=0, keepdims=True)
        o_ref[r:r + 1, :] = idx
        s = jnp.where(blk == idx, NEG_PICKED, s)


def _sample_select(q3, kmean):
    bs, nb, _ = kmean.shape
    return pl.pallas_call(
        _sample_select_kernel,
        grid=(bs,),
        in_specs=[pl.BlockSpec((None, 1, GROUP_W), lambda b: (b, 0, 0)),
                  pl.BlockSpec((None, nb, GROUP_W), lambda b: (b, 0, 0))],
        out_specs=pl.BlockSpec((None, 8, LANE), lambda b: (b, 0, 0)),
        out_shape=jax.ShapeDtypeStruct((bs, 8, LANE), I32),
        compiler_params=_params("parallel"),
        name="sample_select",
    )(q3, kmean)


N_SEL_PAGES = MOBA_TOPK * PAGES_PER_BLOCK


def _sample_attn_kernel(ph_ref, q_ref, kn_ref, vn_ref, *refs):
    kp, vp, o_ref = refs[:N_SEL_PAGES], refs[N_SEL_PAGES:2 * N_SEL_PAGES], refs[-1]
    h = pl.program_id(1)
    scale = HEAD_DIM ** -0.5
    qm = jnp.where(_lane_head((1, GROUP_W), 1) == h, q_ref[...], 0.0)
    own = jnp.sum(kn_ref[...] * qm, axis=1, keepdims=True) * scale
    logits = [jnp.sum(r[...] * qm, axis=1, keepdims=True) * scale for r in kp]
    m = own
    for lg in logits:
        m = jnp.maximum(m, jnp.max(lg, axis=0, keepdims=True))
    p_own = jnp.exp(own - m)
    den = p_own
    acc = p_own * vn_ref[...]
    for lg, vr in zip(logits, vp):
        p = jnp.exp(lg - m)
        den = den + jnp.sum(p, axis=0, keepdims=True)
        acc = acc + jnp.sum(p * vr[...], axis=0, keepdims=True)
    out = jnp.where(_lane_head((1, GROUP_W), 1) == h, acc / den, 0.0)

    @pl.when(h == 0)
    def _():
        o_ref[...] = out

    @pl.when(h > 0)
    def _():
        o_ref[...] += out


def _sample_attn(phys, q3, kn3, vn3, cache_k, cache_v, page_base):
    bs = q3.shape[0]
    row = pl.BlockSpec((None, 1, GROUP_W), lambda b, h, ph: (b, 0, 0))
    page = lambda s: pl.BlockSpec(
        (None, PAGE_SIZE, GROUP_W),
        lambda b, h, ph, s=s: (page_base + ph[(b * N_HEADS + h) * N_SEL_PAGES + s], 0, 0))
    pages = [page(s) for s in range(N_SEL_PAGES)]
    grid_spec = pltpu.PrefetchScalarGridSpec(
        num_scalar_prefetch=1,
        grid=(bs, N_HEADS),
        in_specs=[row, row, row] + pages + pages,
        out_specs=row)
    return pl.pallas_call(
        _sample_attn_kernel,
        grid_spec=grid_spec,
        out_shape=jax.ShapeDtypeStruct((bs, 1, GROUP_W), F32),
        compiler_params=_params("parallel", "arbitrary"),
        name="sample_attn",
    )(phys, q3, kn3, vn3, *([cache_k] * N_SEL_PAGES), *([cache_v] * N_SEL_PAGES))


def _prep_layer(w_in, w_out, pool_w, pool_scale, sgu_w, sgu_b, norm_w):
    d = w_in.shape[0]
    g = GROUP_W
    gate0 = 7 * g
    att0 = gate0 + 2 * N_HEADS
    pad = jnp.zeros((d, GATE_W - 2 * N_HEADS), w_in.dtype)
    w_z = jnp.concatenate([w_in[:, :gate0], w_in[:, att0:], w_in[:, gate0:att0], pad], axis=1).astype(BF16)
    wbd = jnp.zeros((g, g), F32)
    for i in range(len(POOL_WINDOWS)):
        sl = slice(i * POOL_GROUP, (i + 1) * POOL_GROUP)
        wbd = wbd.at[sl, sl].set(pool_w[i])
    return dict(
        w_z=w_z, w_out=w_out.astype(BF16), pool_wbd=wbd.astype(BF16),
        pool_scale=pool_scale.reshape(1, g),
        sgu_w=sgu_w, sgu_bias=jnp.repeat(sgu_b.T, HEAD_DIM, axis=1),
        sgu_w0=jnp.repeat(sgu_w[:, 0, 0], HEAD_DIM).reshape(1, g),
        sgu_b0=jnp.repeat(sgu_b[:, 0], HEAD_DIM).reshape(1, g),
        norm_w=norm_w.reshape(1, g))


def _head_cols(a, bs):
    return a.reshape(bs * N_HEADS, HEAD_DIM, 1)


def _head_rows(a, bs):
    return a.reshape(bs * N_HEADS, 1, HEAD_DIM)


def _mixers_prompt(x, batch, seq, nmw, lp, gate_b, rope_tabs):
    z = _inproj(x, nmw, lp["w_z"], rope_tabs)
    g = GROUP_W
    nb = seq // MOBA_BLOCK
    col = lambda c: z[:, c * g:(c + 1) * g]
    m_ab = _poolsgu(z, batch, seq, lp["pool_wbd"], lp["pool_scale"], lp["sgu_w"], lp["sgu_bias"])
    gates_t = z[:, 10 * g:10 * g + 2 * N_HEADS].T
    m_c, ct, nst, mst = _mlstm(z, gates_t, batch, seq, gate_b, lp["norm_w"])
    k, v = col(COL_AK), col(COL_AV)
    kb4 = k.astype(BF16).reshape(batch, nb, MOBA_BLOCK, g)
    vt4 = jnp.swapaxes(v.astype(BF16).reshape(batch, nb, MOBA_BLOCK, g), 2, 3)
    kmean = _kmean(z, batch * nb).reshape(batch, nb, g)
    m_d = _moba(z, kb4, vt4, kmean, batch, nb)
    h = _outproj(x, m_ab, m_c, m_d, lp["w_out"])

    heads = lambda a: a.reshape(batch, seq, N_HEADS, HEAD_DIM)
    c_new = jnp.stack([ct[:, i * HEAD_DIM:(i + 1) * HEAD_DIM, i * HEAD_DIM:(i + 1) * HEAD_DIM]
                       for i in range(N_HEADS)], axis=1)
    state = (heads(k), heads(v), col(COL_P).reshape(batch, seq, g)[:, seq - POOL_BUF:],
             jnp.swapaxes(c_new, -1, -2), nst.reshape(batch, N_HEADS, HEAD_DIM), mst[:, :N_HEADS, 0])
    return h, state


def _mixers_sample(x, past_len, nmw, lp, gate_b, rope_tabs, pool_state, c_st, n_st, m_st,
                   cache_k, cache_v, page_table, page_base):
    bs = x.shape[0]
    g = GROUP_W
    z = _inproj(x, nmw, lp["w_z"], rope_tabs)
    col = lambda c: z[:, c * g:(c + 1) * g]
    p, sv = col(COL_P), col(COL_SV)
    gates = z[:, 10 * g:10 * g + 2 * N_HEADS] + gate_b.reshape(1, -1)
    bh = bs * N_HEADS
    prefix = jnp.concatenate([jnp.zeros((1, bs, g), F32), jnp.swapaxes(pool_state, 0, 1)], axis=0)
    args = (p, prefix, lp["pool_wbd"], lp["pool_scale"], col(COL_SU), sv, lp["sgu_w0"], lp["sgu_b0"],
            _head_rows(col(COL_MQ), bs), _head_rows(col(COL_MK), bs), _head_cols(col(COL_MV), bs),
            gates[:, :N_HEADS].reshape(bh, 1, 1), gates[:, N_HEADS:].reshape(bh, 1, 1),
            c_st.reshape(bh, HEAD_DIM, HEAD_DIM), n_st.reshape(bh, 1, HEAD_DIM), m_st.reshape(bh, 1, 1),
            _head_cols(col(COL_MO), bs), _head_cols(jnp.tile(lp["norm_w"], (bs, 1)), bs))
    sds = jax.ShapeDtypeStruct
    outs = (sds((bs, g), F32), sds((bs, g), F32), sds((bh, HEAD_DIM, 1), F32),
            sds((bh, HEAD_DIM, HEAD_DIM), F32), sds((bh, 1, HEAD_DIM), F32), sds((bh, 1, 1), F32))
    y_pool, y_sgu, y_ml, c_new, n_new, m_new = _smix(args, outs, min(past_len + 1, POOL_WINDOWS[-1]))

    q3, k3, v3 = (col(c).reshape(bs, 1, g) for c in (COL_AQ, COL_AK, COL_AV))
    kmean = _paged_kmean(cache_k, page_table, page_base)
    sel = _sample_select(q3, kmean)[:, :MOBA_TOPK, :N_HEADS]
    pages = (jnp.swapaxes(sel, 1, 2)[..., None] * PAGES_PER_BLOCK
             + jnp.arange(PAGES_PER_BLOCK, dtype=I32))
    phys = jnp.take_along_axis(page_table, pages.reshape(bs, -1), axis=1).reshape(-1)
    y_at = _sample_attn(phys, q3, k3, v3, cache_k, cache_v, page_base).reshape(bs, g)

    m_ab = jnp.concatenate([y_pool, y_sgu], axis=1).astype(BF16)
    h = _outproj(x, m_ab, y_ml.reshape(bs, g).astype(BF16), y_at.astype(BF16), lp["w_out"])

    heads = lambda a: a.reshape(bs, 1, N_HEADS, HEAD_DIM)
    state = (heads(col(COL_AK)), heads(col(COL_AV)),
             jnp.concatenate([pool_state[:, 1:], p[:, None, :]], axis=1), sv.reshape(bs, 1, g),
             c_new.reshape(bs, N_HEADS, HEAD_DIM, HEAD_DIM), n_new.reshape(bs, N_HEADS, HEAD_DIM),
             m_new.reshape(bs, N_HEADS))
    return h, state


def kernel(x_prompt, x_sample, cache_k, cache_v, page_table, state_pool, state_mlstm_c, state_mlstm_n,
           state_mlstm_m, norm_mix_w, norm_ffn_w, final_norm_w, w_in, w_out, pool_w, pool_scale, sgu_w,
           sgu_b, mlstm_gate_b, mlstm_norm_w, ffn_w1, ffn_w3, ffn_w2, router_w, router_b,
           moe_w1, moe_w3, moe_w2):
    batch, seq, d = x_prompt.shape
    bs = x_sample.shape[0]
    depth = w_in.shape[0]
    n_pool = cache_k.shape[1]
    n_pages = page_table.shape[1]
    past_len = n_pages * PAGE_SIZE
    n_experts = router_w.shape[-1]
    assert x_sample.shape[1] == 1 and d == N_MIXERS * GROUP_W
    assert seq % MOBA_BLOCK == 0 and seq % (2 * MLSTM_CHUNK) == 0 and seq >= POOL_BUF
    assert past_len % MOBA_BLOCK == 0 and past_len // MOBA_BLOCK >= MOBA_TOPK
    assert n_pages % PAGES_PER_STEP == 0

    rope_p = _rope_tables(jnp.arange(seq, dtype=I32))
    rope_s = _rope_tables(jnp.full((bs,), past_len, I32))
    ck = cache_k.reshape(depth * n_pool, PAGE_SIZE, GROUP_W)
    cv = cache_v.reshape(depth * n_pool, PAGE_SIZE, GROUP_W)
    fw = final_norm_w.reshape(1, d)
    tm_moe = 512

    hp = x_prompt.reshape(batch * seq, d)
    hs = x_sample.reshape(bs, d)
    st_p, st_s = [], []
    for l in range(depth):
        lp = _prep_layer(w_in[l], w_out[l], pool_w[l], pool_scale[l], sgu_w[l], sgu_b[l], mlstm_norm_w[l])
        nmw = norm_mix_w[l].reshape(1, d)
        nfw = norm_ffn_w[l].reshape(1, d)
        hp, sp = _mixers_prompt(hp, batch, seq, nmw, lp, mlstm_gate_b[l], rope_p)
        hs, ss = _mixers_sample(hs, past_len, nmw, lp, mlstm_gate_b[l], rope_s, state_pool[l],
                                state_mlstm_c[l], state_mlstm_n[l], state_mlstm_m[l],
                                ck, cv, page_table, l * n_pool)
        st_p.append(sp)
        st_s.append(ss)
        last = l == depth - 1
        j = l // 2
        if l % 2 == 0:
            w1, w3, w2 = ffn_w1[j].astype(BF16), ffn_w3[j].astype(BF16), ffn_w2[j].astype(BF16)
            hp = _ffn(hp, nfw, w1, w3, w2)
            hs = _ffn(hs, nfw, w1, w3, w2)
            if last:
                hp, hs = _final_norm(hp, fw), _final_norm(hs, fw)
        else:
            w1, w3, w2 = moe_w1[j].astype(BF16), moe_w3[j].astype(BF16), moe_w2[j].astype(BF16)
            rw = jnp.zeros((d, LANE), F32).at[:, :n_experts].set(router_w[j])
            rb = jnp.full((1, LANE), -jnp.inf, F32).at[0, :n_experts].set(router_b[j])
            xp, gates, e_idx = _router(hp, nfw, rw, rb)
            pos, src, tile_expert, nused = _dispatch_plan(e_idx, n_experts, tm_moe)
            o_sorted = _gmm(xp, tile_expert, src, nused, w1, w3, w2, tm_moe)
            hp = _combine(pos, hp, gates, o_sorted, fw, last)
            _, gates_s, e_s = _router(hs, nfw, rw, rb)
            hs = _moe_small(hs, nfw, gates_s, e_s, w1, w3, w2, fw, last)

    stack = lambda sts, i: jnp.stack([s[i] for s in sts])
    return (hp.reshape(batch, seq, d), hs.reshape(bs, 1, d),
            stack(st_p, 0), stack(st_p, 1), stack(st_s, 0), stack(st_s, 1),
            stack(st_p, 2), stack(st_s, 2), stack(st_s, 3),
            stack(st_p, 3), stack(st_p, 4), stack(st_p, 5),
            stack(st_s, 4), stack(st_s, 5), stack(st_s, 6))
```

```python
import functools

import numpy as np
import jax
import jax.numpy as jnp
from jax import lax
from jax.experimental import pallas as pl
from jax.experimental.pallas import tpu as pltpu

F32 = jnp.float32
BF16 = jnp.bfloat16
I32 = jnp.int32
HIGHEST = lax.Precision.HIGHEST

N_MIXERS = 4
HEAD_DIM = 64
N_HEADS = 4
GROUP_W = N_HEADS * HEAD_DIM
POOL_WINDOWS = (2, 4, 8, 16)
POOL_GROUP = GROUP_W // len(POOL_WINDOWS)
POOL_BUF = max(POOL_WINDOWS) - 1
HALO = POOL_BUF + 1
SGU_CHUNK = 128
MLSTM_CHUNK = 64
MOBA_BLOCK = 256
MOBA_TOPK = 3
PAGE_SIZE = 128
ROPE_THETA = 500000.0
ROT_DIM = HEAD_DIM // 4
TOP_K = 2
NORM_EPS = 1e-6
NEG = -1e30
NEG_PICKED = -3e38

LANE = 128
GATE_W = LANE
Z_W = 10 * GROUP_W + GATE_W
Z_TN = 7 * LANE
COL_P, COL_SU, COL_SV, COL_MQ, COL_MK, COL_MV, COL_MO, COL_AQ, COL_AK, COL_AV = range(10)
COL_GATE = (10 * GROUP_W) // GATE_W
VMEM_LIMIT = 56 * 1024 * 1024


def _params(*sem):
    return pltpu.CompilerParams(dimension_semantics=sem, vmem_limit_bytes=VMEM_LIMIT)


def _row_tile(n, pref):
    return pref if n % pref == 0 else n


def _lane_head(shape, dim):
    return lax.broadcasted_iota(I32, shape, dim) // HEAD_DIM


def _by_head(vals, lh):
    out = vals[N_HEADS - 1]
    for h in range(N_HEADS - 2, -1, -1):
        out = jnp.where(lh == h, vals[h], out)
    return out


def _dot_nt(a, b, precision=None):
    return lax.dot_general(a, b, (((1,), (1,)), ((), ())), precision=precision,
                           preferred_element_type=F32)


def _logsig(x):
    return jnp.minimum(x, 0.0) - jnp.log1p(jnp.exp(-jnp.abs(x)))


def _inproj_kernel(x_ref, nw_ref, w_ref, cos_ref, sa_ref, sb_ref, z_ref, xn_ref):
    j = pl.program_id(1)

    @pl.when(j == 0)
    def _():
        x = x_ref[...]
        ms = jnp.mean(x * x, axis=-1, keepdims=True)
        xn_ref[...] = (x * lax.rsqrt(ms + NORM_EPS) * nw_ref[...]).astype(BF16)

    z = jnp.dot(xn_ref[...], w_ref[...], preferred_element_type=F32)

    @pl.when(j < 2)
    def _():
        z_ref[...] = z

    @pl.when(j == 2)
    def _():
        c, sa, sb = cos_ref[...], sa_ref[...], sb_ref[...]
        half = ROT_DIM // 2
        for o in (0, GROUP_W):
            t = z[:, o:o + GROUP_W]
            z_ref[:, o:o + GROUP_W] = (t * c + pltpu.roll(t, GROUP_W - half, 1) * sa
                                       + pltpu.roll(t, half, 1) * sb)
        z_ref[:, 2 * GROUP_W:] = z[:, 2 * GROUP_W:]


def _inproj(x, nw, w, rope_tabs):
    n, d = x.shape
    tm = _row_tile(n, 512)
    tab_tiles = rope_tabs[0].shape[0] // tm
    tab_spec = pl.BlockSpec((tm, GROUP_W), lambda i, j: (i % tab_tiles, 0))
    return pl.pallas_call(
        _inproj_kernel,
        grid=(n // tm, Z_W // Z_TN),
        in_specs=[pl.BlockSpec((tm, d), lambda i, j: (i, 0)),
                  pl.BlockSpec((1, d), lambda i, j: (0, 0)),
                  pl.BlockSpec((d, Z_TN), lambda i, j: (0, j)),
                  tab_spec, tab_spec, tab_spec],
        out_specs=pl.BlockSpec((tm, Z_TN), lambda i, j: (i, j)),
        out_shape=jax.ShapeDtypeStruct((n, Z_W), F32),
        scratch_shapes=[pltpu.VMEM((tm, d), BF16)],
        compiler_params=_params("parallel", "arbitrary"),
        name="inproj",
    )(x, nw, w, *rope_tabs)


def _rope_tables(pos):
    half = ROT_DIM // 2
    inv = 1.0 / (ROPE_THETA ** (jnp.arange(half, dtype=F32) / half))
    ang = pos.astype(F32)[:, None] * inv[None, :]
    cos, sin = jnp.cos(ang), jnp.sin(ang)
    t = pos.shape[0]
    rest = HEAD_DIM - ROT_DIM
    one = jnp.ones((t, rest), F32)
    zero = jnp.zeros((t, rest), F32)
    zh = jnp.zeros((t, half), F32)
    c = jnp.concatenate([cos, cos, one], axis=1)
    sa = jnp.concatenate([-sin, zh, zero], axis=1)
    sb = jnp.concatenate([zh, sin, zero], axis=1)
    tile = lambda a: jnp.tile(a, (1, N_HEADS))
    return tile(c), tile(sa), tile(sb)


def _window_of_lane(shape, dim, lo):
    g = (lax.broadcasted_iota(I32, shape, dim) + lo) // POOL_GROUP
    w = jnp.full(shape, POOL_WINDOWS[-1], I32)
    for i in range(len(POOL_WINDOWS) - 2, -1, -1):
        w = jnp.where(g == i, POOL_WINDOWS[i], w)
    return w


def _poolsgu_kernel(p_ref, su_ref, sv_ref, pw_ref, ps_ref, sw_ref, sb_ref, o_ref, ebuf):
    t = pl.program_id(1)
    tt = p_ref.shape[0]

    @pl.when(t == 0)
    def _():
        ebuf[0:HALO, :] = jnp.zeros((HALO, GROUP_W), F32)

    @pl.when(t > 0)
    def _():
        ebuf[0:HALO, :] = ebuf[tt:tt + HALO, :]

    ebuf[HALO:HALO + tt, :] = p_ref[...]

    pos1 = lax.broadcasted_iota(I32, (tt, LANE), 0) + t * tt + 1
    halves = []
    for half in range(2):
        lo = half * LANE
        wsmall, wbig = POOL_WINDOWS[2 * half], POOL_WINDOWS[2 * half + 1]
        e0 = ebuf[HALO:HALO + tt, lo:lo + LANE]
        acc = e0
        for off in range(1, wsmall):
            acc = acc + ebuf[HALO - off:HALO - off + tt, lo:lo + LANE]
        small = acc
        for off in range(wsmall, wbig):
            acc = acc + ebuf[HALO - off:HALO - off + tt, lo:lo + LANE]
        first = lax.broadcasted_iota(I32, (tt, LANE), 1) < POOL_GROUP
        wsum = jnp.where(first, small, acc)
        cnt = jnp.minimum(pos1, _window_of_lane((tt, LANE), 1, lo)).astype(F32)
        halves.append(wsum / cnt - e0)
    d = jnp.concatenate(halves, axis=1).astype(BF16)
    y_pool = jnp.dot(d, pw_ref[...], preferred_element_type=F32) * ps_ref[...]
    o_ref[:, 0:GROUP_W] = y_pool.astype(o_ref.dtype)

    row = lax.broadcasted_iota(I32, (SGU_CHUNK, SGU_CHUNK), 0)
    col = lax.broadcasted_iota(I32, (SGU_CHUNK, SGU_CHUNK), 1)
    ws = [jnp.where(row >= col, sw_ref[h], 0.0).astype(BF16) for h in range(N_HEADS)]
    lh = _lane_head((SGU_CHUNK, GROUP_W), 1)
    for c in range(tt // SGU_CHUNK):
        rows = slice(c * SGU_CHUNK, (c + 1) * SGU_CHUNK)
        v = sv_ref[rows, :].astype(BF16)
        mixed = _by_head([jnp.dot(ws[h], v, preferred_element_type=F32) for h in range(N_HEADS)], lh)
        y = su_ref[rows, :] * (mixed + sb_ref[...])
        o_ref[rows, GROUP_W:2 * GROUP_W] = y.astype(o_ref.dtype)


def _poolsgu(z, batch, seq, pool_wbd, pool_scale, sgu_w, sgu_bias):
    tt = _row_tile(seq, 512)
    tps = seq // tt
    zspec = lambda c: pl.BlockSpec((tt, GROUP_W), lambda b, t, c=c: (b * tps + t, c))
    return pl.pallas_call(
        _poolsgu_kernel,
        grid=(batch, tps),
        in_specs=[zspec(COL_P), zspec(COL_SU), zspec(COL_SV),
                  pl.BlockSpec((GROUP_W, GROUP_W), lambda b, t: (0, 0)),
                  pl.BlockSpec((1, GROUP_W), lambda b, t: (0, 0)),
                  pl.BlockSpec((N_HEADS, SGU_CHUNK, SGU_CHUNK), lambda b, t: (0, 0, 0)),
                  pl.BlockSpec((SGU_CHUNK, GROUP_W), lambda b, t: (0, 0))],
        out_specs=pl.BlockSpec((tt, 2 * GROUP_W), lambda b, t: (b * tps + t, 0)),
        out_shape=jax.ShapeDtypeStruct((batch * seq, 2 * GROUP_W), BF16),
        scratch_shapes=[pltpu.VMEM((HALO + tt, GROUP_W), F32)],
        compiler_params=_params("parallel", "arbitrary"),
        name="pool_sgu",
    )(z, z, z, pool_wbd, pool_scale, sgu_w, sgu_bias)


def _mlstm_kernel(q_ref, k_ref, v_ref, o_ref, g_ref, gt_ref, gbc_ref, gbr_ref, nw_ref,
                  y_ref, ct_ref, n_ref, m_ref):
    L = MLSTM_CHUNK

    @pl.when(pl.program_id(1) == 0)
    def _():
        ct_ref[...] = jnp.zeros(ct_ref.shape, F32)
        n_ref[...] = jnp.zeros(n_ref.shape, F32)
        m_ref[...] = jnp.zeros(m_ref.shape, F32)

    r_i = lax.broadcasted_iota(I32, (L, L), 0)
    c_i = lax.broadcasted_iota(I32, (L, L), 1)
    causal = r_i >= c_i
    tri_l = causal.astype(F32)
    tri_u = (r_i <= c_i).astype(F32)
    lh = _lane_head((L, GROUP_W), 1)
    bd = _lane_head((GROUP_W, GROUP_W), 0) == _lane_head((GROUP_W, GROUP_W), 1)
    same_head = bd.astype(F32)

    ct = ct_ref[...]
    nrow = n_ref[...]
    m_heads = [m_ref[h:h + 1, 0:1] for h in range(N_HEADS)]

    for c in range(q_ref.shape[0] // L):
        rows = slice(c * L, (c + 1) * L)
        q = q_ref[rows, :]
        ks = k_ref[rows, :] * (HEAD_DIM ** -0.5)
        v = v_ref[rows, :]
        g = g_ref[rows, :] + gbc_ref[...]
        gt = gt_ref[:, rows] + gbr_ref[...]
        bcol = jnp.dot(tri_l, _logsig(g), precision=HIGHEST, preferred_element_type=F32)
        brow = jnp.dot(_logsig(gt), tri_u, precision=HIGHEST, preferred_element_type=F32)
        qb, kb, vb = q.astype(BF16), ks.astype(BF16), v.astype(BF16)

        s_l, inter_l, deni_l, emt_l, wsrc_l, dec_l, mnew_l = [], [], [], [], [], [], []
        for h in range(N_HEADS):
            bc = bcol[:, N_HEADS + h:N_HEADS + h + 1]
            ic = g[:, h:h + 1]
            br = brow[N_HEADS + h:N_HEADS + h + 1, :]
            ir = gt[h:h + 1, :]
            dm = jnp.where(causal, bc - br + ir, NEG)
            a = bc + m_heads[h]
            mt = jnp.maximum(a, jnp.max(dm, axis=1, keepdims=True))
            qm = jnp.where(lh == h, q, 0.0).astype(BF16)
            s = _dot_nt(qm, kb) * jnp.exp(dm - mt)
            m_new = mt[L - 1:L, :]
            s_l.append(s)
            inter_l.append(jnp.exp(a - mt))
            deni_l.append(jnp.sum(s, axis=1, keepdims=True))
            emt_l.append(jnp.exp(-mt))
            wsrc_l.append(jnp.exp(bc[L - 1:L, :] - bc + ic - m_new))
            dec_l.append(jnp.exp(a[L - 1:L, :] - m_new))
            mnew_l.append(m_new)

        r = jnp.dot(jnp.concatenate(s_l, axis=0).astype(BF16), vb, preferred_element_type=F32)
        num_i = _by_head([r[h * L:(h + 1) * L, :] for h in range(N_HEADS)], lh)
        q_c = jnp.dot(qb, ct.astype(BF16), preferred_element_type=F32)
        q_n = jnp.dot(q * nrow, same_head, precision=HIGHEST, preferred_element_type=F32)
        inter = _by_head(inter_l, lh)
        num = num_i + inter * q_c
        den = _by_head(deni_l, lh) + inter * q_n
        hval = num / jnp.maximum(jnp.abs(den), _by_head(emt_l, lh))
        ms = jnp.dot(hval * hval, same_head, precision=HIGHEST,
                     preferred_element_type=F32) * (1.0 / HEAD_DIM)
        y = hval * lax.rsqrt(ms + NORM_EPS) * nw_ref[...]
        y_ref[rows, :] = (jax.nn.sigmoid(o_ref[rows, :]) * y).astype(y_ref.dtype)

        kw = ks * _by_head(wsrc_l, lh)
        dec = _by_head(dec_l, lh[0:1, :])
        upd = jnp.dot(kw.T.astype(BF16), vb, preferred_element_type=F32)
        ct = ct * dec + jnp.where(bd, upd, 0.0)
        nrow = nrow * dec + jnp.sum(kw, axis=0, keepdims=True)
        m_heads = mnew_l

    ct_ref[...] = ct
    n_ref[...] = nrow
    for h in range(N_HEADS):
        m_ref[h:h + 1, :] = jnp.broadcast_to(m_heads[h], (1, m_ref.shape[1]))


def _mlstm(z, gates_t, batch, seq, gate_b, norm_w):
    rows = _row_tile(seq, 4 * MLSTM_CHUNK)
    tps = seq // rows
    zspec = lambda c: pl.BlockSpec((rows, GROUP_W), lambda b, t, c=c: (b * tps + t, c))
    gbc = jnp.zeros((1, GATE_W), F32).at[0, :2 * N_HEADS].set(gate_b.reshape(-1))
    gbr = gate_b.reshape(2 * N_HEADS, 1)
    state = lambda r, c: (pl.BlockSpec((None, r, c), lambda b, t: (b, 0, 0)),
                          jax.ShapeDtypeStruct((batch, r, c), F32))
    (cs, csh), (ns, nsh), (msp, msh) = state(GROUP_W, GROUP_W), state(1, GROUP_W), state(8, LANE)
    return pl.pallas_call(
        _mlstm_kernel,
        grid=(batch, tps),
        in_specs=[zspec(COL_MQ), zspec(COL_MK), zspec(COL_MV), zspec(COL_MO),
                  pl.BlockSpec((rows, GATE_W), lambda b, t: (b * tps + t, COL_GATE)),
                  pl.BlockSpec((2 * N_HEADS, rows), lambda b, t: (0, b * tps + t)),
                  pl.BlockSpec((1, GATE_W), lambda b, t: (0, 0)),
                  pl.BlockSpec((2 * N_HEADS, 1), lambda b, t: (0, 0)),
                  pl.BlockSpec((1, GROUP_W), lambda b, t: (0, 0))],
        out_specs=[pl.BlockSpec((rows, GROUP_W), lambda b, t: (b * tps + t, 0)), cs, ns, msp],
        out_shape=[jax.ShapeDtypeStruct((batch * seq, GROUP_W), BF16), csh, nsh, msh],
        compiler_params=_params("parallel", "arbitrary"),
        name="mlstm",
    )(z, z, z, z, z, gates_t, gbc, gbr, norm_w)


def _kmean_kernel(k_ref, o_ref):
    nb = o_ref.shape[0]
    k = k_ref[...].reshape(nb, MOBA_BLOCK, GROUP_W)
    o_ref[...] = jnp.sum(k, axis=1) * (1.0 / MOBA_BLOCK)


def _kmean(z, nblocks):
    nb = _row_tile(nblocks, 8)
    return pl.pallas_call(
        _kmean_kernel,
        grid=(nblocks // nb,),
        in_specs=[pl.BlockSpec((nb * MOBA_BLOCK, GROUP_W), lambda i: (i, COL_AK))],
        out_specs=pl.BlockSpec((nb, GROUP_W), lambda i: (i, 0)),
        out_shape=jax.ShapeDtypeStruct((nblocks, GROUP_W), F32),
        compiler_params=_params("parallel"),
        name="kmean",
    )(z)


def _pick_top_blocks(s, n_valid_mask):
    blk = lax.broadcasted_iota(I32, s.shape, 0)
    s = jnp.where(n_valid_mask, s, NEG)
    bias = jnp.full(s.shape, NEG, F32)
    for _ in range(MOBA_TOPK):
        mx = jnp.max(s, axis=0, keepdims=True)
        idx = jnp.min(jnp.where(s == mx, blk, s.shape[0]), axis=0, keepdims=True)
        pick = blk == idx
        bias = jnp.where(jnp.logical_and(pick, mx > 0.5 * NEG), 0.0, bias)
        s = jnp.where(pick, NEG_PICKED, s)
    return bias


MOBA_KEY_CHUNK = 64
LOG2E = 1.4426950408889634


def _moba_logits(k_blk, qbd_sc, lt_sc, buf):
    lt_sc[buf] = jnp.dot(k_blk[...], qbd_sc[...], preferred_element_type=F32)


def _moba_softmax_pv(h, buf, vt_blk, lt_sc, p_sc, m_sc, l_sc, acc_sc, bias, key_le_query):
    B = MOBA_BLOCK
    ch = MOBA_KEY_CHUNK
    groups = ch // 8
    first = bias is None
    cols = slice(h * B, (h + 1) * B)
    mx = None
    for c in range(B // ch):
        rows = slice(c * ch, (c + 1) * ch)
        s = lt_sc[buf, rows, cols]
        if first:
            s = jnp.where(key_le_query[rows, :], s, NEG)
            lt_sc[buf, rows, cols] = s
        part = jnp.max(s.reshape(groups, 8, B), axis=0)
        mx = part if mx is None else jnp.maximum(mx, part)
    cand = jnp.max(mx, axis=0, keepdims=True)
    if first:
        m_new = shift = cand
    else:
        m_old = m_sc[h:h + 1, :]
        m_new = jnp.maximum(m_old, cand + bias)
        alpha = jnp.exp2(m_old - m_new)
        shift = m_new - bias
    lsum = None
    for c in range(B // ch):
        rows = slice(c * ch, (c + 1) * ch)
        p = jnp.exp2(lt_sc[buf, rows, cols] - shift)
        part = jnp.sum(p.reshape(groups, 8, B), axis=0)
        lsum = part if lsum is None else lsum + part
        p_sc[h, rows, :] = p.astype(BF16)
    lnew = jnp.sum(lsum, axis=0, keepdims=True)
    hs = slice(h * HEAD_DIM, (h + 1) * HEAD_DIM)
    pv = jnp.dot(vt_blk[hs, :], p_sc[h], preferred_element_type=F32)
    if first:
        l_sc[h:h + 1, :] = lnew
        acc_sc[hs, :] = pv
    else:
        l_sc[h:h + 1, :] = alpha * l_sc[h:h + 1, :] + lnew
        acc_sc[hs, :] = alpha * acc_sc[hs, :] + pv
    m_sc[h:h + 1, :] = m_new


def _moba_kernel(q_ref, kb_ref, vt_ref, km_ref, o_ref, qbd_sc, sel_sc, m_sc, l_sc, acc_sc, lt_sc, p_sc):
    i = pl.program_id(1)
    nb = km_ref.shape[0]
    nbp = sel_sc.shape[0] // N_HEADS
    B = MOBA_BLOCK
    q = q_ref[...]
    km = km_ref[...]
    lh = _lane_head((B, GROUP_W), 1)
    past = lax.broadcasted_iota(I32, (nb, B), 0) < i
    key_le_query = lax.broadcasted_iota(I32, (B, B), 0) <= lax.broadcasted_iota(I32, (B, B), 1)
    stats = (lt_sc, p_sc, m_sc, l_sc, acc_sc)

    q_t = q.T * (HEAD_DIM ** -0.5 * LOG2E)
    row_head = _lane_head((GROUP_W, B), 0)
    for h in range(N_HEADS):
        s_blk = _dot_nt(km, jnp.where(lh == h, q, 0.0), precision=HIGHEST)
        sel_sc[h * nbp:h * nbp + nb, :] = _pick_top_blocks(s_blk, past)
        qbd_sc[:, h * B:(h + 1) * B] = jnp.where(row_head == h, q_t, 0.0).astype(BF16)

    _moba_logits(kb_ref.at[i], qbd_sc, lt_sc, 1)
    _moba_logits(kb_ref.at[0], qbd_sc, lt_sc, 0)
    for h in range(N_HEADS):
        _moba_softmax_pv(h, 1, vt_ref.at[i], *stats, None, key_le_query)

    def step(n, buf):
        _moba_logits(kb_ref.at[jnp.minimum(n + 1, nb - 1)], qbd_sc, lt_sc, 1 - buf)
        for h in range(N_HEADS):
            bias = sel_sc[pl.ds(h * nbp + n, 1), :]
            _moba_softmax_pv(h, buf, vt_ref.at[n], *stats, bias, None)

    def body(k, carry):
        step(2 * k, 0)
        step(2 * k + 1, 1)
        return carry

    lax.fori_loop(0, (i + 1) // 2, body, 0)

    for h in range(N_HEADS):
        hs = slice(h * HEAD_DIM, (h + 1) * HEAD_DIM)
        acc_sc[hs, :] = acc_sc[hs, :] / l_sc[h:h + 1, :]
    o_ref[...] = acc_sc[...].T.astype(o_ref.dtype)


def _moba(z, kb4, vt4, kmean, batch, nb):
    nbp = -(-nb // 8) * 8
    B = MOBA_BLOCK
    blk4 = pl.BlockSpec((None, nb, B, GROUP_W), lambda b, i: (b, 0, 0, 0))
    return pl.pallas_call(
        _moba_kernel,
        grid=(batch, nb),
        in_specs=[pl.BlockSpec((B, GROUP_W), lambda b, i: (b * nb + i, COL_AQ)),
                  blk4, blk4,
                  pl.BlockSpec((None, nb, GROUP_W), lambda b, i: (b, 0, 0))],
        out_specs=pl.BlockSpec((B, GROUP_W), lambda b, i: (b * nb + i, 0)),
        out_shape=jax.ShapeDtypeStruct((batch * nb * B, GROUP_W), BF16),
        scratch_shapes=[pltpu.VMEM((GROUP_W, N_HEADS * B), BF16),
                        pltpu.VMEM((N_HEADS * nbp, B), F32),
                        pltpu.VMEM((8, B), F32), pltpu.VMEM((8, B), F32),
                        pltpu.VMEM((GROUP_W, B), F32),
                        pltpu.VMEM((2, B, N_HEADS * B), F32),
                        pltpu.VMEM((N_HEADS, B, B), BF16)],
        compiler_params=_params("parallel", "arbitrary"),
        name="moba",
    )(z, kb4, vt4, kmean)


def _outproj_kernel(x_ref, ma_ref, mc_ref, md_ref, w_ref, o_ref):
    g2 = 2 * GROUP_W
    acc = jnp.dot(ma_ref[...], w_ref[0:g2, :], preferred_element_type=F32)
    acc += jnp.dot(mc_ref[...], w_ref[g2:g2 + GROUP_W, :], preferred_element_type=F32)
    acc += jnp.dot(md_ref[...], w_ref[g2 + GROUP_W:, :], preferred_element_type=F32)
    o_ref[...] = x_ref[...] + acc


def _outproj(x, ma, mc, md, w):
    n, d = x.shape
    tm = _row_tile(n, 512)
    rows = lambda c: pl.BlockSpec((tm, c), lambda i: (i, 0))
    return pl.pallas_call(
        _outproj_kernel,
        grid=(n // tm,),
        in_specs=[rows(d), rows(2 * GROUP_W), rows(GROUP_W), rows(GROUP_W),
                  pl.BlockSpec(w.shape, lambda i: (0, 0))],
        out_specs=rows(d),
        out_shape=jax.ShapeDtypeStruct((n, d), F32),
        compiler_params=_params("parallel"),
        name="outproj",
    )(x, ma, mc, md, w)


def _ff_tile(d_ff):
    return d_ff // 2 if (d_ff // 2) % LANE == 0 else d_ff


def _swiglu_partial(xn, w1_ref, w3_ref, w2_ref):
    a = jnp.dot(xn, w1_ref[...], preferred_element_type=F32)
    b = jnp.dot(xn, w3_ref[...], preferred_element_type=F32)
    hmid = (a * jax.nn.sigmoid(a) * b).astype(BF16)
    return jnp.dot(hmid, w2_ref[...], preferred_element_type=F32)


def _ffn_kernel(x_ref, nw_ref, w1_ref, w3_ref, w2_ref, o_ref, xn_ref, acc_ref):
    j = pl.program_id(1)

    @pl.when(j == 0)
    def _():
        x = x_ref[...]
        ms = jnp.mean(x * x, axis=-1, keepdims=True)
        xn_ref[...] = (x * lax.rsqrt(ms + NORM_EPS) * nw_ref[...]).astype(BF16)
        acc_ref[...] = jnp.zeros(acc_ref.shape, F32)

    acc_ref[...] += _swiglu_partial(xn_ref[...], w1_ref, w3_ref, w2_ref)

    @pl.when(j == pl.num_programs(1) - 1)
    def _():
        o_ref[...] = x_ref[...] + acc_ref[...]


def _ffn(x, nw, w1, w3, w2):
    n, d = x.shape
    d_ff = w1.shape[1]
    tm = _row_tile(n, 512)
    tf = _ff_tile(d_ff)
    return pl.pallas_call(
        _ffn_kernel,
        grid=(n // tm, d_ff // tf),
        in_specs=[pl.BlockSpec((tm, d), lambda i, j: (i, 0)),
                  pl.BlockSpec((1, d), lambda i, j: (0, 0)),
                  pl.BlockSpec((d, tf), lambda i, j: (0, j)),
                  pl.BlockSpec((d, tf), lambda i, j: (0, j)),
                  pl.BlockSpec((tf, d), lambda i, j: (j, 0))],
        out_specs=pl.BlockSpec((tm, d), lambda i, j: (i, 0)),
        out_shape=jax.ShapeDtypeStruct((n, d), F32),
        scratch_shapes=[pltpu.VMEM((tm, d), BF16), pltpu.VMEM((tm, d), F32)],
        compiler_params=_params("parallel", "arbitrary"),
        name="ffn_dense",
    )(x, nw, w1, w3, w2)


def _router_kernel(x_ref, nw_ref, rw_ref, rb_ref, xp_ref, g_ref, e_ref):
    x = x_ref[...]
    ms = jnp.mean(x * x, axis=-1, keepdims=True)
    xn = x * lax.rsqrt(ms + NORM_EPS) * nw_ref[...]
    half = x.shape[1] // 2
    bits = pltpu.bitcast(xn.astype(BF16).astype(F32), jnp.uint32)
    xp_ref[...] = jnp.bitwise_or(jnp.right_shift(bits[:, :half], jnp.uint32(16)), bits[:, half:])

    logits = jnp.dot(xn, rw_ref[...], precision=HIGHEST, preferred_element_type=F32) + rb_ref[...]
    lane = lax.broadcasted_iota(I32, logits.shape, 1)
    mx1 = jnp.max(logits, axis=1, keepdims=True)
    i1 = jnp.min(jnp.where(logits == mx1, lane, LANE), axis=1, keepdims=True)
    rest = jnp.where(lane == i1, -jnp.inf, logits)
    mx2 = jnp.max(rest, axis=1, keepdims=True)
    i2 = jnp.min(jnp.where(rest == mx2, lane, LANE), axis=1, keepdims=True)
    e2 = jnp.exp(mx2 - mx1)
    g1 = 1.0 / (1.0 + e2)
    g_ref[...] = jnp.where(lane == 0, g1, jnp.where(lane == 1, e2 * g1, 0.0))
    e_ref[...] = jnp.where(lane == 0, i1, jnp.where(lane == 1, i2, 0))


def _router(x, nw, rw, rb):
    n, d = x.shape
    tm = _row_tile(n, 512)
    rows = lambda c: pl.BlockSpec((tm, c), lambda i: (i, 0))
    full = lambda a: pl.BlockSpec(a.shape, lambda i: (0, 0))
    return pl.pallas_call(
        _router_kernel,
        grid=(n // tm,),
        in_specs=[rows(d), full(nw), full(rw), full(rb)],
        out_specs=[rows(d // 2), rows(LANE), rows(LANE)],
        out_shape=[jax.ShapeDtypeStruct((n, d // 2), jnp.uint32),
                   jax.ShapeDtypeStruct((n, LANE), F32),
                   jax.ShapeDtypeStruct((n, LANE), I32)],
        compiler_params=_params("parallel"),
        name="router",
    )(x, nw, rw, rb)


def _unpack_rows(words):
    lo = pltpu.bitcast(jnp.left_shift(words, jnp.uint32(16)), F32)
    hi = pltpu.bitcast(jnp.bitwise_and(words, jnp.uint32(0xFFFF0000)), F32)
    return jnp.concatenate([lo, hi], axis=1).astype(BF16)


def _gmm_kernel(te_ref, src_ref, nused_ref, xp_hbm, w1_ref, w3_ref, w2_ref, o_ref,
                xbuf, sem, xn_ref, acc_ref):
    i = pl.program_id(0)
    j = pl.program_id(1)
    ntiles = pl.num_programs(0)
    tm = xbuf.shape[1]
    slot = i % 2

    def row_copy(tile, sl, r):
        tok = src_ref[tile * tm + r]
        return pltpu.make_async_copy(xp_hbm.at[pl.ds(tok, 1), :], xbuf.at[sl, pl.ds(r, 1), :], sem.at[sl])

    def issue(tile, sl):
        def body(r, carry):
            row_copy(tile, sl, r).start()
            return carry
        lax.fori_loop(0, tm, body, 0, unroll=8)

    @pl.when(j == 0)
    def _():
        @pl.when(i == 0)
        def _():
            issue(0, 0)

        @pl.when(i + 1 < ntiles)
        def _():
            issue(i + 1, 1 - slot)

        pltpu.make_async_copy(xbuf.at[slot], xbuf.at[slot], sem.at[slot]).wait()
        xn_ref[...] = _unpack_rows(xbuf[slot])
        acc_ref[...] = jnp.zeros(acc_ref.shape, F32)

    @pl.when(i < nused_ref[0])
    def _():
        acc_ref[...] += _swiglu_partial(xn_ref[...], w1_ref, w3_ref, w2_ref)

    @pl.when(j == pl.num_programs(1) - 1)
    def _():
        o_ref[...] = acc_ref[...]


def _gmm(xp, tile_expert, src, nused, w1, w3, w2, tm):
    ncap = src.shape[0]
    half = xp.shape[1]
    d = 2 * half
    d_ff = w1.shape[2]
    tf = _ff_tile(d_ff)
    grid_spec = pltpu.PrefetchScalarGridSpec(
        num_scalar_prefetch=3,
        grid=(ncap // tm, d_ff // tf),
        in_specs=[pl.BlockSpec(memory_space=pl.ANY),
                  pl.BlockSpec((None, d, tf), lambda i, j, te, s, nu: (te[i], 0, j)),
                  pl.BlockSpec((None, d, tf), lambda i, j, te, s, nu: (te[i], 0, j)),
                  pl.BlockSpec((None, tf, d), lambda i, j, te, s, nu: (te[i], j, 0))],
        out_specs=pl.BlockSpec((tm, d), lambda i, j, te, s, nu: (i, 0)),
        scratch_shapes=[pltpu.VMEM((2, tm, half), jnp.uint32),
                        pltpu.SemaphoreType.DMA((2,)),
                        pltpu.VMEM((tm, d), BF16),
                        pltpu.VMEM((tm, d), F32)])
    return pl.pallas_call(
        _gmm_kernel,
        grid_spec=grid_spec,
        out_shape=jax.ShapeDtypeStruct((ncap, d), F32),
        compiler_params=_params("arbitrary", "arbitrary"),
        name="moe_gmm",
    )(tile_expert, src, nused, xp, w1, w3, w2)


def _dispatch_plan(e_idx, n_experts, tm):
    n = e_idx.shape[0]
    ef = e_idx[:, :TOP_K].T.reshape(-1)
    onehot = (ef[:, None] == jnp.arange(n_experts, dtype=I32)[None, :]).astype(I32)
    rank = jnp.sum((jnp.cumsum(onehot, axis=0) - 1) * onehot, axis=1)
    counts = jnp.sum(onehot, axis=0)
    padded = ((counts + tm - 1) // tm) * tm
    ends = jnp.cumsum(padded)
    pos = (ends - padded)[ef] + rank
    ncap = TOP_K * n + n_experts * tm
    src = jnp.zeros((ncap,), I32).at[pos].set(jnp.arange(TOP_K * n, dtype=I32) % n)
    tile_start = jnp.arange(ncap // tm, dtype=I32) * tm
    tile_expert = jnp.minimum(jnp.searchsorted(ends, tile_start, side="right"), n_experts - 1).astype(I32)
    nused = (ends[-1] // tm).astype(I32).reshape(1)
    return pos.astype(I32), src, tile_expert, nused


def _combine_kernel(pos_ref, h_ref, g_ref, o_hbm, fw_ref, y_ref, obuf, sem, *, final_norm):
    i = pl.program_id(0)
    ntiles = pl.num_programs(0)
    tc = h_ref.shape[0]
    n = ntiles * tc
    slot = i % 2

    def issue(tile, sl):
        def body(r, carry):
            for k in range(TOP_K):
                p = pos_ref[k * n + tile * tc + r]
                pltpu.make_async_copy(o_hbm.at[pl.ds(p, 1), :], obuf.at[sl, k, pl.ds(r, 1), :],
                                      sem.at[sl]).start()
            return carry
        lax.fori_loop(0, tc, body, 0, unroll=8)

    @pl.when(i == 0)
    def _():
        issue(0, 0)

    @pl.when(i + 1 < ntiles)
    def _():
        issue(i + 1, 1 - slot)

    pltpu.make_async_copy(obuf.at[slot], obuf.at[slot], sem.at[slot]).wait()
    g = g_ref[...]
    y = h_ref[...] + g[:, 0:1] * obuf[slot, 0] + g[:, 1:2] * obuf[slot, 1]
    if final_norm:
        ms = jnp.mean(y * y, axis=-1, keepdims=True)
        y = y * lax.rsqrt(ms + NORM_EPS) * fw_ref[...]
    y_ref[...] = y


def _combine(pos, h, gates, o_sorted, fw, final_norm):
    n, d = h.shape
    tc = _row_tile(n, 256)
    grid_spec = pltpu.PrefetchScalarGridSpec(
        num_scalar_prefetch=1,
        grid=(n // tc,),
        in_specs=[pl.BlockSpec((tc, d), lambda i, p: (i, 0)),
                  pl.BlockSpec((tc, LANE), lambda i, p: (i, 0)),
                  pl.BlockSpec(memory_space=pl.ANY),
                  pl.BlockSpec((1, d), lambda i, p: (0, 0))],
        out_specs=pl.BlockSpec((tc, d), lambda i, p: (i, 0)),
        scratch_shapes=[pltpu.VMEM((2, TOP_K, tc, d), F32), pltpu.SemaphoreType.DMA((2,))])
    return pl.pallas_call(
        functools.partial(_combine_kernel, final_norm=final_norm),
        grid_spec=grid_spec,
        out_shape=jax.ShapeDtypeStruct((n, d), F32),
        compiler_params=_params("arbitrary"),
        name="moe_combine",
    )(pos, h, gates, o_sorted, fw)


def _moe_small_kernel(x_ref, nw_ref, g_ref, e_ref, w1_ref, w3_ref, w2_ref, fw_ref, y_ref,
                      xn_ref, acc_ref, *, final_norm):
    e = pl.program_id(0)
    j = pl.program_id(1)

    @pl.when(jnp.logical_and(e == 0, j == 0))
    def _():
        x = x_ref[...]
        ms = jnp.mean(x * x, axis=-1, keepdims=True)
        xn_ref[...] = (x * lax.rsqrt(ms + NORM_EPS) * nw_ref[...]).astype(BF16)
        acc_ref[...] = jnp.zeros(acc_ref.shape, F32)

    g = g_ref[...]
    idx = e_ref[...]
    gate = (jnp.where(idx[:, 0:1] == e, g[:, 0:1], 0.0) + jnp.where(idx[:, 1:2] == e, g[:, 1:2], 0.0))
    acc_ref[...] += gate * _swiglu_partial(xn_ref[...], w1_ref, w3_ref, w2_ref)

    @pl.when(jnp.logical_and(e == pl.num_programs(0) - 1, j == pl.num_programs(1) - 1))
    def _():
        y = x_ref[...] + acc_ref[...]
        if final_norm:
            ms = jnp.mean(y * y, axis=-1, keepdims=True)
            y = y * lax.rsqrt(ms + NORM_EPS) * fw_ref[...]
        y_ref[...] = y


def _moe_small(x, nw, gates, e_idx, w1, w3, w2, fw, final_norm):
    n, d = x.shape
    n_experts, _, d_ff = w1.shape
    tf = _ff_tile(d_ff)
    full = lambda a: pl.BlockSpec(a.shape, lambda e, j: (0, 0))
    return pl.pallas_call(
        functools.partial(_moe_small_kernel, final_norm=final_norm),
        grid=(n_experts, d_ff // tf),
        in_specs=[full(x), full(nw), full(gates), full(e_idx),
                  pl.BlockSpec((None, d, tf), lambda e, j: (e, 0, j)),
                  pl.BlockSpec((None, d, tf), lambda e, j: (e, 0, j)),
                  pl.BlockSpec((None, tf, d), lambda e, j: (e, j, 0)),
                  full(fw)],
        out_specs=full(x),
        out_shape=jax.ShapeDtypeStruct((n, d), F32),
        scratch_shapes=[pltpu.VMEM((n, d), BF16), pltpu.VMEM((n, d), F32)],
        compiler_params=_params("arbitrary", "arbitrary"),
        name="moe_small",
    )(x, nw, gates, e_idx, w1, w3, w2, fw)


def _norm_kernel(x_ref, w_ref, o_ref):
    x = x_ref[...]
    ms = jnp.mean(x * x, axis=-1, keepdims=True)
    o_ref[...] = x * lax.rsqrt(ms + NORM_EPS) * w_ref[...]


def _final_norm(x, w):
    n, d = x.shape
    tm = _row_tile(n, 512)
    return pl.pallas_call(
        _norm_kernel,
        grid=(n // tm,),
        in_specs=[pl.BlockSpec((tm, d), lambda i: (i, 0)), pl.BlockSpec((1, d), lambda i: (0, 0))],
        out_specs=pl.BlockSpec((tm, d), lambda i: (i, 0)),
        out_shape=jax.ShapeDtypeStruct((n, d), F32),
        compiler_params=_params("parallel"),
        name="final_norm",
    )(x, w)


def _smix_kernel(p_ref, pf_ref, pw_ref, ps_ref, su_ref, sv_ref, sw0_ref, sb0_ref,
                 q_ref, k_ref, v_ref, ig_ref, fg_ref, c_ref, n_ref, m_ref, mo_ref, nw_ref,
                 ypool_ref, ysgu_ref, yml_ref, cn_ref, nn_ref, mn_ref, *, cnt):
    p = p_ref[...]
    lane_w = _window_of_lane(p.shape, 1, 0)
    acc = p
    wsum = jnp.zeros(p.shape, F32)
    for off in range(1, POOL_WINDOWS[-1] + 1):
        if off in POOL_WINDOWS:
            wsum = jnp.where(lane_w == off, acc, wsum)
        if off <= POOL_BUF:
            acc = acc + pf_ref[HALO - off]
    count = jnp.minimum(lane_w, cnt).astype(F32)
    d = (wsum / count - p).astype(BF16)
    ypool_ref[...] = jnp.dot(d, pw_ref[...], preferred_element_type=F32) * ps_ref[...]

    ysgu_ref[...] = su_ref[...] * (sw0_ref[...] * sv_ref[...] + sb0_ref[...])

    q = q_ref[...]
    ks = k_ref[...] * (HEAD_DIM ** -0.5)
    v = v_ref[...]
    ig = ig_ref[...]
    c = c_ref[...]
    nrow = n_ref[...]
    a = _logsig(fg_ref[...]) + m_ref[...]
    mt = jnp.maximum(a, ig)
    inter = jnp.exp(a - mt)
    e_i = jnp.exp(ig - mt)
    s = jnp.sum(q * ks, axis=-1, keepdims=True) * e_i
    c_q = jnp.sum(c * q, axis=-1, keepdims=True)
    n_q = jnp.sum(nrow * q, axis=-1, keepdims=True)
    num = s * v + inter * c_q
    den = s + inter * n_q
    hval = num / jnp.maximum(jnp.abs(den), jnp.exp(-mt))
    cn_ref[...] = inter * c + (e_i * v) * ks
    nn_ref[...] = inter * nrow + e_i * ks
    mn_ref[...] = mt
    ms = jnp.mean(hval * hval, axis=1, keepdims=True)
    y = hval * lax.rsqrt(ms + NORM_EPS) * nw_ref[...]
    yml_ref[...] = jax.nn.sigmoid(mo_ref[...]) * y


def _smix(args, out_shapes, cnt):
    return pl.pallas_call(
        functools.partial(_smix_kernel, cnt=cnt),
        out_shape=out_shapes,
        compiler_params=pltpu.CompilerParams(vmem_limit_bytes=VMEM_LIMIT),
        name="sample_mixers",
    )(*args)


PAGES_PER_STEP = 16
PAGES_PER_BLOCK = MOBA_BLOCK // PAGE_SIZE


def _paged_kmean_kernel(pt_ref, *refs):
    o_ref = refs[-1]
    j = pl.program_id(1)
    nblk = PAGES_PER_STEP // PAGES_PER_BLOCK

    @pl.when(j == 0)
    def _():
        o_ref[...] = jnp.zeros(o_ref.shape, F32)

    lane = lax.broadcasted_iota(I32, o_ref.shape, 1)
    acc = o_ref[...]
    for bb in range(nblk):
        s = refs[bb * PAGES_PER_BLOCK][...]
        for pg in range(1, PAGES_PER_BLOCK):
            s = s + refs[bb * PAGES_PER_BLOCK + pg][...]
        col = jnp.sum(s, axis=1, keepdims=True) * (1.0 / MOBA_BLOCK)
        acc = jnp.where(lane == j * nblk + bb, col, acc)
    o_ref[...] = acc


def _paged_kmean(cache_t, page_table, page_base):
    bs, n_pages = page_table.shape
    nb = n_pages // PAGES_PER_BLOCK
    page = lambda s: pl.BlockSpec((None, GROUP_W, PAGE_SIZE),
                                  lambda b, j, pt, s=s: (page_base + pt[b, j * PAGES_PER_STEP + s], 0, 0))
    grid_spec = pltpu.PrefetchScalarGridSpec(
        num_scalar_prefetch=1,
        grid=(bs, n_pages // PAGES_PER_STEP),
        in_specs=[page(s) for s in range(PAGES_PER_STEP)],
        out_specs=pl.BlockSpec((None, GROUP_W, nb), lambda b, j, pt: (b, 0, 0)))
    return pl.pallas_call(
        _paged_kmean_kernel,
        grid_spec=grid_spec,
        out_shape=jax.ShapeDtypeStruct((bs, GROUP_W, nb), F32),
        compiler_params=_params("parallel", "arbitrary"),
        name="paged_kmean",
    )(page_table, *([cache_t] * PAGES_PER_STEP))


def _sample_select_kernel(q_ref, kmt_ref, o_ref):
    head_row = lax.broadcasted_iota(I32, (8, GROUP_W), 0) == _lane_head((8, GROUP_W), 1)
    q8 = jnp.where(head_row, q_ref[...], 0.0)
    s = jnp.dot(q8, kmt_ref[...], precision=HIGHEST, preferred_element_type=F32)
    blk = lax.broadcasted_iota(I32, s.shape, 1)
    slot = lax.broadcasted_iota(I32, o_ref.shape, 1)
    out = jnp.zeros(o_ref.shape, I32)
    for r in range(MOBA_TOPK):
        mx = jnp.max(s, axis=1, keepdims=True)
        idx = jnp.min(jnp.where(s == mx, blk, s.shape[1]), axis=1, keepdims=True)
        out = jnp.where(slot == r, idx, out)
        s = jnp.where(blk == idx, NEG_PICKED, s)
    o_ref[...] = out


def _sample_select(q3, kmean_t):
    bs, _, nb = kmean_t.shape
    return pl.pallas_call(
        _sample_select_kernel,
        grid=(bs,),
        in_specs=[pl.BlockSpec((None, 1, GROUP_W), lambda b: (b, 0, 0)),
                  pl.BlockSpec((None, GROUP_W, nb), lambda b: (b, 0, 0))],
        out_specs=pl.BlockSpec((None, 8, LANE), lambda b: (b, 0, 0)),
        out_shape=jax.ShapeDtypeStruct((bs, 8, LANE), I32),
        compiler_params=_params("parallel"),
        name="sample_select",
    )(q3, kmean_t)


N_SEL_PAGES = MOBA_TOPK * PAGES_PER_BLOCK


def _sample_attn_kernel(ph_ref, q_ref, kn_ref, vn_ref, *refs):
    kp, vp, o_ref = refs[:N_SEL_PAGES], refs[N_SEL_PAGES:2 * N_SEL_PAGES], refs[-1]
    q = q_ref[...] * (HEAD_DIM ** -0.5)
    own = jnp.sum(kn_ref[...] * q, axis=0, keepdims=True)
    logits = [jnp.sum(r[...] * q, axis=0, keepdims=True) for r in kp]
    m = own
    for lg in logits:
        m = jnp.maximum(m, jnp.max(lg, axis=1, keepdims=True))
    p_own = jnp.exp(own - m)
    den = p_own
    acc = p_own * vn_ref[...]
    for lg, vr in zip(logits, vp):
        p = jnp.exp(lg - m)
        den = den + jnp.sum(p, axis=1, keepdims=True)
        acc = acc + jnp.sum(vr[...] * p, axis=1, keepdims=True)
    o_ref[...] = acc / den


def _sample_attn(phys, q_col, kn_col, vn_col, cache_kt, cache_vt, page_base):
    bs = q_col.shape[0]
    col = pl.BlockSpec((None, HEAD_DIM, 1), lambda b, h, ph: (b, h, 0))
    page = lambda s: pl.BlockSpec(
        (None, HEAD_DIM, PAGE_SIZE),
        lambda b, h, ph, s=s: (page_base + ph[(b * N_HEADS + h) * N_SEL_PAGES + s], h, 0))
    pages = [page(s) for s in range(N_SEL_PAGES)]
    grid_spec = pltpu.PrefetchScalarGridSpec(
        num_scalar_prefetch=1,
        grid=(bs, N_HEADS),
        in_specs=[col, col, col] + pages + pages,
        out_specs=col)
    return pl.pallas_call(
        _sample_attn_kernel,
        grid_spec=grid_spec,
        out_shape=jax.ShapeDtypeStruct((bs, GROUP_W, 1), F32),
        compiler_params=_params("parallel", "parallel"),
        name="sample_attn",
    )(phys, q_col, kn_col, vn_col, *([cache_kt] * N_SEL_PAGES), *([cache_vt] * N_SEL_PAGES))


def _prep_layer(w_in, w_out, pool_w, pool_scale, sgu_w, sgu_b, norm_w):
    d = w_in.shape[0]
    g = GROUP_W
    gate0 = 7 * g
    att0 = gate0 + 2 * N_HEADS
    pad = jnp.zeros((d, GATE_W - 2 * N_HEADS), w_in.dtype)
    w_z = jnp.concatenate([w_in[:, :gate0], w_in[:, att0:], w_in[:, gate0:att0], pad], axis=1).astype(BF16)
    wbd = jnp.zeros((g, g), F32)
    for i in range(len(POOL_WINDOWS)):
        sl = slice(i * POOL_GROUP, (i + 1) * POOL_GROUP)
        wbd = wbd.at[sl, sl].set(pool_w[i])
    return dict(
        w_z=w_z, w_out=w_out.astype(BF16), pool_wbd=wbd.astype(BF16),
        pool_scale=pool_scale.reshape(1, g),
        sgu_w=sgu_w, sgu_bias=jnp.repeat(sgu_b.T, HEAD_DIM, axis=1),
        sgu_w0=jnp.repeat(sgu_w[:, 0, 0], HEAD_DIM).reshape(1, g),
        sgu_b0=jnp.repeat(sgu_b[:, 0], HEAD_DIM).reshape(1, g),
        norm_w=norm_w.reshape(1, g))


def _head_cols(a, bs):
    return a.reshape(bs * N_HEADS, HEAD_DIM, 1)


def _head_rows(a, bs):
    return a.reshape(bs * N_HEADS, 1, HEAD_DIM)


def _mixers_prompt(x, batch, seq, nmw, lp, gate_b, rope_tabs):
    z = _inproj(x, nmw, lp["w_z"], rope_tabs)
    g = GROUP_W
    nb = seq // MOBA_BLOCK
    col = lambda c: z[:, c * g:(c + 1) * g]
    m_ab = _poolsgu(z, batch, seq, lp["pool_wbd"], lp["pool_scale"], lp["sgu_w"], lp["sgu_bias"])
    gates_t = z[:, 10 * g:10 * g + 2 * N_HEADS].T
    m_c, ct, nst, mst = _mlstm(z, gates_t, batch, seq, gate_b, lp["norm_w"])
    k, v = col(COL_AK), col(COL_AV)
    kb4 = k.astype(BF16).reshape(batch, nb, MOBA_BLOCK, g)
    vt4 = jnp.swapaxes(v.astype(BF16).reshape(batch, nb, MOBA_BLOCK, g), 2, 3)
    kmean = _kmean(z, batch * nb).reshape(batch, nb, g)
    m_d = _moba(z, kb4, vt4, kmean, batch, nb)
    h = _outproj(x, m_ab, m_c, m_d, lp["w_out"])

    heads = lambda a: a.reshape(batch, seq, N_HEADS, HEAD_DIM)
    c_new = jnp.stack([ct[:, i * HEAD_DIM:(i + 1) * HEAD_DIM, i * HEAD_DIM:(i + 1) * HEAD_DIM]
                       for i in range(N_HEADS)], axis=1)
    state = (heads(k), heads(v), col(COL_P).reshape(batch, seq, g)[:, seq - POOL_BUF:],
             jnp.swapaxes(c_new, -1, -2), nst.reshape(batch, N_HEADS, HEAD_DIM), mst[:, :N_HEADS, 0])
    return h, state


def _mixers_sample(x, past_len, nmw, lp, gate_b, rope_tabs, pool_state, c_st, n_st, m_st,
                   cache_k, cache_v, page_table, page_base):
    bs = x.shape[0]
    g = GROUP_W
    z = _inproj(x, nmw, lp["w_z"], rope_tabs)
    col = lambda c: z[:, c * g:(c + 1) * g]
    p, sv = col(COL_P), col(COL_SV)
    gates = z[:, 10 * g:10 * g + 2 * N_HEADS] + gate_b.reshape(1, -1)
    bh = bs * N_HEADS
    prefix = jnp.concatenate([jnp.zeros((1, bs, g), F32), jnp.swapaxes(pool_state, 0, 1)], axis=0)
    args = (p, prefix, lp["pool_wbd"], lp["pool_scale"], col(COL_SU), sv, lp["sgu_w0"], lp["sgu_b0"],
            _head_rows(col(COL_MQ), bs), _head_rows(col(COL_MK), bs), _head_cols(col(COL_MV), bs),
            gates[:, :N_HEADS].reshape(bh, 1, 1), gates[:, N_HEADS:].reshape(bh, 1, 1),
            c_st.reshape(bh, HEAD_DIM, HEAD_DIM), n_st.reshape(bh, 1, HEAD_DIM), m_st.reshape(bh, 1, 1),
            _head_cols(col(COL_MO), bs), _head_cols(jnp.tile(lp["norm_w"], (bs, 1)), bs))
    sds = jax.ShapeDtypeStruct
    outs = (sds((bs, g), F32), sds((bs, g), F32), sds((bh, HEAD_DIM, 1), F32),
            sds((bh, HEAD_DIM, HEAD_DIM), F32), sds((bh, 1, HEAD_DIM), F32), sds((bh, 1, 1), F32))
    y_pool, y_sgu, y_ml, c_new, n_new, m_new = _smix(args, outs, min(past_len + 1, POOL_WINDOWS[-1]))

    q_col, k_col, v_col = (col(c).reshape(bs, g, 1) for c in (COL_AQ, COL_AK, COL_AV))
    kmean_t = _paged_kmean(cache_k, page_table, page_base)
    sel = _sample_select(col(COL_AQ).reshape(bs, 1, g), kmean_t)[:, :N_HEADS, :MOBA_TOPK]
    pages = sel[..., None] * PAGES_PER_BLOCK + jnp.arange(PAGES_PER_BLOCK, dtype=I32)
    phys = jnp.take_along_axis(page_table, pages.reshape(bs, -1), axis=1).reshape(-1)
    y_at = _sample_attn(phys, q_col, k_col, v_col, cache_k, cache_v, page_base).reshape(bs, g)

    m_ab = jnp.concatenate([y_pool, y_sgu], axis=1).astype(BF16)
    h = _outproj(x, m_ab, y_ml.reshape(bs, g).astype(BF16), y_at.astype(BF16), lp["w_out"])

    heads = lambda a: a.reshape(bs, 1, N_HEADS, HEAD_DIM)
    state = (heads(col(COL_AK)), heads(col(COL_AV)),
             jnp.concatenate([pool_state[:, 1:], p[:, None, :]], axis=1), sv.reshape(bs, 1, g),
             c_new.reshape(bs, N_HEADS, HEAD_DIM, HEAD_DIM), n_new.reshape(bs, N_HEADS, HEAD_DIM),
             m_new.reshape(bs, N_HEADS))
    return h, state


def kernel(x_prompt, x_sample, cache_k, cache_v, page_table, state_pool, state_mlstm_c, state_mlstm_n,
           state_mlstm_m, norm_mix_w, norm_ffn_w, final_norm_w, w_in, w_out, pool_w, pool_scale, sgu_w,
           sgu_b, mlstm_gate_b, mlstm_norm_w, ffn_w1, ffn_w3, ffn_w2, router_w, router_b,
           moe_w1, moe_w3, moe_w2):
    batch, seq, d = x_prompt.shape
    bs = x_sample.shape[0]
    depth = w_in.shape[0]
    n_pool = cache_k.shape[1]
    n_pages = page_table.shape[1]
    past_len = n_pages * PAGE_SIZE
    n_experts = router_w.shape[-1]
    assert x_sample.shape[1] == 1 and d == N_MIXERS * GROUP_W
    assert seq % MOBA_BLOCK == 0 and seq % (2 * MLSTM_CHUNK) == 0 and seq >= POOL_BUF
    assert past_len % MOBA_BLOCK == 0 and past_len // MOBA_BLOCK >= MOBA_TOPK
    assert n_pages % PAGES_PER_STEP == 0

    rope_p = _rope_tables(jnp.arange(seq, dtype=I32))
    rope_s = _rope_tables(jnp.full((bs,), past_len, I32))
    pages_t = lambda c: jnp.transpose(c, (0, 1, 3, 4, 2)).reshape(depth * n_pool, GROUP_W, PAGE_SIZE)
    ck, cv = pages_t(cache_k), pages_t(cache_v)
    fw = final_norm_w.reshape(1, d)
    tm_moe = 512

    hp = x_prompt.reshape(batch * seq, d)
    hs = x_sample.reshape(bs, d)
    st_p, st_s = [], []
    for l in range(depth):
        lp = _prep_layer(w_in[l], w_out[l], pool_w[l], pool_scale[l], sgu_w[l], sgu_b[l], mlstm_norm_w[l])
        nmw = norm_mix_w[l].reshape(1, d)
        nfw = norm_ffn_w[l].reshape(1, d)
        hp, sp = _mixers_prompt(hp, batch, seq, nmw, lp, mlstm_gate_b[l], rope_p)
        hs, ss = _mixers_sample(hs, past_len, nmw, lp, mlstm_gate_b[l], rope_s, state_pool[l],
                                state_mlstm_c[l], state_mlstm_n[l], state_mlstm_m[l],
                                ck, cv, page_table, l * n_pool)
        st_p.append(sp)
        st_s.append(ss)
        last = l == depth - 1
        j = l // 2
        if l % 2 == 0:
            w1, w3, w2 = ffn_w1[j].astype(BF16), ffn_w3[j].astype(BF16), ffn_w2[j].astype(BF16)
            hp = _ffn(hp, nfw, w1, w3, w2)
            hs = _ffn(hs, nfw, w1, w3, w2)
            if last:
                hp, hs = _final_norm(hp, fw), _final_norm(hs, fw)
        else:
            w1, w3, w2 = moe_w1[j].astype(BF16), moe_w3[j].astype(BF16), moe_w2[j].astype(BF16)
            rw = jnp.zeros((d, LANE), F32).at[:, :n_experts].set(router_w[j])
            rb = jnp.full((1, LANE), -jnp.inf, F32).at[0, :n_experts].set(router_b[j])
            xp, gates, e_idx = _router(hp, nfw, rw, rb)
            pos, src, tile_expert, nused = _dispatch_plan(e_idx, n_experts, tm_moe)
            o_sorted = _gmm(xp, tile_expert, src, nused, w1, w3, w2, tm_moe)
            hp = _combine(pos, hp, gates, o_sorted, fw, last)
            _, gates_s, e_s = _router(hs, nfw, rw, rb)
            hs = _moe_small(hs, nfw, gates_s, e_s, w1, w3, w2, fw, last)

    stack = lambda sts, i: jnp.stack([s[i] for s in sts])
    return (hp.reshape(batch, seq, d), hs.reshape(bs, 1, d),
            stack(st_p, 0), stack(st_p, 1), stack(st_s, 0), stack(st_s, 1),
            stack(st_p, 2), stack(st_s, 2), stack(st_s, 3),
            stack(st_p, 3), stack(st_p, 4), stack(st_p, 5),
            stack(st_s, 4), stack(st_s, 5), stack(st_s, 6))
```

```python
import functools

import numpy as np
import jax
import jax.numpy as jnp
from jax import lax
from jax.experimental import pallas as pl
from jax.experimental.pallas import tpu as pltpu

F32 = jnp.float32
BF16 = jnp.bfloat16
I32 = jnp.int32
HIGHEST = lax.Precision.HIGHEST

N_MIXERS = 4
HEAD_DIM = 64
N_HEADS = 4
GROUP_W = N_HEADS * HEAD_DIM
POOL_WINDOWS = (2, 4, 8, 16)
POOL_GROUP = GROUP_W // len(POOL_WINDOWS)
POOL_BUF = max(POOL_WINDOWS) - 1
HALO = POOL_BUF + 1
SGU_CHUNK = 128
MLSTM_CHUNK = 64
MOBA_BLOCK = 256
MOBA_TOPK = 3
PAGE_SIZE = 128
ROPE_THETA = 500000.0
ROT_DIM = HEAD_DIM // 4
TOP_K = 2
NORM_EPS = 1e-6
NEG = -1e30
NEG_PICKED = -3e38

LANE = 128
GATE_W = LANE
Z_W = 10 * GROUP_W + GATE_W
Z_TN = 7 * LANE
COL_P, COL_SU, COL_SV, COL_MQ, COL_MK, COL_MV, COL_MO, COL_AQ, COL_AK, COL_AV = range(10)
COL_GATE = (10 * GROUP_W) // GATE_W
VMEM_LIMIT = 56 * 1024 * 1024
ISSUE_UNROLL = 8


def _params(*sem):
    return pltpu.CompilerParams(dimension_semantics=sem, vmem_limit_bytes=VMEM_LIMIT)


def _row_tile(n, pref):
    return pref if n % pref == 0 else n


def _lane_head(shape, dim):
    return lax.broadcasted_iota(I32, shape, dim) // HEAD_DIM


def _by_head(vals, lh):
    out = vals[N_HEADS - 1]
    for h in range(N_HEADS - 2, -1, -1):
        out = jnp.where(lh == h, vals[h], out)
    return out


def _dot_nt(a, b, precision=None):
    return lax.dot_general(a, b, (((1,), (1,)), ((), ())), precision=precision,
                           preferred_element_type=F32)


def _logsig(x):
    return jnp.minimum(x, 0.0) - jnp.log1p(jnp.exp(-jnp.abs(x)))


def _inproj_kernel(x_ref, nw_ref, w_ref, cos_ref, sa_ref, sb_ref, z_ref, xn_ref):
    j = pl.program_id(1)

    @pl.when(j == 0)
    def _():
        x = x_ref[...]
        ms = jnp.mean(x * x, axis=-1, keepdims=True)
        xn_ref[...] = (x * lax.rsqrt(ms + NORM_EPS) * nw_ref[...]).astype(BF16)

    z = jnp.dot(xn_ref[...], w_ref[...], preferred_element_type=F32)

    @pl.when(j < 2)
    def _():
        z_ref[...] = z

    @pl.when(j == 2)
    def _():
        c, sa, sb = cos_ref[...], sa_ref[...], sb_ref[...]
        half = ROT_DIM // 2
        for o in (0, GROUP_W):
            t = z[:, o:o + GROUP_W]
            z_ref[:, o:o + GROUP_W] = (t * c + pltpu.roll(t, GROUP_W - half, 1) * sa
                                       + pltpu.roll(t, half, 1) * sb)
        z_ref[:, 2 * GROUP_W:] = z[:, 2 * GROUP_W:]


def _inproj(x, nw, w, rope_tabs):
    n, d = x.shape
    tm = _row_tile(n, 512)
    tab_tiles = rope_tabs[0].shape[0] // tm
    tab_spec = pl.BlockSpec((tm, GROUP_W), lambda i, j: (i % tab_tiles, 0))
    return pl.pallas_call(
        _inproj_kernel,
        grid=(n // tm, Z_W // Z_TN),
        in_specs=[pl.BlockSpec((tm, d), lambda i, j: (i, 0)),
                  pl.BlockSpec((1, d), lambda i, j: (0, 0)),
                  pl.BlockSpec((d, Z_TN), lambda i, j: (0, j)),
                  tab_spec, tab_spec, tab_spec],
        out_specs=pl.BlockSpec((tm, Z_TN), lambda i, j: (i, j)),
        out_shape=jax.ShapeDtypeStruct((n, Z_W), F32),
        scratch_shapes=[pltpu.VMEM((tm, d), BF16)],
        compiler_params=_params("parallel", "arbitrary"),
        name="inproj",
    )(x, nw, w, *rope_tabs)


def _rope(t, c, sa, sb):
    half = ROT_DIM // 2
    return t * c + pltpu.roll(t, GROUP_W - half, 1) * sa + pltpu.roll(t, half, 1) * sb


def _inproj_prompt_kernel(x_ref, nw_ref, w_ref, cos_ref, sa_ref, sb_ref, kprev_hbm, vprev_hbm,
                          zm_ref, q_ref, g_ref, gt_ref, kb_ref, vt_ref, km_ref, knew_ref, vnew_ref,
                          xn_ref, *, tiles_per_seq):
    del kprev_hbm, vprev_hbm
    i = pl.program_id(0)
    j = pl.program_id(1)
    tm = x_ref.shape[0]
    g = GROUP_W

    @pl.when(j == 0)
    def _():
        x = x_ref[...]
        ms = jnp.mean(x * x, axis=-1, keepdims=True)
        xn_ref[...] = (x * lax.rsqrt(ms + NORM_EPS) * nw_ref[...]).astype(BF16)

    z = jnp.dot(xn_ref[...], w_ref[...], preferred_element_type=F32)

    @pl.when(j < 2)
    def _():
        zm_ref[...] = z

    @pl.when(j == 2)
    def _():
        c, sa, sb = cos_ref[...], sa_ref[...], sb_ref[...]
        q_ref[...] = _rope(z[:, 0:g], c, sa, sb)
        k = _rope(z[:, g:2 * g], c, sa, sb)
        v = z[:, 2 * g:3 * g]
        gates = z[:, 3 * g:]
        g_ref[...] = gates
        gt_ref[...] = gates.T[0:2 * N_HEADS, :]
        nblk = tm // MOBA_BLOCK
        kb_ref[...] = k.astype(BF16).reshape(nblk, MOBA_BLOCK, g)
        knew_ref[...] = k.T
        v_t = v.T
        vnew_ref[...] = v_t
        ti = i % tiles_per_seq

        @pl.when(ti == 0)
        def _():
            km_ref[...] = jnp.zeros(km_ref.shape, F32)

        km = km_ref[...]
        row = lax.broadcasted_iota(I32, km.shape, 0)
        for b in range(nblk):
            cols = slice(b * MOBA_BLOCK, (b + 1) * MOBA_BLOCK)
            vt_ref[b] = v_t[:, cols].astype(BF16)
            mean = jnp.sum(k[cols, :], axis=0, keepdims=True) * (1.0 / MOBA_BLOCK)
            km = jnp.where(row == ti * nblk + b, mean, km)
        km_ref[...] = km


def _inproj_prompt(x, nw, w, rope_tabs, batch, seq, layer, depth, knew, vnew):
    n, d = x.shape
    tm = 2 * MOBA_BLOCK
    assert seq % tm == 0
    tps = seq // tm
    nb = seq // MOBA_BLOCK
    nblk = tm // MOBA_BLOCK
    g = GROUP_W
    tab_spec = pl.BlockSpec((tm, g), lambda i, j: (i % tps, 0))
    slab = pl.BlockSpec((None, None, g, tm), lambda i, j: (layer, i // tps, 0, i % tps))
    blocks = pl.BlockSpec((nblk, MOBA_BLOCK, g), lambda i, j: (i, 0, 0))
    rows = lambda c: pl.BlockSpec((tm, c), lambda i, j: (i, 0))
    sds = jax.ShapeDtypeStruct
    slab_shape = sds((depth, batch, g, seq), F32)
    any_spec = pl.BlockSpec(memory_space=pl.ANY)
    return pl.pallas_call(
        functools.partial(_inproj_prompt_kernel, tiles_per_seq=tps),
        grid=(n // tm, Z_W // Z_TN),
        in_specs=[pl.BlockSpec((tm, d), lambda i, j: (i, 0)),
                  pl.BlockSpec((1, d), lambda i, j: (0, 0)),
                  pl.BlockSpec((d, Z_TN), lambda i, j: (0, j)),
                  tab_spec, tab_spec, tab_spec, any_spec, any_spec],
        out_specs=[pl.BlockSpec((tm, Z_TN), lambda i, j: (i, jnp.minimum(j, 1))),
                   rows(g), rows(GATE_W),
                   pl.BlockSpec((2 * N_HEADS, tm), lambda i, j: (0, i)),
                   blocks, blocks,
                   pl.BlockSpec((None, nb, g), lambda i, j: (i // tps, 0, 0)),
                   slab, slab],
        out_shape=[sds((n, 2 * Z_TN), F32), sds((n, g), F32), sds((n, GATE_W), F32),
                   sds((2 * N_HEADS, n), F32),
                   sds((batch * nb, MOBA_BLOCK, g), BF16), sds((batch * nb, g, MOBA_BLOCK), BF16),
                   sds((batch, nb, g), F32), slab_shape, slab_shape],
        input_output_aliases={6: 7, 7: 8},
        scratch_shapes=[pltpu.VMEM((tm, d), BF16)],
        compiler_params=_params("arbitrary", "arbitrary"),
        name="inproj_prompt",
    )(x, nw, w, *rope_tabs, knew, vnew)


def _rope_tables(pos):
    half = ROT_DIM // 2
    inv = 1.0 / (ROPE_THETA ** (jnp.arange(half, dtype=F32) / half))
    ang = pos.astype(F32)[:, None] * inv[None, :]
    cos, sin = jnp.cos(ang), jnp.sin(ang)
    t = pos.shape[0]
    rest = HEAD_DIM - ROT_DIM
    one = jnp.ones((t, rest), F32)
    zero = jnp.zeros((t, rest), F32)
    zh = jnp.zeros((t, half), F32)
    c = jnp.concatenate([cos, cos, one], axis=1)
    sa = jnp.concatenate([-sin, zh, zero], axis=1)
    sb = jnp.concatenate([zh, sin, zero], axis=1)
    tile = lambda a: jnp.tile(a, (1, N_HEADS))
    return tile(c), tile(sa), tile(sb)


def _window_of_lane(shape, dim, lo):
    g = (lax.broadcasted_iota(I32, shape, dim) + lo) // POOL_GROUP
    w = jnp.full(shape, POOL_WINDOWS[-1], I32)
    for i in range(len(POOL_WINDOWS) - 2, -1, -1):
        w = jnp.where(g == i, POOL_WINDOWS[i], w)
    return w


def _poolsgu_kernel(p_ref, su_ref, sv_ref, pw_ref, ps_ref, sw_ref, sb_ref, o_ref, ebuf):
    t = pl.program_id(1)
    tt = p_ref.shape[0]

    @pl.when(t == 0)
    def _():
        ebuf[0:HALO, :] = jnp.zeros((HALO, GROUP_W), F32)

    @pl.when(t > 0)
    def _():
        ebuf[0:HALO, :] = ebuf[tt:tt + HALO, :]

    ebuf[HALO:HALO + tt, :] = p_ref[...]

    pos1 = lax.broadcasted_iota(I32, (tt, LANE), 0) + t * tt + 1
    halves = []
    for half in range(2):
        lo = half * LANE
        wsmall, wbig = POOL_WINDOWS[2 * half], POOL_WINDOWS[2 * half + 1]
        e0 = ebuf[HALO:HALO + tt, lo:lo + LANE]
        acc = e0
        for off in range(1, wsmall):
            acc = acc + ebuf[HALO - off:HALO - off + tt, lo:lo + LANE]
        small = acc
        for off in range(wsmall, wbig):
            acc = acc + ebuf[HALO - off:HALO - off + tt, lo:lo + LANE]
        first = lax.broadcasted_iota(I32, (tt, LANE), 1) < POOL_GROUP
        wsum = jnp.where(first, small, acc)
        cnt = jnp.minimum(pos1, _window_of_lane((tt, LANE), 1, lo)).astype(F32)
        halves.append(wsum / cnt - e0)
    d = jnp.concatenate(halves, axis=1).astype(BF16)
    y_pool = jnp.dot(d, pw_ref[...], preferred_element_type=F32) * ps_ref[...]
    o_ref[:, 0:GROUP_W] = y_pool.astype(o_ref.dtype)

    row = lax.broadcasted_iota(I32, (SGU_CHUNK, SGU_CHUNK), 0)
    col = lax.broadcasted_iota(I32, (SGU_CHUNK, SGU_CHUNK), 1)
    ws = [jnp.where(row >= col, sw_ref[h], 0.0).astype(BF16) for h in range(N_HEADS)]
    lh = _lane_head((SGU_CHUNK, GROUP_W), 1)
    for c in range(tt // SGU_CHUNK):
        rows = slice(c * SGU_CHUNK, (c + 1) * SGU_CHUNK)
        v = sv_ref[rows, :].astype(BF16)
        mixed = _by_head([jnp.dot(ws[h], v, preferred_element_type=F32) for h in range(N_HEADS)], lh)
        y = su_ref[rows, :] * (mixed + sb_ref[...])
        o_ref[rows, GROUP_W:2 * GROUP_W] = y.astype(o_ref.dtype)


def _poolsgu(z, batch, seq, pool_wbd, pool_scale, sgu_w, sgu_bias):
    tt = _row_tile(seq, 512)
    tps = seq // tt
    zspec = lambda c: pl.BlockSpec((tt, GROUP_W), lambda b, t, c=c: (b * tps + t, c))
    return pl.pallas_call(
        _poolsgu_kernel,
        grid=(batch, tps),
        in_specs=[zspec(COL_P), zspec(COL_SU), zspec(COL_SV),
                  pl.BlockSpec((GROUP_W, GROUP_W), lambda b, t: (0, 0)),
                  pl.BlockSpec((1, GROUP_W), lambda b, t: (0, 0)),
                  pl.BlockSpec((N_HEADS, SGU_CHUNK, SGU_CHUNK), lambda b, t: (0, 0, 0)),
                  pl.BlockSpec((SGU_CHUNK, GROUP_W), lambda b, t: (0, 0))],
        out_specs=pl.BlockSpec((tt, 2 * GROUP_W), lambda b, t: (b * tps + t, 0)),
        out_shape=jax.ShapeDtypeStruct((batch * seq, 2 * GROUP_W), BF16),
        scratch_shapes=[pltpu.VMEM((HALO + tt, GROUP_W), F32)],
        compiler_params=_params("parallel", "arbitrary"),
        name="pool_sgu",
    )(z, z, z, pool_wbd, pool_scale, sgu_w, sgu_bias)


def _mlstm_kernel(q_ref, k_ref, v_ref, o_ref, g_ref, gt_ref, gbc_ref, gbr_ref, nw_ref,
                  y_ref, ct_ref, n_ref, m_ref):
    L = MLSTM_CHUNK

    @pl.when(pl.program_id(1) == 0)
    def _():
        ct_ref[...] = jnp.zeros(ct_ref.shape, F32)
        n_ref[...] = jnp.zeros(n_ref.shape, F32)
        m_ref[...] = jnp.zeros(m_ref.shape, F32)

    r_i = lax.broadcasted_iota(I32, (L, L), 0)
    c_i = lax.broadcasted_iota(I32, (L, L), 1)
    causal = r_i >= c_i
    tri_l = causal.astype(F32)
    tri_u = (r_i <= c_i).astype(F32)
    lh = _lane_head((L, GROUP_W), 1)
    bd = _lane_head((GROUP_W, GROUP_W), 0) == _lane_head((GROUP_W, GROUP_W), 1)
    same_head = bd.astype(F32)

    ct = ct_ref[...]
    nrow = n_ref[...]
    m_heads = [m_ref[h:h + 1, 0:1] for h in range(N_HEADS)]

    for c in range(q_ref.shape[0] // L):
        rows = slice(c * L, (c + 1) * L)
        q = q_ref[rows, :]
        ks = k_ref[rows, :] * (HEAD_DIM ** -0.5)
        v = v_ref[rows, :]
        g = g_ref[rows, :] + gbc_ref[...]
        gt = gt_ref[:, rows] + gbr_ref[...]
        bcol = jnp.dot(tri_l, _logsig(g), precision=HIGHEST, preferred_element_type=F32)
        brow = jnp.dot(_logsig(gt), tri_u, precision=HIGHEST, preferred_element_type=F32)
        qb, kb, vb = q.astype(BF16), ks.astype(BF16), v.astype(BF16)

        s_l, inter_l, deni_l, emt_l, wsrc_l, dec_l, mnew_l = [], [], [], [], [], [], []
        for h in range(N_HEADS):
            bc = bcol[:, N_HEADS + h:N_HEADS + h + 1]
            ic = g[:, h:h + 1]
            br = brow[N_HEADS + h:N_HEADS + h + 1, :]
            ir = gt[h:h + 1, :]
            dm = jnp.where(causal, bc - br + ir, NEG)
            a = bc + m_heads[h]
            mt = jnp.maximum(a, jnp.max(dm, axis=1, keepdims=True))
            qm = jnp.where(lh == h, q, 0.0).astype(BF16)
            s = _dot_nt(qm, kb) * jnp.exp(dm - mt)
            m_new = mt[L - 1:L, :]
            s_l.append(s)
            inter_l.append(jnp.exp(a - mt))
            deni_l.append(jnp.sum(s, axis=1, keepdims=True))
            emt_l.append(jnp.exp(-mt))
            wsrc_l.append(jnp.exp(bc[L - 1:L, :] - bc + ic - m_new))
            dec_l.append(jnp.exp(a[L - 1:L, :] - m_new))
            mnew_l.append(m_new)

        r = jnp.dot(jnp.concatenate(s_l, axis=0).astype(BF16), vb, preferred_element_type=F32)
        num_i = _by_head([r[h * L:(h + 1) * L, :] for h in range(N_HEADS)], lh)
        q_c = jnp.dot(qb, ct.astype(BF16), preferred_element_type=F32)
        q_n = jnp.dot(q * nrow, same_head, precision=HIGHEST, preferred_element_type=F32)
        inter = _by_head(inter_l, lh)
        num = num_i + inter * q_c
        den = _by_head(deni_l, lh) + inter * q_n
        hval = num / jnp.maximum(jnp.abs(den), _by_head(emt_l, lh))
        ms = jnp.dot(hval * hval, same_head, precision=HIGHEST,
                     preferred_element_type=F32) * (1.0 / HEAD_DIM)
        y = hval * lax.rsqrt(ms + NORM_EPS) * nw_ref[...]
        y_ref[rows, :] = (jax.nn.sigmoid(o_ref[rows, :]) * y).astype(y_ref.dtype)

        kw = ks * _by_head(wsrc_l, lh)
        dec = _by_head(dec_l, lh[0:1, :])
        upd = jnp.dot(kw.T.astype(BF16), vb, preferred_element_type=F32)
        ct = ct * dec + jnp.where(bd, upd, 0.0)
        nrow = nrow * dec + jnp.sum(kw, axis=0, keepdims=True)
        m_heads = mnew_l

    ct_ref[...] = ct
    n_ref[...] = nrow
    for h in range(N_HEADS):
        m_ref[h:h + 1, :] = jnp.broadcast_to(m_heads[h], (1, m_ref.shape[1]))


def _mlstm(z, gates, gates_t, batch, seq, gate_b, norm_w):
    rows = _row_tile(seq, 4 * MLSTM_CHUNK)
    tps = seq // rows
    zspec = lambda c: pl.BlockSpec((rows, GROUP_W), lambda b, t, c=c: (b * tps + t, c))
    gbc = jnp.zeros((1, GATE_W), F32).at[0, :2 * N_HEADS].set(gate_b.reshape(-1))
    gbr = gate_b.reshape(2 * N_HEADS, 1)
    state = lambda r, c: (pl.BlockSpec((None, r, c), lambda b, t: (b, 0, 0)),
                          jax.ShapeDtypeStruct((batch, r, c), F32))
    (cs, csh), (ns, nsh), (msp, msh) = state(GROUP_W, GROUP_W), state(1, GROUP_W), state(8, LANE)
    return pl.pallas_call(
        _mlstm_kernel,
        grid=(batch, tps),
        in_specs=[zspec(COL_MQ), zspec(COL_MK), zspec(COL_MV), zspec(COL_MO),
                  pl.BlockSpec((rows, GATE_W), lambda b, t: (b * tps + t, 0)),
                  pl.BlockSpec((2 * N_HEADS, rows), lambda b, t: (0, b * tps + t)),
                  pl.BlockSpec((1, GATE_W), lambda b, t: (0, 0)),
                  pl.BlockSpec((2 * N_HEADS, 1), lambda b, t: (0, 0)),
                  pl.BlockSpec((1, GROUP_W), lambda b, t: (0, 0))],
        out_specs=[pl.BlockSpec((rows, GROUP_W), lambda b, t: (b * tps + t, 0)), cs, ns, msp],
        out_shape=[jax.ShapeDtypeStruct((batch * seq, GROUP_W), BF16), csh, nsh, msh],
        compiler_params=_params("parallel", "arbitrary"),
        name="mlstm",
    )(z, z, z, z, gates, gates_t, gbc, gbr, norm_w)


def _pick_top_blocks(s, n_valid_mask):
    blk = lax.broadcasted_iota(I32, s.shape, 0)
    s = jnp.where(n_valid_mask, s, NEG)
    bias = jnp.full(s.shape, NEG, F32)
    for _ in range(MOBA_TOPK):
        mx = jnp.max(s, axis=0, keepdims=True)
        idx = jnp.min(jnp.where(s == mx, blk, s.shape[0]), axis=0, keepdims=True)
        pick = blk == idx
        bias = jnp.where(jnp.logical_and(pick, mx > 0.5 * NEG), 0.0, bias)
        s = jnp.where(pick, NEG_PICKED, s)
    return bias


MOBA_KEY_CHUNK = 64
LOG2E = 1.4426950408889634


def _moba_logits(k_blk, qbd_sc, lt_sc, buf):
    lt_sc[buf] = jnp.dot(k_blk[...], qbd_sc[...], preferred_element_type=F32)


def _moba_softmax_pv(h, buf, vt_blk, lt_sc, p_sc, m_sc, l_sc, acc_sc, bias, key_le_query):
    B = MOBA_BLOCK
    ch = MOBA_KEY_CHUNK
    groups = ch // 8
    first = bias is None
    cols = slice(h * B, (h + 1) * B)
    mx = None
    for c in range(B // ch):
        rows = slice(c * ch, (c + 1) * ch)
        s = lt_sc[buf, rows, cols]
        if first:
            s = jnp.where(key_le_query[rows, :], s, NEG)
            lt_sc[buf, rows, cols] = s
        part = jnp.max(s.reshape(groups, 8, B), axis=0)
        mx = part if mx is None else jnp.maximum(mx, part)
    cand = jnp.max(mx, axis=0, keepdims=True)
    if first:
        m_new = shift = cand
    else:
        m_old = m_sc[h:h + 1, :]
        m_new = jnp.maximum(m_old, cand + bias)
        alpha = jnp.exp2(m_old - m_new)
        shift = m_new - bias
    lsum = None
    for c in range(B // ch):
        rows = slice(c * ch, (c + 1) * ch)
        p = jnp.exp2(lt_sc[buf, rows, cols] - shift)
        part = jnp.sum(p.reshape(groups, 8, B), axis=0)
        lsum = part if lsum is None else lsum + part
        p_sc[h, rows, :] = p.astype(BF16)
    lnew = jnp.sum(lsum, axis=0, keepdims=True)
    hs = slice(h * HEAD_DIM, (h + 1) * HEAD_DIM)
    pv = jnp.dot(vt_blk[hs, :], p_sc[h], preferred_element_type=F32)
    if first:
        l_sc[h:h + 1, :] = lnew
        acc_sc[hs, :] = pv
    else:
        l_sc[h:h + 1, :] = alpha * l_sc[h:h + 1, :] + lnew
        acc_sc[hs, :] = alpha * acc_sc[hs, :] + pv
    m_sc[h:h + 1, :] = m_new


def _moba_kernel(q_ref, kb_ref, vt_ref, km_ref, o_ref, qbd_sc, sel_sc, m_sc, l_sc, acc_sc, lt_sc, p_sc):
    i = pl.program_id(1)
    nb = km_ref.shape[0]
    nbp = sel_sc.shape[0] // N_HEADS
    B = MOBA_BLOCK
    q = q_ref[...]
    km = km_ref[...]
    lh = _lane_head((B, GROUP_W), 1)
    past = lax.broadcasted_iota(I32, (nb, B), 0) < i
    key_le_query = lax.broadcasted_iota(I32, (B, B), 0) <= lax.broadcasted_iota(I32, (B, B), 1)
    stats = (lt_sc, p_sc, m_sc, l_sc, acc_sc)

    q_t = q.T * (HEAD_DIM ** -0.5 * LOG2E)
    row_head = _lane_head((GROUP_W, B), 0)
    for h in range(N_HEADS):
        s_blk = _dot_nt(km, jnp.where(lh == h, q, 0.0), precision=HIGHEST)
        sel_sc[h * nbp:h * nbp + nb, :] = _pick_top_blocks(s_blk, past)
        qbd_sc[:, h * B:(h + 1) * B] = jnp.where(row_head == h, q_t, 0.0).astype(BF16)

    _moba_logits(kb_ref.at[i], qbd_sc, lt_sc, 1)
    _moba_logits(kb_ref.at[0], qbd_sc, lt_sc, 0)
    for h in range(N_HEADS):
        _moba_softmax_pv(h, 1, vt_ref.at[i], *stats, None, key_le_query)

    def step(n, buf):
        _moba_logits(kb_ref.at[jnp.minimum(n + 1, nb - 1)], qbd_sc, lt_sc, 1 - buf)
        for h in range(N_HEADS):
            bias = sel_sc[pl.ds(h * nbp + n, 1), :]
            _moba_softmax_pv(h, buf, vt_ref.at[n], *stats, bias, None)

    def body(k, carry):
        step(2 * k, 0)
        step(2 * k + 1, 1)
        return carry

    lax.fori_loop(0, (i + 1) // 2, body, 0)

    for h in range(N_HEADS):
        hs = slice(h * HEAD_DIM, (h + 1) * HEAD_DIM)
        acc_sc[hs, :] = acc_sc[hs, :] / l_sc[h:h + 1, :]
    o_ref[...] = acc_sc[...].T.astype(o_ref.dtype)


def _moba(q, kb4, vt4, kmean, batch, nb):
    nbp = -(-nb // 8) * 8
    B = MOBA_BLOCK
    blk4 = pl.BlockSpec((None, nb, B, GROUP_W), lambda b, i: (b, 0, 0, 0))
    return pl.pallas_call(
        _moba_kernel,
        grid=(batch, nb),
        in_specs=[pl.BlockSpec((B, GROUP_W), lambda b, i: (b * nb + i, 0)),
                  blk4, blk4,
                  pl.BlockSpec((None, nb, GROUP_W), lambda b, i: (b, 0, 0))],
        out_specs=pl.BlockSpec((B, GROUP_W), lambda b, i: (b * nb + i, 0)),
        out_shape=jax.ShapeDtypeStruct((batch * nb * B, GROUP_W), BF16),
        scratch_shapes=[pltpu.VMEM((GROUP_W, N_HEADS * B), BF16),
                        pltpu.VMEM((N_HEADS * nbp, B), F32),
                        pltpu.VMEM((8, B), F32), pltpu.VMEM((8, B), F32),
                        pltpu.VMEM((GROUP_W, B), F32),
                        pltpu.VMEM((2, B, N_HEADS * B), F32),
                        pltpu.VMEM((N_HEADS, B, B), BF16)],
        compiler_params=_params("parallel", "arbitrary"),
        name="moba",
    )(q, kb4, vt4, kmean)


def _outproj_kernel(x_ref, ma_ref, mc_ref, md_ref, w_ref, o_ref):
    g2 = 2 * GROUP_W
    acc = jnp.dot(ma_ref[...], w_ref[0:g2, :], preferred_element_type=F32)
    acc += jnp.dot(mc_ref[...], w_ref[g2:g2 + GROUP_W, :], preferred_element_type=F32)
    acc += jnp.dot(md_ref[...], w_ref[g2 + GROUP_W:, :], preferred_element_type=F32)
    o_ref[...] = x_ref[...] + acc


def _outproj(x, ma, mc, md, w):
    n, d = x.shape
    tm = _row_tile(n, 512)
    rows = lambda c: pl.BlockSpec((tm, c), lambda i: (i, 0))
    return pl.pallas_call(
        _outproj_kernel,
        grid=(n // tm,),
        in_specs=[rows(d), rows(2 * GROUP_W), rows(GROUP_W), rows(GROUP_W),
                  pl.BlockSpec(w.shape, lambda i: (0, 0))],
        out_specs=rows(d),
        out_shape=jax.ShapeDtypeStruct((n, d), F32),
        compiler_params=_params("parallel"),
        name="outproj",
    )(x, ma, mc, md, w)


def _ff_tile(d_ff):
    return d_ff // 2 if (d_ff // 2) % LANE == 0 else d_ff


def _swiglu_partial(xn, w1_ref, w3_ref, w2_ref):
    a = jnp.dot(xn, w1_ref[...], preferred_element_type=F32)
    b = jnp.dot(xn, w3_ref[...], preferred_element_type=F32)
    hmid = (a * jax.nn.sigmoid(a) * b).astype(BF16)
    return jnp.dot(hmid, w2_ref[...], preferred_element_type=F32)


def _ffn_kernel(x_ref, nw_ref, w1_ref, w3_ref, w2_ref, o_ref, xn_ref, acc_ref):
    j = pl.program_id(1)

    @pl.when(j == 0)
    def _():
        x = x_ref[...]
        ms = jnp.mean(x * x, axis=-1, keepdims=True)
        xn_ref[...] = (x * lax.rsqrt(ms + NORM_EPS) * nw_ref[...]).astype(BF16)
        acc_ref[...] = jnp.zeros(acc_ref.shape, F32)

    acc_ref[...] += _swiglu_partial(xn_ref[...], w1_ref, w3_ref, w2_ref)

    @pl.when(j == pl.num_programs(1) - 1)
    def _():
        o_ref[...] = x_ref[...] + acc_ref[...]


def _ffn(x, nw, w1, w3, w2):
    n, d = x.shape
    d_ff = w1.shape[1]
    tm = _row_tile(n, 512)
    tf = _ff_tile(d_ff)
    return pl.pallas_call(
        _ffn_kernel,
        grid=(n // tm, d_ff // tf),
        in_specs=[pl.BlockSpec((tm, d), lambda i, j: (i, 0)),
                  pl.BlockSpec((1, d), lambda i, j: (0, 0)),
                  pl.BlockSpec((d, tf), lambda i, j: (0, j)),
                  pl.BlockSpec((d, tf), lambda i, j: (0, j)),
                  pl.BlockSpec((tf, d), lambda i, j: (j, 0))],
        out_specs=pl.BlockSpec((tm, d), lambda i, j: (i, 0)),
        out_shape=jax.ShapeDtypeStruct((n, d), F32),
        scratch_shapes=[pltpu.VMEM((tm, d), BF16), pltpu.VMEM((tm, d), F32)],
        compiler_params=_params("parallel", "arbitrary"),
        name="ffn_dense",
    )(x, nw, w1, w3, w2)


def _router_kernel(x_ref, nw_ref, rw_ref, rb_ref, xp_ref, g_ref, e_ref):
    x = x_ref[...]
    ms = jnp.mean(x * x, axis=-1, keepdims=True)
    xn = x * lax.rsqrt(ms + NORM_EPS) * nw_ref[...]
    tm, d = x.shape
    half = d // 2
    bits = pltpu.bitcast(xn.astype(BF16).astype(F32), jnp.uint32)
    packed = jnp.bitwise_or(jnp.right_shift(bits[:, :half], jnp.uint32(16)), bits[:, half:])
    nc = half // LANE
    for c in range(nc):
        xp_ref[pl.ds(c, tm, stride=nc), :] = packed[:, c * LANE:(c + 1) * LANE]

    logits = jnp.dot(xn, rw_ref[...], precision=HIGHEST, preferred_element_type=F32) + rb_ref[...]
    lane = lax.broadcasted_iota(I32, logits.shape, 1)
    mx1 = jnp.max(logits, axis=1, keepdims=True)
    i1 = jnp.min(jnp.where(logits == mx1, lane, LANE), axis=1, keepdims=True)
    rest = jnp.where(lane == i1, -jnp.inf, logits)
    mx2 = jnp.max(rest, axis=1, keepdims=True)
    i2 = jnp.min(jnp.where(rest == mx2, lane, LANE), axis=1, keepdims=True)
    e2 = jnp.exp(mx2 - mx1)
    g1 = 1.0 / (1.0 + e2)
    g_ref[...] = jnp.where(lane == 0, g1, jnp.where(lane == 1, e2 * g1, 0.0))
    e_ref[...] = jnp.where(lane == 0, i1, jnp.where(lane == 1, i2, 0))


def _router(x, nw, rw, rb):
    n, d = x.shape
    tm = _row_tile(n, 512)
    rows = lambda c: pl.BlockSpec((tm, c), lambda i: (i, 0))
    full = lambda a: pl.BlockSpec(a.shape, lambda i: (0, 0))
    return pl.pallas_call(
        _router_kernel,
        grid=(n // tm,),
        in_specs=[rows(d), full(nw), full(rw), full(rb)],
        out_specs=[pl.BlockSpec((tm * (d // 2 // LANE), LANE), lambda i: (i, 0)), rows(LANE), rows(LANE)],
        out_shape=[jax.ShapeDtypeStruct((n * (d // 2 // LANE), LANE), jnp.uint32),
                   jax.ShapeDtypeStruct((n, LANE), F32),
                   jax.ShapeDtypeStruct((n, LANE), I32)],
        compiler_params=_params("parallel"),
        name="router",
    )(x, nw, rw, rb)


def _unpack_rows(words):
    lo = pltpu.bitcast(jnp.left_shift(words, jnp.uint32(16)), F32)
    hi = pltpu.bitcast(jnp.bitwise_and(words, jnp.uint32(0xFFFF0000)), F32)
    return jnp.concatenate([lo, hi], axis=1).astype(BF16)


def _gmm_kernel(te_ref, src_ref, nused_ref, xp_hbm, w1_ref, w3_ref, w2_ref, o_ref,
                xbuf, sem, xn_ref, acc_ref):
    i = pl.program_id(0)
    j = pl.program_id(1)
    ntiles = pl.num_programs(0)
    tm, d = acc_ref.shape
    nc = xbuf.shape[1] // tm
    slot = i % 2

    def issue(tile, sl):
        def body(g, carry):
            for u in range(ISSUE_UNROLL):
                r = g * ISSUE_UNROLL + u
                tok = src_ref[tile * tm + r]
                pltpu.make_async_copy(xp_hbm.at[pl.ds(pl.multiple_of(tok * nc, nc), nc), :],
                                      xbuf.at[sl, pl.ds(pl.multiple_of(r * nc, nc), nc), :],
                                      sem.at[sl]).start(priority=u % 2)
            return carry
        lax.fori_loop(0, tm // ISSUE_UNROLL, body, 0)

    @pl.when(j == 0)
    def _():
        @pl.when(i == 0)
        def _():
            issue(0, 0)

        @pl.when(i + 1 < ntiles)
        def _():
            issue(i + 1, 1 - slot)

        pltpu.make_async_copy(xbuf.at[slot], xbuf.at[slot], sem.at[slot]).wait()
        words = jnp.concatenate([xbuf[slot, pl.ds(c, tm, stride=nc), :] for c in range(nc)], axis=1)
        xn_ref[...] = _unpack_rows(words)
        acc_ref[...] = jnp.zeros(acc_ref.shape, F32)

    @pl.when(i < nused_ref[0])
    def _():
        acc_ref[...] += _swiglu_partial(xn_ref[...], w1_ref, w3_ref, w2_ref)

    @pl.when(j == pl.num_programs(1) - 1)
    def _():
        for c in range(d // LANE):
            o_ref[pl.ds(c, tm, stride=d // LANE), :] = acc_ref[:, c * LANE:(c + 1) * LANE]


def _gmm(xp, tile_expert, src, nused, w1, w3, w2, tm):
    ncap = src.shape[0]
    d = w1.shape[1]
    half = d // 2
    d_ff = w1.shape[2]
    tf = _ff_tile(d_ff)
    oc = d // LANE
    grid_spec = pltpu.PrefetchScalarGridSpec(
        num_scalar_prefetch=3,
        grid=(ncap // tm, d_ff // tf),
        in_specs=[pl.BlockSpec(memory_space=pl.ANY),
                  pl.BlockSpec((None, d, tf), lambda i, j, te, s, nu: (te[i], 0, j)),
                  pl.BlockSpec((None, d, tf), lambda i, j, te, s, nu: (te[i], 0, j)),
                  pl.BlockSpec((None, tf, d), lambda i, j, te, s, nu: (te[i], j, 0))],
        out_specs=pl.BlockSpec((tm * oc, LANE), lambda i, j, te, s, nu: (i, 0)),
        scratch_shapes=[pltpu.VMEM((2, tm * (half // LANE), LANE), jnp.uint32),
                        pltpu.SemaphoreType.DMA((2,)),
                        pltpu.VMEM((tm, d), BF16),
                        pltpu.VMEM((tm, d), F32)])
    return pl.pallas_call(
        _gmm_kernel,
        grid_spec=grid_spec,
        out_shape=jax.ShapeDtypeStruct((ncap * oc, LANE), F32),
        compiler_params=_params("arbitrary", "arbitrary"),
        name="moe_gmm",
    )(tile_expert, src, nused, xp, w1, w3, w2)


def _dispatch_plan(e_idx, n_experts, tm):
    n = e_idx.shape[0]
    ef = e_idx[:, :TOP_K].T.reshape(-1)
    onehot = (ef[:, None] == jnp.arange(n_experts, dtype=I32)[None, :]).astype(I32)
    rank = jnp.sum((jnp.cumsum(onehot, axis=0) - 1) * onehot, axis=1)
    counts = jnp.sum(onehot, axis=0)
    padded = ((counts + tm - 1) // tm) * tm
    ends = jnp.cumsum(padded)
    pos = (ends - padded)[ef] + rank
    ncap = TOP_K * n + n_experts * tm
    src = jnp.zeros((ncap,), I32).at[pos].set(jnp.arange(TOP_K * n, dtype=I32) % n)
    tile_start = jnp.arange(ncap // tm, dtype=I32) * tm
    tile_expert = jnp.minimum(jnp.searchsorted(ends, tile_start, side="right"), n_experts - 1).astype(I32)
    nused = (ends[-1] // tm).astype(I32).reshape(1)
    return pos.astype(I32), src, tile_expert, nused


def _combine_kernel(pos_ref, h_ref, g_ref, o_hbm, fw_ref, y_ref, obuf, sem, *, final_norm):
    i = pl.program_id(0)
    ntiles = pl.num_programs(0)
    tc, d = h_ref.shape
    oc = d // LANE
    n = ntiles * tc
    slot = i % 2

    def issue(tile, sl):
        def body(g, carry):
            for u in range(ISSUE_UNROLL):
                r = g * ISSUE_UNROLL + u
                for k in range(TOP_K):
                    p = pos_ref[k * n + tile * tc + r]
                    pltpu.make_async_copy(o_hbm.at[pl.ds(pl.multiple_of(p * oc, oc), oc), :],
                                          obuf.at[sl, k, pl.ds(pl.multiple_of(r * oc, oc), oc), :],
                                          sem.at[sl]).start(priority=k)
            return carry
        lax.fori_loop(0, tc // ISSUE_UNROLL, body, 0)

    @pl.when(i == 0)
    def _():
        issue(0, 0)

    @pl.when(i + 1 < ntiles)
    def _():
        issue(i + 1, 1 - slot)

    pltpu.make_async_copy(obuf.at[slot], obuf.at[slot], sem.at[slot]).wait()
    g = g_ref[...]
    rows = lambda k: jnp.concatenate([obuf[slot, k, pl.ds(c, tc, stride=oc), :] for c in range(oc)], axis=1)
    y = h_ref[...] + g[:, 0:1] * rows(0) + g[:, 1:2] * rows(1)
    if final_norm:
        ms = jnp.mean(y * y, axis=-1, keepdims=True)
        y = y * lax.rsqrt(ms + NORM_EPS) * fw_ref[...]
    y_ref[...] = y


def _combine(pos, h, gates, o_sorted, fw, final_norm):
    n, d = h.shape
    tc = _row_tile(n, 256)
    grid_spec = pltpu.PrefetchScalarGridSpec(
        num_scalar_prefetch=1,
        grid=(n // tc,),
        in_specs=[pl.BlockSpec((tc, d), lambda i, p: (i, 0)),
                  pl.BlockSpec((tc, LANE), lambda i, p: (i, 0)),
                  pl.BlockSpec(memory_space=pl.ANY),
                  pl.BlockSpec((1, d), lambda i, p: (0, 0))],
        out_specs=pl.BlockSpec((tc, d), lambda i, p: (i, 0)),
        scratch_shapes=[pltpu.VMEM((2, TOP_K, tc * (d // LANE), LANE), F32), pltpu.SemaphoreType.DMA((2,))])
    return pl.pallas_call(
        functools.partial(_combine_kernel, final_norm=final_norm),
        grid_spec=grid_spec,
        out_shape=jax.ShapeDtypeStruct((n, d), F32),
        compiler_params=_params("arbitrary"),
        name="moe_combine",
    )(pos, h, gates, o_sorted, fw)


def _moe_small_kernel(x_ref, nw_ref, g_ref, e_ref, w1_ref, w3_ref, w2_ref, fw_ref, y_ref,
                      xn_ref, acc_ref, *, final_norm):
    e = pl.program_id(0)
    j = pl.program_id(1)

    @pl.when(jnp.logical_and(e == 0, j == 0))
    def _():
        x = x_ref[...]
        ms = jnp.mean(x * x, axis=-1, keepdims=True)
        xn_ref[...] = (x * lax.rsqrt(ms + NORM_EPS) * nw_ref[...]).astype(BF16)
        acc_ref[...] = jnp.zeros(acc_ref.shape, F32)

    g = g_ref[...]
    idx = e_ref[...]
    gate = (jnp.where(idx[:, 0:1] == e, g[:, 0:1], 0.0) + jnp.where(idx[:, 1:2] == e, g[:, 1:2], 0.0))
    acc_ref[...] += gate * _swiglu_partial(xn_ref[...], w1_ref, w3_ref, w2_ref)

    @pl.when(jnp.logical_and(e == pl.num_programs(0) - 1, j == pl.num_programs(1) - 1))
    def _():
        y = x_ref[...] + acc_ref[...]
        if final_norm:
            ms = jnp.mean(y * y, axis=-1, keepdims=True)
            y = y * lax.rsqrt(ms + NORM_EPS) * fw_ref[...]
        y_ref[...] = y


def _moe_small(x, nw, gates, e_idx, w1, w3, w2, fw, final_norm):
    n, d = x.shape
    n_experts, _, d_ff = w1.shape
    tf = _ff_tile(d_ff)
    full = lambda a: pl.BlockSpec(a.shape, lambda e, j: (0, 0))
    return pl.pallas_call(
        functools.partial(_moe_small_kernel, final_norm=final_norm),
        grid=(n_experts, d_ff // tf),
        in_specs=[full(x), full(nw), full(gates), full(e_idx),
                  pl.BlockSpec((None, d, tf), lambda e, j: (e, 0, j)),
                  pl.BlockSpec((None, d, tf), lambda e, j: (e, 0, j)),
                  pl.BlockSpec((None, tf, d), lambda e, j: (e, j, 0)),
                  full(fw)],
        out_specs=full(x),
        out_shape=jax.ShapeDtypeStruct((n, d), F32),
        scratch_shapes=[pltpu.VMEM((n, d), BF16), pltpu.VMEM((n, d), F32)],
        compiler_params=_params("arbitrary", "arbitrary"),
        name="moe_small",
    )(x, nw, gates, e_idx, w1, w3, w2, fw)


def _norm_kernel(x_ref, w_ref, o_ref):
    x = x_ref[...]
    ms = jnp.mean(x * x, axis=-1, keepdims=True)
    o_ref[...] = x * lax.rsqrt(ms + NORM_EPS) * w_ref[...]


def _final_norm(x, w):
    n, d = x.shape
    tm = _row_tile(n, 512)
    return pl.pallas_call(
        _norm_kernel,
        grid=(n // tm,),
        in_specs=[pl.BlockSpec((tm, d), lambda i: (i, 0)), pl.BlockSpec((1, d), lambda i: (0, 0))],
        out_specs=pl.BlockSpec((tm, d), lambda i: (i, 0)),
        out_shape=jax.ShapeDtypeStruct((n, d), F32),
        compiler_params=_params("parallel"),
        name="final_norm",
    )(x, w)


def _smix_kernel(p_ref, pf_ref, pw_ref, ps_ref, su_ref, sv_ref, sw0_ref, sb0_ref,
                 q_ref, k_ref, v_ref, ig_ref, fg_ref, c_ref, n_ref, m_ref, mo_ref, nw_ref,
                 ypool_ref, ysgu_ref, yml_ref, cn_ref, nn_ref, mn_ref, *, cnt):
    p = p_ref[...]
    lane_w = _window_of_lane(p.shape, 1, 0)
    acc = p
    wsum = jnp.zeros(p.shape, F32)
    for off in range(1, POOL_WINDOWS[-1] + 1):
        if off in POOL_WINDOWS:
            wsum = jnp.where(lane_w == off, acc, wsum)
        if off <= POOL_BUF:
            acc = acc + pf_ref[HALO - off]
    count = jnp.minimum(lane_w, cnt).astype(F32)
    d = (wsum / count - p).astype(BF16)
    ypool_ref[...] = jnp.dot(d, pw_ref[...], preferred_element_type=F32) * ps_ref[...]

    ysgu_ref[...] = su_ref[...] * (sw0_ref[...] * sv_ref[...] + sb0_ref[...])

    q = q_ref[...]
    ks = k_ref[...] * (HEAD_DIM ** -0.5)
    v = v_ref[...]
    ig = ig_ref[...]
    c = c_ref[...]
    nrow = n_ref[...]
    a = _logsig(fg_ref[...]) + m_ref[...]
    mt = jnp.maximum(a, ig)
    inter = jnp.exp(a - mt)
    e_i = jnp.exp(ig - mt)
    s = jnp.sum(q * ks, axis=-1, keepdims=True) * e_i
    c_q = jnp.sum(c * q, axis=-1, keepdims=True)
    n_q = jnp.sum(nrow * q, axis=-1, keepdims=True)
    num = s * v + inter * c_q
    den = s + inter * n_q
    hval = num / jnp.maximum(jnp.abs(den), jnp.exp(-mt))
    cn_ref[...] = inter * c + (e_i * v) * ks
    nn_ref[...] = inter * nrow + e_i * ks
    mn_ref[...] = mt
    ms = jnp.mean(hval * hval, axis=1, keepdims=True)
    y = hval * lax.rsqrt(ms + NORM_EPS) * nw_ref[...]
    yml_ref[...] = jax.nn.sigmoid(mo_ref[...]) * y


def _smix(args, out_shapes, cnt):
    return pl.pallas_call(
        functools.partial(_smix_kernel, cnt=cnt),
        out_shape=out_shapes,
        compiler_params=pltpu.CompilerParams(vmem_limit_bytes=VMEM_LIMIT),
        name="sample_mixers",
    )(*args)


PAGES_PER_STEP = 32
PAGES_PER_BLOCK = MOBA_BLOCK // PAGE_SIZE


def _paged_kmean_kernel(pt_ref, *refs):
    o_ref = refs[-1]
    j = pl.program_id(1)
    nblk = PAGES_PER_STEP // PAGES_PER_BLOCK

    @pl.when(j == 0)
    def _():
        o_ref[...] = jnp.zeros(o_ref.shape, F32)

    lane = lax.broadcasted_iota(I32, o_ref.shape, 1)
    acc = o_ref[...]
    for bb in range(nblk):
        s = refs[bb * PAGES_PER_BLOCK][...]
        for pg in range(1, PAGES_PER_BLOCK):
            s = s + refs[bb * PAGES_PER_BLOCK + pg][...]
        col = jnp.sum(s, axis=1, keepdims=True) * (1.0 / MOBA_BLOCK)
        acc = jnp.where(lane == j * nblk + bb, col, acc)
    o_ref[...] = acc


def _paged_kmean(cache_t, page_table, page_base):
    bs, n_pages = page_table.shape
    nb = n_pages // PAGES_PER_BLOCK
    page = lambda s: pl.BlockSpec((None, GROUP_W, PAGE_SIZE),
                                  lambda b, j, pt, s=s: (page_base + pt[b, j * PAGES_PER_STEP + s], 0, 0))
    grid_spec = pltpu.PrefetchScalarGridSpec(
        num_scalar_prefetch=1,
        grid=(bs, n_pages // PAGES_PER_STEP),
        in_specs=[page(s) for s in range(PAGES_PER_STEP)],
        out_specs=pl.BlockSpec((None, GROUP_W, nb), lambda b, j, pt: (b, 0, 0)))
    return pl.pallas_call(
        _paged_kmean_kernel,
        grid_spec=grid_spec,
        out_shape=jax.ShapeDtypeStruct((bs, GROUP_W, nb), F32),
        compiler_params=_params("parallel", "arbitrary"),
        name="paged_kmean",
    )(page_table, *([cache_t] * PAGES_PER_STEP))


def _sample_select_kernel(q_ref, kmt_ref, o_ref):
    head_row = lax.broadcasted_iota(I32, (8, GROUP_W), 0) == _lane_head((8, GROUP_W), 1)
    q8 = jnp.where(head_row, q_ref[...], 0.0)
    s = jnp.dot(q8, kmt_ref[...], precision=HIGHEST, preferred_element_type=F32)
    blk = lax.broadcasted_iota(I32, s.shape, 1)
    slot = lax.broadcasted_iota(I32, o_ref.shape, 1)
    out = jnp.zeros(o_ref.shape, I32)
    for r in range(MOBA_TOPK):
        mx = jnp.max(s, axis=1, keepdims=True)
        idx = jnp.min(jnp.where(s == mx, blk, s.shape[1]), axis=1, keepdims=True)
        out = jnp.where(slot == r, idx, out)
        s = jnp.where(blk == idx, NEG_PICKED, s)
    o_ref[...] = out


def _sample_select(q3, kmean_t):
    bs, _, nb = kmean_t.shape
    return pl.pallas_call(
        _sample_select_kernel,
        grid=(bs,),
        in_specs=[pl.BlockSpec((None, 1, GROUP_W), lambda b: (b, 0, 0)),
                  pl.BlockSpec((None, GROUP_W, nb), lambda b: (b, 0, 0))],
        out_specs=pl.BlockSpec((None, 8, LANE), lambda b: (b, 0, 0)),
        out_shape=jax.ShapeDtypeStruct((bs, 8, LANE), I32),
        compiler_params=_params("parallel"),
        name="sample_select",
    )(q3, kmean_t)


N_SEL_PAGES = MOBA_TOPK * PAGES_PER_BLOCK


def _sample_attn_kernel(ph_ref, q_ref, kn_ref, vn_ref, *refs):
    kp, vp, o_ref = refs[:N_SEL_PAGES], refs[N_SEL_PAGES:2 * N_SEL_PAGES], refs[-1]
    q = q_ref[...] * (HEAD_DIM ** -0.5)
    own = jnp.sum(kn_ref[...] * q, axis=0, keepdims=True)
    logits = [jnp.sum(r[...] * q, axis=0, keepdims=True) for r in kp]
    m = own
    for lg in logits:
        m = jnp.maximum(m, jnp.max(lg, axis=1, keepdims=True))
    p_own = jnp.exp(own - m)
    den = p_own
    acc = p_own * vn_ref[...]
    for lg, vr in zip(logits, vp):
        p = jnp.exp(lg - m)
        den = den + jnp.sum(p, axis=1, keepdims=True)
        acc = acc + jnp.sum(vr[...] * p, axis=1, keepdims=True)
    o_ref[...] = acc / den


def _sample_attn(phys, q_col, kn_col, vn_col, cache_kt, cache_vt, page_base):
    bs = q_col.shape[0]
    col = pl.BlockSpec((None, HEAD_DIM, 1), lambda b, h, ph: (b, h, 0))
    page = lambda s: pl.BlockSpec(
        (None, HEAD_DIM, PAGE_SIZE),
        lambda b, h, ph, s=s: (page_base + ph[(b * N_HEADS + h) * N_SEL_PAGES + s], h, 0))
    pages = [page(s) for s in range(N_SEL_PAGES)]
    grid_spec = pltpu.PrefetchScalarGridSpec(
        num_scalar_prefetch=1,
        grid=(bs, N_HEADS),
        in_specs=[col, col, col] + pages + pages,
        out_specs=col)
    return pl.pallas_call(
        _sample_attn_kernel,
        grid_spec=grid_spec,
        out_shape=jax.ShapeDtypeStruct((bs, GROUP_W, 1), F32),
        compiler_params=_params("parallel", "parallel"),
        name="sample_attn",
    )(phys, q_col, kn_col, vn_col, *([cache_kt] * N_SEL_PAGES), *([cache_vt] * N_SEL_PAGES))


def _prep_layer(w_in, w_out, pool_w, pool_scale, sgu_w, sgu_b, norm_w):
    d = w_in.shape[0]
    g = GROUP_W
    gate0 = 7 * g
    att0 = gate0 + 2 * N_HEADS
    pad = jnp.zeros((d, GATE_W - 2 * N_HEADS), w_in.dtype)
    w_z = jnp.concatenate([w_in[:, :gate0], w_in[:, att0:], w_in[:, gate0:att0], pad], axis=1).astype(BF16)
    wbd = jnp.zeros((g, g), F32)
    for i in range(len(POOL_WINDOWS)):
        sl = slice(i * POOL_GROUP, (i + 1) * POOL_GROUP)
        wbd = wbd.at[sl, sl].set(pool_w[i])
    return dict(
        w_z=w_z, w_out=w_out.astype(BF16), pool_wbd=wbd.astype(BF16),
        pool_scale=pool_scale.reshape(1, g),
        sgu_w=sgu_w, sgu_bias=jnp.repeat(sgu_b.T, HEAD_DIM, axis=1),
        sgu_w0=jnp.repeat(sgu_w[:, 0, 0], HEAD_DIM).reshape(1, g),
        sgu_b0=jnp.repeat(sgu_b[:, 0], HEAD_DIM).reshape(1, g),
        norm_w=norm_w.reshape(1, g))


def _head_cols(a, bs):
    return a.reshape(bs * N_HEADS, HEAD_DIM, 1)


def _head_rows(a, bs):
    return a.reshape(bs * N_HEADS, 1, HEAD_DIM)


def _mixers_prompt(x, batch, seq, nmw, lp, gate_b, rope_tabs, layer, depth, knew, vnew):
    g = GROUP_W
    nb = seq // MOBA_BLOCK
    zm, q, gates, gates_t, kb, vt, kmean, knew, vnew = _inproj_prompt(
        x, nmw, lp["w_z"], rope_tabs, batch, seq, layer, depth, knew, vnew)
    m_ab = _poolsgu(zm, batch, seq, lp["pool_wbd"], lp["pool_scale"], lp["sgu_w"], lp["sgu_bias"])
    m_c, ct, nst, mst = _mlstm(zm, gates, gates_t, batch, seq, gate_b, lp["norm_w"])
    blocks = lambda a: a.reshape(batch, nb, MOBA_BLOCK, g)
    m_d = _moba(q, blocks(kb), blocks(vt), kmean, batch, nb)
    h = _outproj(x, m_ab, m_c, m_d, lp["w_out"])

    c_new = jnp.stack([ct[:, i * HEAD_DIM:(i + 1) * HEAD_DIM, i * HEAD_DIM:(i + 1) * HEAD_DIM]
                       for i in range(N_HEADS)], axis=1)
    state = (zm[:, :g].reshape(batch, seq, g)[:, seq - POOL_BUF:],
             jnp.swapaxes(c_new, -1, -2), nst.reshape(batch, N_HEADS, HEAD_DIM), mst[:, :N_HEADS, 0])
    return h, state, knew, vnew


def _mixers_sample(x, past_len, nmw, lp, gate_b, rope_tabs, pool_state, c_st, n_st, m_st,
                   cache_k, cache_v, page_table, page_base):
    bs = x.shape[0]
    g = GROUP_W
    z = _inproj(x, nmw, lp["w_z"], rope_tabs)
    col = lambda c: z[:, c * g:(c + 1) * g]
    p, sv = col(COL_P), col(COL_SV)
    gates = z[:, 10 * g:10 * g + 2 * N_HEADS] + gate_b.reshape(1, -1)
    bh = bs * N_HEADS
    prefix = jnp.concatenate([jnp.zeros((1, bs, g), F32), jnp.swapaxes(pool_state, 0, 1)], axis=0)
    args = (p, prefix, lp["pool_wbd"], lp["pool_scale"], col(COL_SU), sv, lp["sgu_w0"], lp["sgu_b0"],
            _head_rows(col(COL_MQ), bs), _head_rows(col(COL_MK), bs), _head_cols(col(COL_MV), bs),
            gates[:, :N_HEADS].reshape(bh, 1, 1), gates[:, N_HEADS:].reshape(bh, 1, 1),
            c_st.reshape(bh, HEAD_DIM, HEAD_DIM), n_st.reshape(bh, 1, HEAD_DIM), m_st.reshape(bh, 1, 1),
            _head_cols(col(COL_MO), bs), _head_cols(jnp.tile(lp["norm_w"], (bs, 1)), bs))
    sds = jax.ShapeDtypeStruct
    outs = (sds((bs, g), F32), sds((bs, g), F32), sds((bh, HEAD_DIM, 1), F32),
            sds((bh, HEAD_DIM, HEAD_DIM), F32), sds((bh, 1, HEAD_DIM), F32), sds((bh, 1, 1), F32))
    y_pool, y_sgu, y_ml, c_new, n_new, m_new = _smix(args, outs, min(past_len + 1, POOL_WINDOWS[-1]))

    q_col, k_col, v_col = (col(c).reshape(bs, g, 1) for c in (COL_AQ, COL_AK, COL_AV))
    kmean_t = _paged_kmean(cache_k, page_table, page_base)
    sel = _sample_select(col(COL_AQ).reshape(bs, 1, g), kmean_t)[:, :N_HEADS, :MOBA_TOPK]
    pages = sel[..., None] * PAGES_PER_BLOCK + jnp.arange(PAGES_PER_BLOCK, dtype=I32)
    phys = jnp.take_along_axis(page_table, pages.reshape(bs, -1), axis=1).reshape(-1)
    y_at = _sample_attn(phys, q_col, k_col, v_col, cache_k, cache_v, page_base).reshape(bs, g)

    m_ab = jnp.concatenate([y_pool, y_sgu], axis=1).astype(BF16)
    h = _outproj(x, m_ab, y_ml.reshape(bs, g).astype(BF16), y_at.astype(BF16), lp["w_out"])

    heads = lambda a: a.reshape(bs, 1, N_HEADS, HEAD_DIM)
    state = (heads(col(COL_AK)), heads(col(COL_AV)),
             jnp.concatenate([pool_state[:, 1:], p[:, None, :]], axis=1), sv.reshape(bs, 1, g),
             c_new.reshape(bs, N_HEADS, HEAD_DIM, HEAD_DIM), n_new.reshape(bs, N_HEADS, HEAD_DIM),
             m_new.reshape(bs, N_HEADS))
    return h, state


def kernel(x_prompt, x_sample, cache_k, cache_v, page_table, state_pool, state_mlstm_c, state_mlstm_n,
           state_mlstm_m, norm_mix_w, norm_ffn_w, final_norm_w, w_in, w_out, pool_w, pool_scale, sgu_w,
           sgu_b, mlstm_gate_b, mlstm_norm_w, ffn_w1, ffn_w3, ffn_w2, router_w, router_b,
           moe_w1, moe_w3, moe_w2):
    batch, seq, d = x_prompt.shape
    bs = x_sample.shape[0]
    depth = w_in.shape[0]
    n_pool = cache_k.shape[1]
    n_pages = page_table.shape[1]
    past_len = n_pages * PAGE_SIZE
    n_experts = router_w.shape[-1]
    assert x_sample.shape[1] == 1 and d == N_MIXERS * GROUP_W
    assert seq % MOBA_BLOCK == 0 and seq % (2 * MLSTM_CHUNK) == 0 and seq >= POOL_BUF
    assert past_len % MOBA_BLOCK == 0 and past_len // MOBA_BLOCK >= MOBA_TOPK
    assert n_pages % PAGES_PER_STEP == 0

    rope_p = _rope_tables(jnp.arange(seq, dtype=I32))
    rope_s = _rope_tables(jnp.full((bs,), past_len, I32))
    pages_t = lambda c: jnp.transpose(c, (0, 1, 3, 4, 2)).reshape(depth * n_pool, GROUP_W, PAGE_SIZE)
    ck, cv = pages_t(cache_k), pages_t(cache_v)
    fw = final_norm_w.reshape(1, d)
    tm_moe = 512

    hp = x_prompt.reshape(batch * seq, d)
    hs = x_sample.reshape(bs, d)
    st_p, st_s = [], []
    knew = jnp.zeros((depth, batch, GROUP_W, seq), F32)
    vnew = jnp.zeros((depth, batch, GROUP_W, seq), F32)
    for l in range(depth):
        lp = _prep_layer(w_in[l], w_out[l], pool_w[l], pool_scale[l], sgu_w[l], sgu_b[l], mlstm_norm_w[l])
        nmw = norm_mix_w[l].reshape(1, d)
        nfw = norm_ffn_w[l].reshape(1, d)
        hp, sp, knew, vnew = _mixers_prompt(hp, batch, seq, nmw, lp, mlstm_gate_b[l], rope_p,
                                            l, depth, knew, vnew)
        hs, ss = _mixers_sample(hs, past_len, nmw, lp, mlstm_gate_b[l], rope_s, state_pool[l],
                                state_mlstm_c[l], state_mlstm_n[l], state_mlstm_m[l],
                                ck, cv, page_table, l * n_pool)
        st_p.append(sp)
        st_s.append(ss)
        last = l == depth - 1
        j = l // 2
        if l % 2 == 0:
            w1, w3, w2 = ffn_w1[j].astype(BF16), ffn_w3[j].astype(BF16), ffn_w2[j].astype(BF16)
            hp = _ffn(hp, nfw, w1, w3, w2)
            hs = _ffn(hs, nfw, w1, w3, w2)
            if last:
                hp, hs = _final_norm(hp, fw), _final_norm(hs, fw)
        else:
            w1, w3, w2 = moe_w1[j].astype(BF16), moe_w3[j].astype(BF16), moe_w2[j].astype(BF16)
            rw = jnp.zeros((d, LANE), F32).at[:, :n_experts].set(router_w[j])
            rb = jnp.full((1, LANE), -jnp.inf, F32).at[0, :n_experts].set(router_b[j])
            xp, gates, e_idx = _router(hp, nfw, rw, rb)
            pos, src, tile_expert, nused = _dispatch_plan(e_idx, n_experts, tm_moe)
            o_sorted = _gmm(xp, tile_expert, src, nused, w1, w3, w2, tm_moe)
            hp = _combine(pos, hp, gates, o_sorted, fw, last)
            _, gates_s, e_s = _router(hs, nfw, rw, rb)
            hs = _moe_small(hs, nfw, gates_s, e_s, w1, w3, w2, fw, last)

    stack = lambda sts, i: jnp.stack([s[i] for s in sts])
    kv_leaf = lambda a: jnp.transpose(a.reshape(depth, batch, N_HEADS, HEAD_DIM, seq), (0, 1, 4, 2, 3))
    return (hp.reshape(batch, seq, d), hs.reshape(bs, 1, d),
            kv_leaf(knew), kv_leaf(vnew), stack(st_s, 0), stack(st_s, 1),
            stack(st_p, 0), stack(st_s, 2), stack(st_s, 3),
            stack(st_p, 1), stack(st_p, 2), stack(st_p, 3),
            stack(st_s, 4), stack(st_s, 5), stack(st_s, 6))
```

```python
import functools

import numpy as np
import jax
import jax.numpy as jnp
from jax import lax
from jax.experimental import pallas as pl
from jax.experimental.pallas import tpu as pltpu

F32 = jnp.float32
BF16 = jnp.bfloat16
I32 = jnp.int32
HIGHEST = lax.Precision.HIGHEST

N_MIXERS = 4
HEAD_DIM = 64
N_HEADS = 4
GROUP_W = N_HEADS * HEAD_DIM
POOL_WINDOWS = (2, 4, 8, 16)
POOL_GROUP = GROUP_W // len(POOL_WINDOWS)
POOL_BUF = max(POOL_WINDOWS) - 1
HALO = POOL_BUF + 1
SGU_CHUNK = 128
MLSTM_CHUNK = 64
MOBA_BLOCK = 256
MOBA_TOPK = 3
PAGE_SIZE = 128
ROPE_THETA = 500000.0
ROT_DIM = HEAD_DIM // 4
TOP_K = 2
NORM_EPS = 1e-6
NEG = -1e30
NEG_PICKED = -3e38

LANE = 128
GATE_W = LANE
Z_W = 10 * GROUP_W + GATE_W
Z_TN = 7 * LANE
COL_P, COL_SU, COL_SV, COL_MQ, COL_MK, COL_MV, COL_MO, COL_AQ, COL_AK, COL_AV = range(10)
COL_GATE = (10 * GROUP_W) // GATE_W
VMEM_LIMIT = 56 * 1024 * 1024
ISSUE_UNROLL = 8


def _params(*sem):
    return pltpu.CompilerParams(dimension_semantics=sem, vmem_limit_bytes=VMEM_LIMIT)


def _row_tile(n, pref):
    return pref if n % pref == 0 else n


def _lane_head(shape, dim):
    return lax.broadcasted_iota(I32, shape, dim) // HEAD_DIM


def _by_head(vals, lh):
    out = vals[N_HEADS - 1]
    for h in range(N_HEADS - 2, -1, -1):
        out = jnp.where(lh == h, vals[h], out)
    return out


def _dot_nt(a, b, precision=None):
    return lax.dot_general(a, b, (((1,), (1,)), ((), ())), precision=precision,
                           preferred_element_type=F32)


def _logsig(x):
    return jnp.minimum(x, 0.0) - jnp.log1p(jnp.exp(-jnp.abs(x)))


def _inproj_kernel(x_ref, nw_ref, w_ref, cos_ref, sa_ref, sb_ref, z_ref, xn_ref):
    j = pl.program_id(1)

    @pl.when(j == 0)
    def _():
        x = x_ref[...]
        ms = jnp.mean(x * x, axis=-1, keepdims=True)
        xn_ref[...] = (x * lax.rsqrt(ms + NORM_EPS) * nw_ref[...]).astype(BF16)

    z = jnp.dot(xn_ref[...], w_ref[...], preferred_element_type=F32)

    @pl.when(j < 2)
    def _():
        z_ref[...] = z

    @pl.when(j == 2)
    def _():
        c, sa, sb = cos_ref[...], sa_ref[...], sb_ref[...]
        half = ROT_DIM // 2
        for o in (0, GROUP_W):
            t = z[:, o:o + GROUP_W]
            z_ref[:, o:o + GROUP_W] = (t * c + pltpu.roll(t, GROUP_W - half, 1) * sa
                                       + pltpu.roll(t, half, 1) * sb)
        z_ref[:, 2 * GROUP_W:] = z[:, 2 * GROUP_W:]


def _inproj(x, nw, w, rope_tabs):
    n, d = x.shape
    tm = _row_tile(n, 512)
    tab_tiles = rope_tabs[0].shape[0] // tm
    tab_spec = pl.BlockSpec((tm, GROUP_W), lambda i, j: (i % tab_tiles, 0))
    return pl.pallas_call(
        _inproj_kernel,
        grid=(n // tm, Z_W // Z_TN),
        in_specs=[pl.BlockSpec((tm, d), lambda i, j: (i, 0)),
                  pl.BlockSpec((1, d), lambda i, j: (0, 0)),
                  pl.BlockSpec((d, Z_TN), lambda i, j: (0, j)),
                  tab_spec, tab_spec, tab_spec],
        out_specs=pl.BlockSpec((tm, Z_TN), lambda i, j: (i, j)),
        out_shape=jax.ShapeDtypeStruct((n, Z_W), F32),
        scratch_shapes=[pltpu.VMEM((tm, d), BF16)],
        compiler_params=_params("parallel", "arbitrary"),
        name="inproj",
    )(x, nw, w, *rope_tabs)


def _rope(t, c, sa, sb):
    half = ROT_DIM // 2
    return t * c + pltpu.roll(t, GROUP_W - half, 1) * sa + pltpu.roll(t, half, 1) * sb


def _inproj_prompt_kernel(x_ref, nw_ref, w_ref, cos_ref, sa_ref, sb_ref, kprev_hbm, vprev_hbm,
                          zm_ref, q_ref, g_ref, gt_ref, kb_ref, vt_ref, km_ref, knew_ref, vnew_ref,
                          *, tiles_per_seq):
    del kprev_hbm, vprev_hbm
    i = pl.program_id(0)
    tm = x_ref.shape[0]
    g = GROUP_W

    ti = i % tiles_per_seq

    @pl.when(ti == 0)
    def _():
        km_ref[...] = jnp.zeros(km_ref.shape, F32)

    x = x_ref[...]
    ms = jnp.mean(x * x, axis=-1, keepdims=True)
    xn = (x * lax.rsqrt(ms + NORM_EPS) * nw_ref[...]).astype(BF16)
    for jj in range(2):
        zm_ref[:, jj * Z_TN:(jj + 1) * Z_TN] = jnp.dot(xn, w_ref[:, jj * Z_TN:(jj + 1) * Z_TN],
                                                       preferred_element_type=F32)
    z = jnp.dot(xn, w_ref[:, 2 * Z_TN:], preferred_element_type=F32)

    c, sa, sb = cos_ref[...], sa_ref[...], sb_ref[...]
    q_ref[...] = _rope(z[:, 0:g], c, sa, sb)
    k = _rope(z[:, g:2 * g], c, sa, sb)
    v = z[:, 2 * g:3 * g]
    gates = z[:, 3 * g:]
    g_ref[...] = gates
    gt_ref[...] = gates.T[0:2 * N_HEADS, :]
    nblk = tm // MOBA_BLOCK
    kb_ref[...] = k.astype(BF16).reshape(nblk, MOBA_BLOCK, g)
    knew_ref[...] = k.T
    v_t = v.T
    vnew_ref[...] = v_t
    km = km_ref[...]
    row = lax.broadcasted_iota(I32, km.shape, 0)
    for b in range(nblk):
        cols = slice(b * MOBA_BLOCK, (b + 1) * MOBA_BLOCK)
        vt_ref[b] = v_t[:, cols].astype(BF16)
        mean = jnp.sum(k[cols, :], axis=0, keepdims=True) * (1.0 / MOBA_BLOCK)
        km = jnp.where(row == ti * nblk + b, mean, km)
    km_ref[...] = km


def _inproj_prompt(x, nw, w, rope_tabs, batch, seq, layer, depth, knew, vnew):
    n, d = x.shape
    tm = 2 * MOBA_BLOCK
    assert seq % tm == 0
    tps = seq // tm
    nb = seq // MOBA_BLOCK
    nblk = tm // MOBA_BLOCK
    g = GROUP_W
    tab_spec = pl.BlockSpec((tm, g), lambda i: (i % tps, 0))
    slab = pl.BlockSpec((None, None, g, tm), lambda i: (layer, i // tps, 0, i % tps))
    blocks = pl.BlockSpec((nblk, MOBA_BLOCK, g), lambda i: (i, 0, 0))
    rows = lambda c: pl.BlockSpec((tm, c), lambda i: (i, 0))
    sds = jax.ShapeDtypeStruct
    slab_shape = sds((depth, batch, g, seq), F32)
    any_spec = pl.BlockSpec(memory_space=pl.ANY)
    return pl.pallas_call(
        functools.partial(_inproj_prompt_kernel, tiles_per_seq=tps),
        grid=(n // tm,),
        in_specs=[rows(d),
                  pl.BlockSpec((1, d), lambda i: (0, 0)),
                  pl.BlockSpec((d, Z_W), lambda i: (0, 0)),
                  tab_spec, tab_spec, tab_spec, any_spec, any_spec],
        out_specs=[rows(2 * Z_TN), rows(g), rows(GATE_W),
                   pl.BlockSpec((2 * N_HEADS, tm), lambda i: (0, i)),
                   blocks, blocks,
                   pl.BlockSpec((None, nb, g), lambda i: (i // tps, 0, 0)),
                   slab, slab],
        out_shape=[sds((n, 2 * Z_TN), F32), sds((n, g), F32), sds((n, GATE_W), F32),
                   sds((2 * N_HEADS, n), F32),
                   sds((batch * nb, MOBA_BLOCK, g), BF16), sds((batch * nb, g, MOBA_BLOCK), BF16),
                   sds((batch, nb, g), F32), slab_shape, slab_shape],
        input_output_aliases={6: 7, 7: 8},
        compiler_params=_params("arbitrary"),
        name="inproj_prompt",
    )(x, nw, w, *rope_tabs, knew, vnew)


def _rope_tables(pos):
    half = ROT_DIM // 2
    inv = 1.0 / (ROPE_THETA ** (jnp.arange(half, dtype=F32) / half))
    ang = pos.astype(F32)[:, None] * inv[None, :]
    cos, sin = jnp.cos(ang), jnp.sin(ang)
    t = pos.shape[0]
    rest = HEAD_DIM - ROT_DIM
    one = jnp.ones((t, rest), F32)
    zero = jnp.zeros((t, rest), F32)
    zh = jnp.zeros((t, half), F32)
    c = jnp.concatenate([cos, cos, one], axis=1)
    sa = jnp.concatenate([-sin, zh, zero], axis=1)
    sb = jnp.concatenate([zh, sin, zero], axis=1)
    tile = lambda a: jnp.tile(a, (1, N_HEADS))
    return tile(c), tile(sa), tile(sb)


def _window_of_lane(shape, dim, lo):
    g = (lax.broadcasted_iota(I32, shape, dim) + lo) // POOL_GROUP
    w = jnp.full(shape, POOL_WINDOWS[-1], I32)
    for i in range(len(POOL_WINDOWS) - 2, -1, -1):
        w = jnp.where(g == i, POOL_WINDOWS[i], w)
    return w


def _poolsgu_kernel(p_ref, su_ref, sv_ref, pw_ref, ps_ref, sw_ref, sb_ref, o_ref, ebuf):
    t = pl.program_id(1)
    tt = p_ref.shape[0]

    @pl.when(t == 0)
    def _():
        ebuf[0:HALO, :] = jnp.zeros((HALO, GROUP_W), F32)

    @pl.when(t > 0)
    def _():
        ebuf[0:HALO, :] = ebuf[tt:tt + HALO, :]

    ebuf[HALO:HALO + tt, :] = p_ref[...]

    pos1 = lax.broadcasted_iota(I32, (tt, LANE), 0) + t * tt + 1
    halves = []
    for half in range(2):
        lo = half * LANE
        wsmall, wbig = POOL_WINDOWS[2 * half], POOL_WINDOWS[2 * half + 1]
        e0 = ebuf[HALO:HALO + tt, lo:lo + LANE]
        acc = e0
        for off in range(1, wsmall):
            acc = acc + ebuf[HALO - off:HALO - off + tt, lo:lo + LANE]
        small = acc
        for off in range(wsmall, wbig):
            acc = acc + ebuf[HALO - off:HALO - off + tt, lo:lo + LANE]
        first = lax.broadcasted_iota(I32, (tt, LANE), 1) < POOL_GROUP
        wsum = jnp.where(first, small, acc)
        cnt = jnp.minimum(pos1, _window_of_lane((tt, LANE), 1, lo)).astype(F32)
        halves.append(wsum / cnt - e0)
    d = jnp.concatenate(halves, axis=1).astype(BF16)
    y_pool = jnp.dot(d, pw_ref[...], preferred_element_type=F32) * ps_ref[...]
    o_ref[:, 0:GROUP_W] = y_pool.astype(o_ref.dtype)

    row = lax.broadcasted_iota(I32, (SGU_CHUNK, SGU_CHUNK), 0)
    col = lax.broadcasted_iota(I32, (SGU_CHUNK, SGU_CHUNK), 1)
    ws = [jnp.where(row >= col, sw_ref[h], 0.0).astype(BF16) for h in range(N_HEADS)]
    lh = _lane_head((SGU_CHUNK, GROUP_W), 1)
    for c in range(tt // SGU_CHUNK):
        rows = slice(c * SGU_CHUNK, (c + 1) * SGU_CHUNK)
        v = sv_ref[rows, :].astype(BF16)
        mixed = _by_head([jnp.dot(ws[h], v, preferred_element_type=F32) for h in range(N_HEADS)], lh)
        y = su_ref[rows, :] * (mixed + sb_ref[...])
        o_ref[rows, GROUP_W:2 * GROUP_W] = y.astype(o_ref.dtype)


def _poolsgu(z, batch, seq, pool_wbd, pool_scale, sgu_w, sgu_bias):
    tt = _row_tile(seq, 512)
    tps = seq // tt
    zspec = lambda c: pl.BlockSpec((tt, GROUP_W), lambda b, t, c=c: (b * tps + t, c))
    return pl.pallas_call(
        _poolsgu_kernel,
        grid=(batch, tps),
        in_specs=[zspec(COL_P), zspec(COL_SU), zspec(COL_SV),
                  pl.BlockSpec((GROUP_W, GROUP_W), lambda b, t: (0, 0)),
                  pl.BlockSpec((1, GROUP_W), lambda b, t: (0, 0)),
                  pl.BlockSpec((N_HEADS, SGU_CHUNK, SGU_CHUNK), lambda b, t: (0, 0, 0)),
                  pl.BlockSpec((SGU_CHUNK, GROUP_W), lambda b, t: (0, 0))],
        out_specs=pl.BlockSpec((tt, 2 * GROUP_W), lambda b, t: (b * tps + t, 0)),
        out_shape=jax.ShapeDtypeStruct((batch * seq, 2 * GROUP_W), BF16),
        scratch_shapes=[pltpu.VMEM((HALO + tt, GROUP_W), F32)],
        compiler_params=_params("parallel", "arbitrary"),
        name="pool_sgu",
    )(z, z, z, pool_wbd, pool_scale, sgu_w, sgu_bias)


def _mlstm_kernel(q_ref, k_ref, v_ref, o_ref, g_ref, *rest):
    L = MLSTM_CHUNK
    batch = q_ref.shape[0]
    gt_refs = rest[:batch]
    gbc_ref, gbr_ref, nw_ref, y_ref, ct_ref, n_ref, m_ref = rest[batch:]

    @pl.when(pl.program_id(0) == 0)
    def _():
        ct_ref[...] = jnp.zeros(ct_ref.shape, F32)
        n_ref[...] = jnp.zeros(n_ref.shape, F32)
        m_ref[...] = jnp.zeros(m_ref.shape, F32)

    r_i = lax.broadcasted_iota(I32, (L, L), 0)
    c_i = lax.broadcasted_iota(I32, (L, L), 1)
    causal = r_i >= c_i
    tri_l = causal.astype(F32)
    tri_u = (r_i <= c_i).astype(F32)
    lh = _lane_head((L, GROUP_W), 1)
    bd = _lane_head((GROUP_W, GROUP_W), 0) == _lane_head((GROUP_W, GROUP_W), 1)
    same_head = bd.astype(F32)

    ct_b = [ct_ref[b] for b in range(batch)]
    n_b = [n_ref[b] for b in range(batch)]
    m_b = [[m_ref[b, h:h + 1, 0:1] for h in range(N_HEADS)] for b in range(batch)]

    def chunk(c, b):
        ct, nrow, m_heads = ct_b[b], n_b[b], m_b[b]
        rows = slice(c * L, (c + 1) * L)
        q = q_ref[b, rows, :]
        ks = k_ref[b, rows, :] * (HEAD_DIM ** -0.5)
        v = v_ref[b, rows, :]
        g = g_ref[b, rows, :] + gbc_ref[...]
        gt = gt_refs[b][:, rows] + gbr_ref[...]
        bcol = jnp.dot(tri_l, _logsig(g), precision=HIGHEST, preferred_element_type=F32)
        brow = jnp.dot(_logsig(gt), tri_u, precision=HIGHEST, preferred_element_type=F32)
        qb, kb, vb = q.astype(BF16), ks.astype(BF16), v.astype(BF16)

        s_l, inter_l, deni_l, emt_l, wsrc_l, dec_l, mnew_l = [], [], [], [], [], [], []
        for h in range(N_HEADS):
            bc = bcol[:, N_HEADS + h:N_HEADS + h + 1]
            ic = g[:, h:h + 1]
            br = brow[N_HEADS + h:N_HEADS + h + 1, :]
            ir = gt[h:h + 1, :]
            dm = jnp.where(causal, bc - br + ir, NEG)
            a = bc + m_heads[h]
            mt = jnp.maximum(a, jnp.max(dm, axis=1, keepdims=True))
            qm = jnp.where(lh == h, q, 0.0).astype(BF16)
            s = _dot_nt(qm, kb) * jnp.exp(dm - mt)
            m_new = mt[L - 1:L, :]
            s_l.append(s)
            inter_l.append(jnp.exp(a - mt))
            deni_l.append(jnp.sum(s, axis=1, keepdims=True))
            emt_l.append(jnp.exp(-mt))
            wsrc_l.append(jnp.exp(bc[L - 1:L, :] - bc + ic - m_new))
            dec_l.append(jnp.exp(a[L - 1:L, :] - m_new))
            mnew_l.append(m_new)

        r = jnp.dot(jnp.concatenate(s_l, axis=0).astype(BF16), vb, preferred_element_type=F32)
        num_i = _by_head([r[h * L:(h + 1) * L, :] for h in range(N_HEADS)], lh)
        q_c = jnp.dot(qb, ct.astype(BF16), preferred_element_type=F32)
        q_n = jnp.dot(q * nrow, same_head, precision=HIGHEST, preferred_element_type=F32)
        inter = _by_head(inter_l, lh)
        num = num_i + inter * q_c
        den = _by_head(deni_l, lh) + inter * q_n
        hval = num / jnp.maximum(jnp.abs(den), _by_head(emt_l, lh))
        ms = jnp.dot(hval * hval, same_head, precision=HIGHEST,
                     preferred_element_type=F32) * (1.0 / HEAD_DIM)
        y = hval * lax.rsqrt(ms + NORM_EPS) * nw_ref[...]
        y_ref[b, rows, :] = (jax.nn.sigmoid(o_ref[b, rows, :]) * y).astype(y_ref.dtype)

        kw = ks * _by_head(wsrc_l, lh)
        dec = _by_head(dec_l, lh[0:1, :])
        upd = jnp.dot(kw.T.astype(BF16), vb, preferred_element_type=F32)
        ct_b[b] = ct * dec + jnp.where(bd, upd, 0.0)
        n_b[b] = nrow * dec + jnp.sum(kw, axis=0, keepdims=True)
        m_b[b] = mnew_l

    for c in range(q_ref.shape[1] // L):
        for b in range(batch):
            chunk(c, b)

    for b in range(batch):
        ct_ref[b] = ct_b[b]
        n_ref[b] = n_b[b]
        for h in range(N_HEADS):
            m_ref[b, h:h + 1, :] = jnp.broadcast_to(m_b[b][h], (1, m_ref.shape[2]))


def _mlstm(z, gates, gates_t, batch, seq, gate_b, norm_w):
    rows = _row_tile(seq, 2 * MLSTM_CHUNK)
    tps = seq // rows
    z3 = z.reshape(batch, seq, z.shape[1])
    zspec = lambda c: pl.BlockSpec((batch, rows, GROUP_W), lambda t, c=c: (0, t, c))
    gbc = jnp.zeros((1, GATE_W), F32).at[0, :2 * N_HEADS].set(gate_b.reshape(-1))
    gbr = gate_b.reshape(2 * N_HEADS, 1)
    state = lambda r, c: (pl.BlockSpec((batch, r, c), lambda t: (0, 0, 0)),
                          jax.ShapeDtypeStruct((batch, r, c), F32))
    (cs, csh), (ns, nsh), (msp, msh) = state(GROUP_W, GROUP_W), state(1, GROUP_W), state(8, LANE)
    gt_specs = [pl.BlockSpec((2 * N_HEADS, rows), lambda t, b=b: (0, b * tps + t)) for b in range(batch)]
    y, ct, nst, mst = pl.pallas_call(
        _mlstm_kernel,
        grid=(tps,),
        in_specs=[zspec(COL_MQ), zspec(COL_MK), zspec(COL_MV), zspec(COL_MO),
                  pl.BlockSpec((batch, rows, GATE_W), lambda t: (0, t, 0))] + gt_specs + [
                  pl.BlockSpec((1, GATE_W), lambda t: (0, 0)),
                  pl.BlockSpec((2 * N_HEADS, 1), lambda t: (0, 0)),
                  pl.BlockSpec((1, GROUP_W), lambda t: (0, 0))],
        out_specs=[pl.BlockSpec((batch, rows, GROUP_W), lambda t: (0, t, 0)), cs, ns, msp],
        out_shape=[jax.ShapeDtypeStruct((batch, seq, GROUP_W), BF16), csh, nsh, msh],
        compiler_params=_params("arbitrary"),
        name="mlstm",
    )(z3, z3, z3, z3, gates.reshape(batch, seq, GATE_W), *([gates_t] * batch), gbc, gbr, norm_w)
    return y.reshape(batch * seq, GROUP_W), ct, nst, mst


def _pick_top_blocks(s, n_valid_mask):
    blk = lax.broadcasted_iota(I32, s.shape, 0)
    s = jnp.where(n_valid_mask, s, NEG)
    bias = jnp.full(s.shape, NEG, F32)
    for _ in range(MOBA_TOPK):
        mx = jnp.max(s, axis=0, keepdims=True)
        idx = jnp.min(jnp.where(s == mx, blk, s.shape[0]), axis=0, keepdims=True)
        pick = blk == idx
        bias = jnp.where(jnp.logical_and(pick, mx > 0.5 * NEG), 0.0, bias)
        s = jnp.where(pick, NEG_PICKED, s)
    return bias


MOBA_KEY_CHUNK = 64
LOG2E = 1.4426950408889634


def _moba_logits(k_blk, qbd_sc, lt_sc, buf):
    lt_sc[buf] = jnp.dot(k_blk[...], qbd_sc[...], preferred_element_type=F32)


def _moba_softmax_pv(h, buf, vt_blk, lt_sc, p_sc, m_sc, l_sc, acc_sc, bias, key_le_query):
    B = MOBA_BLOCK
    ch = MOBA_KEY_CHUNK
    groups = ch // 8
    first = bias is None
    cols = slice(h * B, (h + 1) * B)
    mx = None
    for c in range(B // ch):
        rows = slice(c * ch, (c + 1) * ch)
        s = lt_sc[buf, rows, cols]
        if first:
            s = jnp.where(key_le_query[rows, :], s, NEG)
            lt_sc[buf, rows, cols] = s
        part = jnp.max(s.reshape(groups, 8, B), axis=0)
        mx = part if mx is None else jnp.maximum(mx, part)
    cand = jnp.max(mx, axis=0, keepdims=True)
    if first:
        m_new = shift = cand
    else:
        m_old = m_sc[h:h + 1, :]
        m_new = jnp.maximum(m_old, cand + bias)
        alpha = jnp.exp2(m_old - m_new)
        shift = m_new - bias
    lsum = None
    for c in range(B // ch):
        rows = slice(c * ch, (c + 1) * ch)
        p = jnp.exp2(lt_sc[buf, rows, cols] - shift)
        part = jnp.sum(p.reshape(groups, 8, B), axis=0)
        lsum = part if lsum is None else lsum + part
        p_sc[h, rows, :] = p.astype(BF16)
    lnew = jnp.sum(lsum, axis=0, keepdims=True)
    hs = slice(h * HEAD_DIM, (h + 1) * HEAD_DIM)
    pv = jnp.dot(vt_blk[hs, :], p_sc[h], preferred_element_type=F32)
    if first:
        l_sc[h:h + 1, :] = lnew
        acc_sc[hs, :] = pv
    else:
        l_sc[h:h + 1, :] = alpha * l_sc[h:h + 1, :] + lnew
        acc_sc[hs, :] = alpha * acc_sc[hs, :] + pv
    m_sc[h:h + 1, :] = m_new


def _moba_kernel(q_ref, kb_ref, vt_ref, km_ref, o_ref, qbd_sc, sel_sc, m_sc, l_sc, acc_sc, lt_sc, p_sc):
    i = pl.program_id(1)
    nb = km_ref.shape[0]
    nbp = sel_sc.shape[0] // N_HEADS
    B = MOBA_BLOCK
    q = q_ref[...]
    km = km_ref[...]
    lh = _lane_head((B, GROUP_W), 1)
    past = lax.broadcasted_iota(I32, (nb, B), 0) < i
    key_le_query = lax.broadcasted_iota(I32, (B, B), 0) <= lax.broadcasted_iota(I32, (B, B), 1)
    stats = (lt_sc, p_sc, m_sc, l_sc, acc_sc)

    q_t = q.T * (HEAD_DIM ** -0.5 * LOG2E)
    row_head = _lane_head((GROUP_W, B), 0)
    for h in range(N_HEADS):
        s_blk = _dot_nt(km, jnp.where(lh == h, q, 0.0), precision=HIGHEST)
        sel_sc[h * nbp:h * nbp + nb, :] = _pick_top_blocks(s_blk, past)
        qbd_sc[:, h * B:(h + 1) * B] = jnp.where(row_head == h, q_t, 0.0).astype(BF16)

    _moba_logits(kb_ref.at[i], qbd_sc, lt_sc, 1)
    _moba_logits(kb_ref.at[0], qbd_sc, lt_sc, 0)
    for h in range(N_HEADS):
        _moba_softmax_pv(h, 1, vt_ref.at[i], *stats, None, key_le_query)

    def step(n, buf):
        _moba_logits(kb_ref.at[jnp.minimum(n + 1, nb - 1)], qbd_sc, lt_sc, 1 - buf)
        for h in range(N_HEADS):
            bias = sel_sc[pl.ds(h * nbp + n, 1), :]
            _moba_softmax_pv(h, buf, vt_ref.at[n], *stats, bias, None)

    def body(k, carry):
        step(2 * k, 0)
        step(2 * k + 1, 1)
        return carry

    lax.fori_loop(0, (i + 1) // 2, body, 0)

    for h in range(N_HEADS):
        hs = slice(h * HEAD_DIM, (h + 1) * HEAD_DIM)
        acc_sc[hs, :] = acc_sc[hs, :] / l_sc[h:h + 1, :]
    o_ref[...] = acc_sc[...].T.astype(o_ref.dtype)


def _moba(q, kb4, vt4, kmean, batch, nb):
    nbp = -(-nb // 8) * 8
    B = MOBA_BLOCK
    blk4 = pl.BlockSpec((None, nb, B, GROUP_W), lambda b, i: (b, 0, 0, 0))
    return pl.pallas_call(
        _moba_kernel,
        grid=(batch, nb),
        in_specs=[pl.BlockSpec((B, GROUP_W), lambda b, i: (b * nb + i, 0)),
                  blk4, blk4,
                  pl.BlockSpec((None, nb, GROUP_W), lambda b, i: (b, 0, 0))],
        out_specs=pl.BlockSpec((B, GROUP_W), lambda b, i: (b * nb + i, 0)),
        out_shape=jax.ShapeDtypeStruct((batch * nb * B, GROUP_W), BF16),
        scratch_shapes=[pltpu.VMEM((GROUP_W, N_HEADS * B), BF16),
                        pltpu.VMEM((N_HEADS * nbp, B), F32),
                        pltpu.VMEM((8, B), F32), pltpu.VMEM((8, B), F32),
                        pltpu.VMEM((GROUP_W, B), F32),
                        pltpu.VMEM((2, B, N_HEADS * B), F32),
                        pltpu.VMEM((N_HEADS, B, B), BF16)],
        compiler_params=_params("parallel", "arbitrary"),
        name="moba",
    )(q, kb4, vt4, kmean)


def _outproj_kernel(x_ref, ma_ref, mc_ref, md_ref, w_ref, o_ref):
    g2 = 2 * GROUP_W
    acc = jnp.dot(ma_ref[...], w_ref[0:g2, :], preferred_element_type=F32)
    acc += jnp.dot(mc_ref[...], w_ref[g2:g2 + GROUP_W, :], preferred_element_type=F32)
    acc += jnp.dot(md_ref[...], w_ref[g2 + GROUP_W:, :], preferred_element_type=F32)
    o_ref[...] = x_ref[...] + acc


def _outproj(x, ma, mc, md, w):
    n, d = x.shape
    tm = _row_tile(n, 512)
    rows = lambda c: pl.BlockSpec((tm, c), lambda i: (i, 0))
    return pl.pallas_call(
        _outproj_kernel,
        grid=(n // tm,),
        in_specs=[rows(d), rows(2 * GROUP_W), rows(GROUP_W), rows(GROUP_W),
                  pl.BlockSpec(w.shape, lambda i: (0, 0))],
        out_specs=rows(d),
        out_shape=jax.ShapeDtypeStruct((n, d), F32),
        compiler_params=_params("parallel"),
        name="outproj",
    )(x, ma, mc, md, w)


def _ff_tile(d_ff):
    return d_ff // 2 if (d_ff // 2) % LANE == 0 else d_ff


def _swiglu_partial(xn, w1_ref, w3_ref, w2_ref):
    a = jnp.dot(xn, w1_ref[...], preferred_element_type=F32)
    b = jnp.dot(xn, w3_ref[...], preferred_element_type=F32)
    hmid = (a * jax.nn.sigmoid(a) * b).astype(BF16)
    return jnp.dot(hmid, w2_ref[...], preferred_element_type=F32)


def _ffn_kernel(x_ref, nw_ref, w1_ref, w3_ref, w2_ref, o_ref, xn_ref, acc_ref):
    j = pl.program_id(1)

    @pl.when(j == 0)
    def _():
        x = x_ref[...]
        ms = jnp.mean(x * x, axis=-1, keepdims=True)
        xn_ref[...] = (x * lax.rsqrt(ms + NORM_EPS) * nw_ref[...]).astype(BF16)
        acc_ref[...] = jnp.zeros(acc_ref.shape, F32)

    acc_ref[...] += _swiglu_partial(xn_ref[...], w1_ref, w3_ref, w2_ref)

    @pl.when(j == pl.num_programs(1) - 1)
    def _():
        o_ref[...] = x_ref[...] + acc_ref[...]


def _ffn(x, nw, w1, w3, w2):
    n, d = x.shape
    d_ff = w1.shape[1]
    tm = _row_tile(n, 512)
    tf = _ff_tile(d_ff)
    return pl.pallas_call(
        _ffn_kernel,
        grid=(n // tm, d_ff // tf),
        in_specs=[pl.BlockSpec((tm, d), lambda i, j: (i, 0)),
                  pl.BlockSpec((1, d), lambda i, j: (0, 0)),
                  pl.BlockSpec((d, tf), lambda i, j: (0, j)),
                  pl.BlockSpec((d, tf), lambda i, j: (0, j)),
                  pl.BlockSpec((tf, d), lambda i, j: (j, 0))],
        out_specs=pl.BlockSpec((tm, d), lambda i, j: (i, 0)),
        out_shape=jax.ShapeDtypeStruct((n, d), F32),
        scratch_shapes=[pltpu.VMEM((tm, d), BF16), pltpu.VMEM((tm, d), F32)],
        compiler_params=_params("parallel", "arbitrary"),
        name="ffn_dense",
    )(x, nw, w1, w3, w2)


def _router_kernel(x_ref, nw_ref, rw_ref, rb_ref, xp_ref, g_ref, e_ref):
    x = x_ref[...]
    ms = jnp.mean(x * x, axis=-1, keepdims=True)
    xn = x * lax.rsqrt(ms + NORM_EPS) * nw_ref[...]
    tm, d = x.shape
    half = d // 2
    bits = pltpu.bitcast(xn.astype(BF16).astype(F32), jnp.uint32)
    packed = jnp.bitwise_or(jnp.right_shift(bits[:, :half], jnp.uint32(16)), bits[:, half:])
    nc = half // LANE
    for c in range(nc):
        xp_ref[pl.ds(c, tm, stride=nc), :] = packed[:, c * LANE:(c + 1) * LANE]

    logits = jnp.dot(xn, rw_ref[...], precision=HIGHEST, preferred_element_type=F32) + rb_ref[...]
    lane = lax.broadcasted_iota(I32, logits.shape, 1)
    mx1 = jnp.max(logits, axis=1, keepdims=True)
    i1 = jnp.min(jnp.where(logits == mx1, lane, LANE), axis=1, keepdims=True)
    rest = jnp.where(lane == i1, -jnp.inf, logits)
    mx2 = jnp.max(rest, axis=1, keepdims=True)
    i2 = jnp.min(jnp.where(rest == mx2, lane, LANE), axis=1, keepdims=True)
    e2 = jnp.exp(mx2 - mx1)
    g1 = 1.0 / (1.0 + e2)
    g_ref[...] = jnp.where(lane == 0, g1, jnp.where(lane == 1, e2 * g1, 0.0))
    e_ref[...] = jnp.where(lane == 0, i1, jnp.where(lane == 1, i2, 0))


def _router(x, nw, rw, rb):
    n, d = x.shape
    tm = _row_tile(n, 512)
    rows = lambda c: pl.BlockSpec((tm, c), lambda i: (i, 0))
    full = lambda a: pl.BlockSpec(a.shape, lambda i: (0, 0))
    return pl.pallas_call(
        _router_kernel,
        grid=(n // tm,),
        in_specs=[rows(d), full(nw), full(rw), full(rb)],
        out_specs=[pl.BlockSpec((tm * (d // 2 // LANE), LANE), lambda i: (i, 0)), rows(LANE), rows(LANE)],
        out_shape=[jax.ShapeDtypeStruct((n * (d // 2 // LANE), LANE), jnp.uint32),
                   jax.ShapeDtypeStruct((n, LANE), F32),
                   jax.ShapeDtypeStruct((n, LANE), I32)],
        compiler_params=_params("parallel"),
        name="router",
    )(x, nw, rw, rb)


def _unpack_rows(words):
    lo = pltpu.bitcast(jnp.left_shift(words, jnp.uint32(16)), F32)
    hi = pltpu.bitcast(jnp.bitwise_and(words, jnp.uint32(0xFFFF0000)), F32)
    return jnp.concatenate([lo, hi], axis=1).astype(BF16)


def _gmm_kernel(te_ref, src_ref, nused_ref, xp_hbm, w1_ref, w3_ref, w2_ref, o_ref,
                xbuf, sem, xn_ref, acc_ref):
    i = pl.program_id(0)
    j = pl.program_id(1)
    ntiles = pl.num_programs(0)
    tm, d = acc_ref.shape
    nc = xbuf.shape[1] // tm
    slot = i % 2

    def issue(tile, sl):
        def body(g, carry):
            for u in range(ISSUE_UNROLL):
                r = g * ISSUE_UNROLL + u
                tok = src_ref[tile * tm + r]
                pltpu.make_async_copy(xp_hbm.at[pl.ds(pl.multiple_of(tok * nc, nc), nc), :],
                                      xbuf.at[sl, pl.ds(pl.multiple_of(r * nc, nc), nc), :],
                                      sem.at[sl]).start(priority=u % 2)
            return carry
        lax.fori_loop(0, tm // ISSUE_UNROLL, body, 0)

    @pl.when(j == 0)
    def _():
        @pl.when(i == 0)
        def _():
            issue(0, 0)

        @pl.when(i + 1 < ntiles)
        def _():
            issue(i + 1, 1 - slot)

        pltpu.make_async_copy(xbuf.at[slot], xbuf.at[slot], sem.at[slot]).wait()
        words = jnp.concatenate([xbuf[slot, pl.ds(c, tm, stride=nc), :] for c in range(nc)], axis=1)
        xn_ref[...] = _unpack_rows(words)
        acc_ref[...] = jnp.zeros(acc_ref.shape, F32)

    @pl.when(i < nused_ref[0])
    def _():
        acc_ref[...] += _swiglu_partial(xn_ref[...], w1_ref, w3_ref, w2_ref)

    @pl.when(j == pl.num_programs(1) - 1)
    def _():
        for c in range(d // LANE):
            o_ref[pl.ds(c, tm, stride=d // LANE), :] = acc_ref[:, c * LANE:(c + 1) * LANE]


def _gmm(xp, tile_expert, src, nused, w1, w3, w2, tm):
    ncap = src.shape[0]
    d = w1.shape[1]
    half = d // 2
    d_ff = w1.shape[2]
    tf = _ff_tile(d_ff)
    oc = d // LANE
    grid_spec = pltpu.PrefetchScalarGridSpec(
        num_scalar_prefetch=3,
        grid=(ncap // tm, d_ff // tf),
        in_specs=[pl.BlockSpec(memory_space=pl.ANY),
                  pl.BlockSpec((None, d, tf), lambda i, j, te, s, nu: (te[i], 0, j)),
                  pl.BlockSpec((None, d, tf), lambda i, j, te, s, nu: (te[i], 0, j)),
                  pl.BlockSpec((None, tf, d), lambda i, j, te, s, nu: (te[i], j, 0))],
        out_specs=pl.BlockSpec((tm * oc, LANE), lambda i, j, te, s, nu: (i, 0)),
        scratch_shapes=[pltpu.VMEM((2, tm * (half // LANE), LANE), jnp.uint32),
                        pltpu.SemaphoreType.DMA((2,)),
                        pltpu.VMEM((tm, d), BF16),
                        pltpu.VMEM((tm, d), F32)])
    return pl.pallas_call(
        _gmm_kernel,
        grid_spec=grid_spec,
        out_shape=jax.ShapeDtypeStruct((ncap * oc, LANE), F32),
        compiler_params=_params("arbitrary", "arbitrary"),
        name="moe_gmm",
    )(tile_expert, src, nused, xp, w1, w3, w2)


def _dispatch_plan(e_idx, n_experts, tm):
    n = e_idx.shape[0]
    ef = e_idx[:, :TOP_K].T.reshape(-1)
    onehot = (ef[:, None] == jnp.arange(n_experts, dtype=I32)[None, :]).astype(I32)
    rank = jnp.sum((jnp.cumsum(onehot, axis=0) - 1) * onehot, axis=1)
    counts = jnp.sum(onehot, axis=0)
    padded = ((counts + tm - 1) // tm) * tm
    ends = jnp.cumsum(padded)
    pos = (ends - padded)[ef] + rank
    ncap = TOP_K * n + n_experts * tm
    src = jnp.zeros((ncap,), I32).at[pos].set(jnp.arange(TOP_K * n, dtype=I32) % n,
                                              unique_indices=True, mode="promise_in_bounds")
    tile_start = jnp.arange(ncap // tm, dtype=I32) * tm
    tile_expert = jnp.minimum(jnp.searchsorted(ends, tile_start, side="right"), n_experts - 1).astype(I32)
    nused = (ends[-1] // tm).astype(I32).reshape(1)
    return pos.astype(I32), src, tile_expert, nused


def _combine_kernel(pos_ref, h_ref, g_ref, o_hbm, fw_ref, y_ref, obuf, sem, *, final_norm):
    i = pl.program_id(0)
    ntiles = pl.num_programs(0)
    tc, d = h_ref.shape
    oc = d // LANE
    n = ntiles * tc
    slot = i % 2

    def issue(tile, sl):
        def body(g, carry):
            for u in range(ISSUE_UNROLL):
                r = g * ISSUE_UNROLL + u
                for k in range(TOP_K):
                    p = pos_ref[k * n + tile * tc + r]
                    pltpu.make_async_copy(o_hbm.at[pl.ds(pl.multiple_of(p * oc, oc), oc), :],
                                          obuf.at[sl, k, pl.ds(pl.multiple_of(r * oc, oc), oc), :],
                                          sem.at[sl]).start(priority=k)
            return carry
        lax.fori_loop(0, tc // ISSUE_UNROLL, body, 0)

    @pl.when(i == 0)
    def _():
        issue(0, 0)

    @pl.when(i + 1 < ntiles)
    def _():
        issue(i + 1, 1 - slot)

    pltpu.make_async_copy(obuf.at[slot], obuf.at[slot], sem.at[slot]).wait()
    g = g_ref[...]
    rows = lambda k: jnp.concatenate([obuf[slot, k, pl.ds(c, tc, stride=oc), :] for c in range(oc)], axis=1)
    y = h_ref[...] + g[:, 0:1] * rows(0) + g[:, 1:2] * rows(1)
    if final_norm:
        ms = jnp.mean(y * y, axis=-1, keepdims=True)
        y = y * lax.rsqrt(ms + NORM_EPS) * fw_ref[...]
    y_ref[...] = y


def _combine(pos, h, gates, o_sorted, fw, final_norm):
    n, d = h.shape
    tc = _row_tile(n, 256)
    grid_spec = pltpu.PrefetchScalarGridSpec(
        num_scalar_prefetch=1,
        grid=(n // tc,),
        in_specs=[pl.BlockSpec((tc, d), lambda i, p: (i, 0)),
                  pl.BlockSpec((tc, LANE), lambda i, p: (i, 0)),
                  pl.BlockSpec(memory_space=pl.ANY),
                  pl.BlockSpec((1, d), lambda i, p: (0, 0))],
        out_specs=pl.BlockSpec((tc, d), lambda i, p: (i, 0)),
        scratch_shapes=[pltpu.VMEM((2, TOP_K, tc * (d // LANE), LANE), F32), pltpu.SemaphoreType.DMA((2,))])
    return pl.pallas_call(
        functools.partial(_combine_kernel, final_norm=final_norm),
        grid_spec=grid_spec,
        out_shape=jax.ShapeDtypeStruct((n, d), F32),
        compiler_params=_params("arbitrary"),
        name="moe_combine",
    )(pos, h, gates, o_sorted, fw)


def _moe_small_kernel(x_ref, nw_ref, g_ref, e_ref, w1_ref, w3_ref, w2_ref, fw_ref, y_ref,
                      xn_ref, acc_ref, *, final_norm):
    e = pl.program_id(0)
    j = pl.program_id(1)

    @pl.when(jnp.logical_and(e == 0, j == 0))
    def _():
        x = x_ref[...]
        ms = jnp.mean(x * x, axis=-1, keepdims=True)
        xn_ref[...] = (x * lax.rsqrt(ms + NORM_EPS) * nw_ref[...]).astype(BF16)
        acc_ref[...] = jnp.zeros(acc_ref.shape, F32)

    g = g_ref[...]
    idx = e_ref[...]
    gate = (jnp.where(idx[:, 0:1] == e, g[:, 0:1], 0.0) + jnp.where(idx[:, 1:2] == e, g[:, 1:2], 0.0))
    acc_ref[...] += gate * _swiglu_partial(xn_ref[...], w1_ref, w3_ref, w2_ref)

    @pl.when(jnp.logical_and(e == pl.num_programs(0) - 1, j == pl.num_programs(1) - 1))
    def _():
        y = x_ref[...] + acc_ref[...]
        if final_norm:
            ms = jnp.mean(y * y, axis=-1, keepdims=True)
            y = y * lax.rsqrt(ms + NORM_EPS) * fw_ref[...]
        y_ref[...] = y


def _moe_small(x, nw, gates, e_idx, w1, w3, w2, fw, final_norm):
    n, d = x.shape
    n_experts, _, d_ff = w1.shape
    tf = _ff_tile(d_ff)
    full = lambda a: pl.BlockSpec(a.shape, lambda e, j: (0, 0))
    return pl.pallas_call(
        functools.partial(_moe_small_kernel, final_norm=final_norm),
        grid=(n_experts, d_ff // tf),
        in_specs=[full(x), full(nw), full(gates), full(e_idx),
                  pl.BlockSpec((None, d, tf), lambda e, j: (e, 0, j)),
                  pl.BlockSpec((None, d, tf), lambda e, j: (e, 0, j)),
                  pl.BlockSpec((None, tf, d), lambda e, j: (e, j, 0)),
                  full(fw)],
        out_specs=full(x),
        out_shape=jax.ShapeDtypeStruct((n, d), F32),
        scratch_shapes=[pltpu.VMEM((n, d), BF16), pltpu.VMEM((n, d), F32)],
        compiler_params=_params("arbitrary", "arbitrary"),
        name="moe_small",
    )(x, nw, gates, e_idx, w1, w3, w2, fw)


def _norm_kernel(x_ref, w_ref, o_ref):
    x = x_ref[...]
    ms = jnp.mean(x * x, axis=-1, keepdims=True)
    o_ref[...] = x * lax.rsqrt(ms + NORM_EPS) * w_ref[...]


def _final_norm(x, w):
    n, d = x.shape
    tm = _row_tile(n, 512)
    return pl.pallas_call(
        _norm_kernel,
        grid=(n // tm,),
        in_specs=[pl.BlockSpec((tm, d), lambda i: (i, 0)), pl.BlockSpec((1, d), lambda i: (0, 0))],
        out_specs=pl.BlockSpec((tm, d), lambda i: (i, 0)),
        out_shape=jax.ShapeDtypeStruct((n, d), F32),
        compiler_params=_params("parallel"),
        name="final_norm",
    )(x, w)


def _smix_kernel(p_ref, pf_ref, pw_ref, ps_ref, su_ref, sv_ref, sw0_ref, sb0_ref,
                 q_ref, k_ref, v_ref, ig_ref, fg_ref, c_ref, n_ref, m_ref, mo_ref, nw_ref,
                 ypool_ref, ysgu_ref, yml_ref, cn_ref, nn_ref, mn_ref, *, cnt):
    p = p_ref[...]
    lane_w = _window_of_lane(p.shape, 1, 0)
    acc = p
    wsum = jnp.zeros(p.shape, F32)
    for off in range(1, POOL_WINDOWS[-1] + 1):
        if off in POOL_WINDOWS:
            wsum = jnp.where(lane_w == off, acc, wsum)
        if off <= POOL_BUF:
            acc = acc + pf_ref[HALO - off]
    count = jnp.minimum(lane_w, cnt).astype(F32)
    d = (wsum / count - p).astype(BF16)
    ypool_ref[...] = jnp.dot(d, pw_ref[...], preferred_element_type=F32) * ps_ref[...]

    ysgu_ref[...] = su_ref[...] * (sw0_ref[...] * sv_ref[...] + sb0_ref[...])

    q = q_ref[...]
    ks = k_ref[...] * (HEAD_DIM ** -0.5)
    v = v_ref[...]
    ig = ig_ref[...]
    c = c_ref[...]
    nrow = n_ref[...]
    a = _logsig(fg_ref[...]) + m_ref[...]
    mt = jnp.maximum(a, ig)
    inter = jnp.exp(a - mt)
    e_i = jnp.exp(ig - mt)
    s = jnp.sum(q * ks, axis=-1, keepdims=True) * e_i
    c_q = jnp.sum(c * q, axis=-1, keepdims=True)
    n_q = jnp.sum(nrow * q, axis=-1, keepdims=True)
    num = s * v + inter * c_q
    den = s + inter * n_q
    hval = num / jnp.maximum(jnp.abs(den), jnp.exp(-mt))
    cn_ref[...] = inter * c + (e_i * v) * ks
    nn_ref[...] = inter * nrow + e_i * ks
    mn_ref[...] = mt
    ms = jnp.mean(hval * hval, axis=1, keepdims=True)
    y = hval * lax.rsqrt(ms + NORM_EPS) * nw_ref[...]
    yml_ref[...] = jax.nn.sigmoid(mo_ref[...]) * y


def _smix(args, out_shapes, cnt):
    return pl.pallas_call(
        functools.partial(_smix_kernel, cnt=cnt),
        out_shape=out_shapes,
        compiler_params=pltpu.CompilerParams(vmem_limit_bytes=VMEM_LIMIT),
        name="sample_mixers",
    )(*args)


PAGES_PER_STEP = 32
PAGES_PER_BLOCK = MOBA_BLOCK // PAGE_SIZE


def _paged_kmean_kernel(pt_ref, *refs):
    o_ref = refs[-1]
    j = pl.program_id(1)
    nblk = PAGES_PER_STEP // PAGES_PER_BLOCK

    @pl.when(j == 0)
    def _():
        o_ref[...] = jnp.zeros(o_ref.shape, F32)

    lane = lax.broadcasted_iota(I32, o_ref.shape, 1)
    acc = o_ref[...]
    for bb in range(nblk):
        s = refs[bb * PAGES_PER_BLOCK][...]
        for pg in range(1, PAGES_PER_BLOCK):
            s = s + refs[bb * PAGES_PER_BLOCK + pg][...]
        col = jnp.sum(s, axis=1, keepdims=True) * (1.0 / MOBA_BLOCK)
        acc = jnp.where(lane == j * nblk + bb, col, acc)
    o_ref[...] = acc


def _paged_kmean(cache_t, page_table, page_base):
    bs, n_pages = page_table.shape
    nb = n_pages // PAGES_PER_BLOCK
    page = lambda s: pl.BlockSpec((None, GROUP_W, PAGE_SIZE),
                                  lambda b, j, pt, s=s: (page_base + pt[b, j * PAGES_PER_STEP + s], 0, 0))
    grid_spec = pltpu.PrefetchScalarGridSpec(
        num_scalar_prefetch=1,
        grid=(bs, n_pages // PAGES_PER_STEP),
        in_specs=[page(s) for s in range(PAGES_PER_STEP)],
        out_specs=pl.BlockSpec((None, GROUP_W, nb), lambda b, j, pt: (b, 0, 0)))
    return pl.pallas_call(
        _paged_kmean_kernel,
        grid_spec=grid_spec,
        out_shape=jax.ShapeDtypeStruct((bs, GROUP_W, nb), F32),
        compiler_params=_params("parallel", "arbitrary"),
        name="paged_kmean",
    )(page_table, *([cache_t] * PAGES_PER_STEP))


def _sample_select_kernel(q_ref, kmt_ref, o_ref):
    head_row = lax.broadcasted_iota(I32, (8, GROUP_W), 0) == _lane_head((8, GROUP_W), 1)
    q8 = jnp.where(head_row, q_ref[...], 0.0)
    s = jnp.dot(q8, kmt_ref[...], precision=HIGHEST, preferred_element_type=F32)
    blk = lax.broadcasted_iota(I32, s.shape, 1)
    slot = lax.broadcasted_iota(I32, o_ref.shape, 1)
    out = jnp.zeros(o_ref.shape, I32)
    for r in range(MOBA_TOPK):
        mx = jnp.max(s, axis=1, keepdims=True)
        idx = jnp.min(jnp.where(s == mx, blk, s.shape[1]), axis=1, keepdims=True)
        out = jnp.where(slot == r, idx, out)
        s = jnp.where(blk == idx, NEG_PICKED, s)
    o_ref[...] = out


def _sample_select(q3, kmean_t):
    bs, _, nb = kmean_t.shape
    return pl.pallas_call(
        _sample_select_kernel,
        grid=(bs,),
        in_specs=[pl.BlockSpec((None, 1, GROUP_W), lambda b: (b, 0, 0)),
                  pl.BlockSpec((None, GROUP_W, nb), lambda b: (b, 0, 0))],
        out_specs=pl.BlockSpec((None, 8, LANE), lambda b: (b, 0, 0)),
        out_shape=jax.ShapeDtypeStruct((bs, 8, LANE), I32),
        compiler_params=_params("parallel"),
        name="sample_select",
    )(q3, kmean_t)


N_SEL_PAGES = MOBA_TOPK * PAGES_PER_BLOCK


def _sample_attn_kernel(ph_ref, q_ref, kn_ref, vn_ref, *refs):
    kp, vp, o_ref = refs[:N_SEL_PAGES], refs[N_SEL_PAGES:2 * N_SEL_PAGES], refs[-1]
    q = q_ref[...] * (HEAD_DIM ** -0.5)
    own = jnp.sum(kn_ref[...] * q, axis=0, keepdims=True)
    logits = [jnp.sum(r[...] * q, axis=0, keepdims=True) for r in kp]
    m = own
    for lg in logits:
        m = jnp.maximum(m, jnp.max(lg, axis=1, keepdims=True))
    p_own = jnp.exp(own - m)
    den = p_own
    acc = p_own * vn_ref[...]
    for lg, vr in zip(logits, vp):
        p = jnp.exp(lg - m)
        den = den + jnp.sum(p, axis=1, keepdims=True)
        acc = acc + jnp.sum(vr[...] * p, axis=1, keepdims=True)
    o_ref[...] = acc / den


def _sample_attn(phys, q_col, kn_col, vn_col, cache_kt, cache_vt, page_base):
    bs = q_col.shape[0]
    col = pl.BlockSpec((None, HEAD_DIM, 1), lambda b, h, ph: (b, h, 0))
    page = lambda s: pl.BlockSpec(
        (None, HEAD_DIM, PAGE_SIZE),
        lambda b, h, ph, s=s: (page_base + ph[(b * N_HEADS + h) * N_SEL_PAGES + s], h, 0))
    pages = [page(s) for s in range(N_SEL_PAGES)]
    grid_spec = pltpu.PrefetchScalarGridSpec(
        num_scalar_prefetch=1,
        grid=(bs, N_HEADS),
        in_specs=[col, col, col] + pages + pages,
        out_specs=col)
    return pl.pallas_call(
        _sample_attn_kernel,
        grid_spec=grid_spec,
        out_shape=jax.ShapeDtypeStruct((bs, GROUP_W, 1), F32),
        compiler_params=_params("parallel", "parallel"),
        name="sample_attn",
    )(phys, q_col, kn_col, vn_col, *([cache_kt] * N_SEL_PAGES), *([cache_vt] * N_SEL_PAGES))


def _prep_layer(w_in, w_out, pool_w, pool_scale, sgu_w, sgu_b, norm_w):
    d = w_in.shape[0]
    g = GROUP_W
    gate0 = 7 * g
    att0 = gate0 + 2 * N_HEADS
    pad = jnp.zeros((d, GATE_W - 2 * N_HEADS), w_in.dtype)
    w_z = jnp.concatenate([w_in[:, :gate0], w_in[:, att0:], w_in[:, gate0:att0], pad], axis=1).astype(BF16)
    wbd = jnp.zeros((g, g), F32)
    for i in range(len(POOL_WINDOWS)):
        sl = slice(i * POOL_GROUP, (i + 1) * POOL_GROUP)
        wbd = wbd.at[sl, sl].set(pool_w[i])
    return dict(
        w_z=w_z, w_out=w_out.astype(BF16), pool_wbd=wbd.astype(BF16),
        pool_scale=pool_scale.reshape(1, g),
        sgu_w=sgu_w, sgu_bias=jnp.repeat(sgu_b.T, HEAD_DIM, axis=1),
        sgu_w0=jnp.repeat(sgu_w[:, 0, 0], HEAD_DIM).reshape(1, g),
        sgu_b0=jnp.repeat(sgu_b[:, 0], HEAD_DIM).reshape(1, g),
        norm_w=norm_w.reshape(1, g))


def _head_cols(a, bs):
    return a.reshape(bs * N_HEADS, HEAD_DIM, 1)


def _head_rows(a, bs):
    return a.reshape(bs * N_HEADS, 1, HEAD_DIM)


def _mixers_prompt(x, batch, seq, nmw, lp, gate_b, rope_tabs, layer, depth, knew, vnew):
    g = GROUP_W
    nb = seq // MOBA_BLOCK
    zm, q, gates, gates_t, kb, vt, kmean, knew, vnew = _inproj_prompt(
        x, nmw, lp["w_z"], rope_tabs, batch, seq, layer, depth, knew, vnew)
    m_ab = _poolsgu(zm, batch, seq, lp["pool_wbd"], lp["pool_scale"], lp["sgu_w"], lp["sgu_bias"])
    m_c, ct, nst, mst = _mlstm(zm, gates, gates_t, batch, seq, gate_b, lp["norm_w"])
    blocks = lambda a: a.reshape(batch, nb, MOBA_BLOCK, g)
    m_d = _moba(q, blocks(kb), blocks(vt), kmean, batch, nb)
    h = _outproj(x, m_ab, m_c, m_d, lp["w_out"])

    c_new = jnp.stack([ct[:, i * HEAD_DIM:(i + 1) * HEAD_DIM, i * HEAD_DIM:(i + 1) * HEAD_DIM]
                       for i in range(N_HEADS)], axis=1)
    state = (zm[:, :g].reshape(batch, seq, g)[:, seq - POOL_BUF:],
             jnp.swapaxes(c_new, -1, -2), nst.reshape(batch, N_HEADS, HEAD_DIM), mst[:, :N_HEADS, 0])
    return h, state, knew, vnew


def _mixers_sample(x, past_len, nmw, lp, gate_b, rope_tabs, pool_state, c_st, n_st, m_st,
                   cache_k, cache_v, page_table, page_base):
    bs = x.shape[0]
    g = GROUP_W
    z = _inproj(x, nmw, lp["w_z"], rope_tabs)
    col = lambda c: z[:, c * g:(c + 1) * g]
    p, sv = col(COL_P), col(COL_SV)
    gates = z[:, 10 * g:10 * g + 2 * N_HEADS] + gate_b.reshape(1, -1)
    bh = bs * N_HEADS
    prefix = jnp.concatenate([jnp.zeros((1, bs, g), F32), jnp.swapaxes(pool_state, 0, 1)], axis=0)
    args = (p, prefix, lp["pool_wbd"], lp["pool_scale"], col(COL_SU), sv, lp["sgu_w0"], lp["sgu_b0"],
            _head_rows(col(COL_MQ), bs), _head_rows(col(COL_MK), bs), _head_cols(col(COL_MV), bs),
            gates[:, :N_HEADS].reshape(bh, 1, 1), gates[:, N_HEADS:].reshape(bh, 1, 1),
            c_st.reshape(bh, HEAD_DIM, HEAD_DIM), n_st.reshape(bh, 1, HEAD_DIM), m_st.reshape(bh, 1, 1),
            _head_cols(col(COL_MO), bs), _head_cols(jnp.tile(lp["norm_w"], (bs, 1)), bs))
    sds = jax.ShapeDtypeStruct
    outs = (sds((bs, g), F32), sds((bs, g), F32), sds((bh, HEAD_DIM, 1), F32),
            sds((bh, HEAD_DIM, HEAD_DIM), F32), sds((bh, 1, HEAD_DIM), F32), sds((bh, 1, 1), F32))
    y_pool, y_sgu, y_ml, c_new, n_new, m_new = _smix(args, outs, min(past_len + 1, POOL_WINDOWS[-1]))

    q_col, k_col, v_col = (col(c).reshape(bs, g, 1) for c in (COL_AQ, COL_AK, COL_AV))
    kmean_t = _paged_kmean(cache_k, page_table, page_base)
    sel = _sample_select(col(COL_AQ).reshape(bs, 1, g), kmean_t)[:, :N_HEADS, :MOBA_TOPK]
    pages = sel[..., None] * PAGES_PER_BLOCK + jnp.arange(PAGES_PER_BLOCK, dtype=I32)
    phys = jnp.take_along_axis(page_table, pages.reshape(bs, -1), axis=1).reshape(-1)
    y_at = _sample_attn(phys, q_col, k_col, v_col, cache_k, cache_v, page_base).reshape(bs, g)

    m_ab = jnp.concatenate([y_pool, y_sgu], axis=1).astype(BF16)
    h = _outproj(x, m_ab, y_ml.reshape(bs, g).astype(BF16), y_at.astype(BF16), lp["w_out"])

    heads = lambda a: a.reshape(bs, 1, N_HEADS, HEAD_DIM)
    state = (heads(col(COL_AK)), heads(col(COL_AV)),
             jnp.concatenate([pool_state[:, 1:], p[:, None, :]], axis=1), sv.reshape(bs, 1, g),
             c_new.reshape(bs, N_HEADS, HEAD_DIM, HEAD_DIM), n_new.reshape(bs, N_HEADS, HEAD_DIM),
             m_new.reshape(bs, N_HEADS))
    return h, state


def kernel(x_prompt, x_sample, cache_k, cache_v, page_table, state_pool, state_mlstm_c, state_mlstm_n,
           state_mlstm_m, norm_mix_w, norm_ffn_w, final_norm_w, w_in, w_out, pool_w, pool_scale, sgu_w,
           sgu_b, mlstm_gate_b, mlstm_norm_w, ffn_w1, ffn_w3, ffn_w2, router_w, router_b,
           moe_w1, moe_w3, moe_w2):
    batch, seq, d = x_prompt.shape
    bs = x_sample.shape[0]
    depth = w_in.shape[0]
    n_pool = cache_k.shape[1]
    n_pages = page_table.shape[1]
    past_len = n_pages * PAGE_SIZE
    n_experts = router_w.shape[-1]
    assert x_sample.shape[1] == 1 and d == N_MIXERS * GROUP_W
    assert seq % MOBA_BLOCK == 0 and seq % (2 * MLSTM_CHUNK) == 0 and seq >= POOL_BUF
    assert past_len % MOBA_BLOCK == 0 and past_len // MOBA_BLOCK >= MOBA_TOPK
    assert n_pages % PAGES_PER_STEP == 0

    rope_p = _rope_tables(jnp.arange(seq, dtype=I32))
    rope_s = _rope_tables(jnp.full((bs,), past_len, I32))
    pages_t = lambda c: jnp.transpose(c, (0, 1, 3, 4, 2)).reshape(depth * n_pool, GROUP_W, PAGE_SIZE)
    ck, cv = pages_t(cache_k), pages_t(cache_v)
    fw = final_norm_w.reshape(1, d)
    tm_moe = 512

    hp = x_prompt.reshape(batch * seq, d)
    hs = x_sample.reshape(bs, d)
    st_p, st_s = [], []
    knew = jnp.zeros((depth, batch, GROUP_W, seq), F32)
    vnew = jnp.zeros((depth, batch, GROUP_W, seq), F32)
    for l in range(depth):
        lp = _prep_layer(w_in[l], w_out[l], pool_w[l], pool_scale[l], sgu_w[l], sgu_b[l], mlstm_norm_w[l])
        nmw = norm_mix_w[l].reshape(1, d)
        nfw = norm_ffn_w[l].reshape(1, d)
        hp, sp, knew, vnew = _mixers_prompt(hp, batch, seq, nmw, lp, mlstm_gate_b[l], rope_p,
                                            l, depth, knew, vnew)
        hs, ss = _mixers_sample(hs, past_len, nmw, lp, mlstm_gate_b[l], rope_s, state_pool[l],
                                state_mlstm_c[l], state_mlstm_n[l], state_mlstm_m[l],
                                ck, cv, page_table, l * n_pool)
        st_p.append(sp)
        st_s.append(ss)
        last = l == depth - 1
        j = l // 2
        if l % 2 == 0:
            w1, w3, w2 = ffn_w1[j].astype(BF16), ffn_w3[j].astype(BF16), ffn_w2[j].astype(BF16)
            hp = _ffn(hp, nfw, w1, w3, w2)
            hs = _ffn(hs, nfw, w1, w3, w2)
            if last:
                hp, hs = _final_norm(hp, fw), _final_norm(hs, fw)
        else:
            w1, w3, w2 = moe_w1[j].astype(BF16), moe_w3[j].astype(BF16), moe_w2[j].astype(BF16)
            rw = jnp.zeros((d, LANE), F32).at[:, :n_experts].set(router_w[j])
            rb = jnp.full((1, LANE), -jnp.inf, F32).at[0, :n_experts].set(router_b[j])
            xp, gates, e_idx = _router(hp, nfw, rw, rb)
            pos, src, tile_expert, nused = _dispatch_plan(e_idx, n_experts, tm_moe)
            o_sorted = _gmm(xp, tile_expert, src, nused, w1, w3, w2, tm_moe)
            hp = _combine(pos, hp, gates, o_sorted, fw, last)
            _, gates_s, e_s = _router(hs, nfw, rw, rb)
            hs = _moe_small(hs, nfw, gates_s, e_s, w1, w3, w2, fw, last)

    stack = lambda sts, i: jnp.stack([s[i] for s in sts])
    kv_leaf = lambda a: jnp.transpose(a.reshape(depth, batch, N_HEADS, HEAD_DIM, seq), (0, 1, 4, 2, 3))
    return (hp.reshape(batch, seq, d), hs.reshape(bs, 1, d),
            kv_leaf(knew), kv_leaf(vnew), stack(st_s, 0), stack(st_s, 1),
            stack(st_p, 0), stack(st_s, 2), stack(st_s, 3),
            stack(st_p, 1), stack(st_p, 2), stack(st_p, 3),
            stack(st_s, 4), stack(st_s, 5), stack(st_s, 6))
```

```python
import functools

import numpy as np
import jax
import jax.numpy as jnp
from jax import lax
from jax.experimental import pallas as pl
from jax.experimental.pallas import tpu as pltpu

F32 = jnp.float32
BF16 = jnp.bfloat16
I32 = jnp.int32
HIGHEST = lax.Precision.HIGHEST

N_MIXERS = 4
HEAD_DIM = 64
N_HEADS = 4
GROUP_W = N_HEADS * HEAD_DIM
POOL_WINDOWS = (2, 4, 8, 16)
POOL_GROUP = GROUP_W // len(POOL_WINDOWS)
POOL_BUF = max(POOL_WINDOWS) - 1
HALO = POOL_BUF + 1
SGU_CHUNK = 128
MLSTM_CHUNK = 64
MOBA_BLOCK = 256
MOBA_TOPK = 3
PAGE_SIZE = 128
ROPE_THETA = 500000.0
ROT_DIM = HEAD_DIM // 4
TOP_K = 2
NORM_EPS = 1e-6
NEG = -1e30
NEG_PICKED = -3e38

LANE = 128
GATE_W = LANE
Z_W = 10 * GROUP_W + GATE_W
Z_TN = 7 * LANE
COL_P, COL_SU, COL_SV, COL_MQ, COL_MK, COL_MV, COL_MO, COL_AQ, COL_AK, COL_AV = range(10)
COL_GATE = (10 * GROUP_W) // GATE_W
VMEM_LIMIT = 56 * 1024 * 1024
ISSUE_UNROLL = 8


def _params(*sem):
    return pltpu.CompilerParams(dimension_semantics=sem, vmem_limit_bytes=VMEM_LIMIT)


def _row_tile(n, pref):
    return pref if n % pref == 0 else n


def _lane_head(shape, dim):
    return lax.broadcasted_iota(I32, shape, dim) // HEAD_DIM


def _by_head(vals, lh):
    out = vals[N_HEADS - 1]
    for h in range(N_HEADS - 2, -1, -1):
        out = jnp.where(lh == h, vals[h], out)
    return out


def _dot_nt(a, b, precision=None):
    return lax.dot_general(a, b, (((1,), (1,)), ((), ())), precision=precision,
                           preferred_element_type=F32)


def _logsig(x):
    return jnp.minimum(x, 0.0) - jnp.log1p(jnp.exp(-jnp.abs(x)))


def _inproj_kernel(x_ref, nw_ref, w_ref, cos_ref, sa_ref, sb_ref, z_ref, xn_ref):
    j = pl.program_id(1)

    @pl.when(j == 0)
    def _():
        x = x_ref[...]
        ms = jnp.mean(x * x, axis=-1, keepdims=True)
        xn_ref[...] = (x * lax.rsqrt(ms + NORM_EPS) * nw_ref[...]).astype(BF16)

    z = jnp.dot(xn_ref[...], w_ref[...], preferred_element_type=F32)

    @pl.when(j < 2)
    def _():
        z_ref[...] = z

    @pl.when(j == 2)
    def _():
        c, sa, sb = cos_ref[...], sa_ref[...], sb_ref[...]
        half = ROT_DIM // 2
        for o in (0, GROUP_W):
            t = z[:, o:o + GROUP_W]
            z_ref[:, o:o + GROUP_W] = (t * c + pltpu.roll(t, GROUP_W - half, 1) * sa
                                       + pltpu.roll(t, half, 1) * sb)
        z_ref[:, 2 * GROUP_W:] = z[:, 2 * GROUP_W:]


def _inproj(x, nw, w, rope_tabs):
    n, d = x.shape
    tm = _row_tile(n, 512)
    tab_tiles = rope_tabs[0].shape[0] // tm
    tab_spec = pl.BlockSpec((tm, GROUP_W), lambda i, j: (i % tab_tiles, 0))
    return pl.pallas_call(
        _inproj_kernel,
        grid=(n // tm, Z_W // Z_TN),
        in_specs=[pl.BlockSpec((tm, d), lambda i, j: (i, 0)),
                  pl.BlockSpec((1, d), lambda i, j: (0, 0)),
                  pl.BlockSpec((d, Z_TN), lambda i, j: (0, j)),
                  tab_spec, tab_spec, tab_spec],
        out_specs=pl.BlockSpec((tm, Z_TN), lambda i, j: (i, j)),
        out_shape=jax.ShapeDtypeStruct((n, Z_W), F32),
        scratch_shapes=[pltpu.VMEM((tm, d), BF16)],
        compiler_params=_params("parallel", "arbitrary"),
        name="inproj",
    )(x, nw, w, *rope_tabs)


def _rope(t, c, sa, sb):
    half = ROT_DIM // 2
    return t * c + pltpu.roll(t, GROUP_W - half, 1) * sa + pltpu.roll(t, half, 1) * sb


def _inproj_prompt_kernel(x_ref, nw_ref, w_ref, cos_ref, sa_ref, sb_ref, kprev_hbm, vprev_hbm,
                          zm_ref, q_ref, g_ref, gt_ref, kb_ref, vt_ref, km_ref, knew_ref, vnew_ref,
                          *, tiles_per_seq):
    del kprev_hbm, vprev_hbm
    i = pl.program_id(0)
    tm = x_ref.shape[0]
    g = GROUP_W

    ti = i % tiles_per_seq

    @pl.when(ti == 0)
    def _():
        km_ref[...] = jnp.zeros(km_ref.shape, F32)

    x = x_ref[...]
    ms = jnp.mean(x * x, axis=-1, keepdims=True)
    xn = (x * lax.rsqrt(ms + NORM_EPS) * nw_ref[...]).astype(BF16)
    for jj in range(2):
        zm_ref[:, jj * Z_TN:(jj + 1) * Z_TN] = jnp.dot(xn, w_ref[:, jj * Z_TN:(jj + 1) * Z_TN],
                                                       preferred_element_type=F32)
    z = jnp.dot(xn, w_ref[:, 2 * Z_TN:], preferred_element_type=F32)

    c, sa, sb = cos_ref[...], sa_ref[...], sb_ref[...]
    q_ref[...] = _rope(z[:, 0:g], c, sa, sb)
    k = _rope(z[:, g:2 * g], c, sa, sb)
    v = z[:, 2 * g:3 * g]
    gates = z[:, 3 * g:]
    g_ref[...] = gates
    gt_ref[...] = gates.T[0:2 * N_HEADS, :]
    nblk = tm // MOBA_BLOCK
    kb_ref[...] = k.astype(BF16).reshape(nblk, MOBA_BLOCK, g)
    knew_ref[...] = k.T
    v_t = v.T
    vnew_ref[...] = v_t
    km = km_ref[...]
    row = lax.broadcasted_iota(I32, km.shape, 0)
    for b in range(nblk):
        cols = slice(b * MOBA_BLOCK, (b + 1) * MOBA_BLOCK)
        vt_ref[b] = v_t[:, cols].astype(BF16)
        mean = jnp.sum(k[cols, :], axis=0, keepdims=True) * (1.0 / MOBA_BLOCK)
        km = jnp.where(row == ti * nblk + b, mean, km)
    km_ref[...] = km


def _inproj_prompt(x, nw, w, rope_tabs, batch, seq, layer, depth, knew, vnew):
    n, d = x.shape
    tm = 2 * MOBA_BLOCK
    assert seq % tm == 0
    tps = seq // tm
    nb = seq // MOBA_BLOCK
    nblk = tm // MOBA_BLOCK
    g = GROUP_W
    tab_spec = pl.BlockSpec((tm, g), lambda i: (i % tps, 0))
    slab = pl.BlockSpec((None, None, g, tm), lambda i: (layer, i // tps, 0, i % tps))
    blocks = pl.BlockSpec((nblk, MOBA_BLOCK, g), lambda i: (i, 0, 0))
    rows = lambda c: pl.BlockSpec((tm, c), lambda i: (i, 0))
    sds = jax.ShapeDtypeStruct
    slab_shape = sds((depth, batch, g, seq), F32)
    any_spec = pl.BlockSpec(memory_space=pl.ANY)
    return pl.pallas_call(
        functools.partial(_inproj_prompt_kernel, tiles_per_seq=tps),
        grid=(n // tm,),
        in_specs=[rows(d),
                  pl.BlockSpec((1, d), lambda i: (0, 0)),
                  pl.BlockSpec((d, Z_W), lambda i: (0, 0)),
                  tab_spec, tab_spec, tab_spec, any_spec, any_spec],
        out_specs=[rows(2 * Z_TN), rows(g), rows(GATE_W),
                   pl.BlockSpec((2 * N_HEADS, tm), lambda i: (0, i)),
                   blocks, blocks,
                   pl.BlockSpec((None, nb, g), lambda i: (i // tps, 0, 0)),
                   slab, slab],
        out_shape=[sds((n, 2 * Z_TN), F32), sds((n, g), F32), sds((n, GATE_W), F32),
                   sds((2 * N_HEADS, n), F32),
                   sds((batch * nb, MOBA_BLOCK, g), BF16), sds((batch * nb, g, MOBA_BLOCK), BF16),
                   sds((batch, nb, g), F32), slab_shape, slab_shape],
        input_output_aliases={6: 7, 7: 8},
        compiler_params=_params("arbitrary"),
        name="inproj_prompt",
    )(x, nw, w, *rope_tabs, knew, vnew)


def _rope_tables(pos):
    half = ROT_DIM // 2
    inv = 1.0 / (ROPE_THETA ** (jnp.arange(half, dtype=F32) / half))
    ang = pos.astype(F32)[:, None] * inv[None, :]
    cos, sin = jnp.cos(ang), jnp.sin(ang)
    t = pos.shape[0]
    rest = HEAD_DIM - ROT_DIM
    one = jnp.ones((t, rest), F32)
    zero = jnp.zeros((t, rest), F32)
    zh = jnp.zeros((t, half), F32)
    c = jnp.concatenate([cos, cos, one], axis=1)
    sa = jnp.concatenate([-sin, zh, zero], axis=1)
    sb = jnp.concatenate([zh, sin, zero], axis=1)
    tile = lambda a: jnp.tile(a, (1, N_HEADS))
    return tile(c), tile(sa), tile(sb)


def _window_of_lane(shape, dim, lo):
    g = (lax.broadcasted_iota(I32, shape, dim) + lo) // POOL_GROUP
    w = jnp.full(shape, POOL_WINDOWS[-1], I32)
    for i in range(len(POOL_WINDOWS) - 2, -1, -1):
        w = jnp.where(g == i, POOL_WINDOWS[i], w)
    return w


def _poolsgu_kernel(p_ref, su_ref, sv_ref, pw_ref, ps_ref, sw_ref, sb_ref, o_ref, ebuf):
    t = pl.program_id(1)
    tt = p_ref.shape[0]

    @pl.when(t == 0)
    def _():
        ebuf[0:HALO, :] = jnp.zeros((HALO, GROUP_W), F32)

    @pl.when(t > 0)
    def _():
        ebuf[0:HALO, :] = ebuf[tt:tt + HALO, :]

    ebuf[HALO:HALO + tt, :] = p_ref[...]

    pos1 = lax.broadcasted_iota(I32, (tt, LANE), 0) + t * tt + 1
    halves = []
    for half in range(2):
        lo = half * LANE
        wsmall, wbig = POOL_WINDOWS[2 * half], POOL_WINDOWS[2 * half + 1]
        e0 = ebuf[HALO:HALO + tt, lo:lo + LANE]
        acc = e0
        for off in range(1, wsmall):
            acc = acc + ebuf[HALO - off:HALO - off + tt, lo:lo + LANE]
        small = acc
        for off in range(wsmall, wbig):
            acc = acc + ebuf[HALO - off:HALO - off + tt, lo:lo + LANE]
        first = lax.broadcasted_iota(I32, (tt, LANE), 1) < POOL_GROUP
        wsum = jnp.where(first, small, acc)
        cnt = jnp.minimum(pos1, _window_of_lane((tt, LANE), 1, lo)).astype(F32)
        halves.append(wsum / cnt - e0)
    d = jnp.concatenate(halves, axis=1).astype(BF16)
    y_pool = jnp.dot(d, pw_ref[...], preferred_element_type=F32) * ps_ref[...]
    o_ref[:, 0:GROUP_W] = y_pool.astype(o_ref.dtype)

    row = lax.broadcasted_iota(I32, (SGU_CHUNK, SGU_CHUNK), 0)
    col = lax.broadcasted_iota(I32, (SGU_CHUNK, SGU_CHUNK), 1)
    ws = [jnp.where(row >= col, sw_ref[h], 0.0).astype(BF16) for h in range(N_HEADS)]
    lh = _lane_head((SGU_CHUNK, GROUP_W), 1)
    for c in range(tt // SGU_CHUNK):
        rows = slice(c * SGU_CHUNK, (c + 1) * SGU_CHUNK)
        v = sv_ref[rows, :].astype(BF16)
        mixed = _by_head([jnp.dot(ws[h], v, preferred_element_type=F32) for h in range(N_HEADS)], lh)
        y = su_ref[rows, :] * (mixed + sb_ref[...])
        o_ref[rows, GROUP_W:2 * GROUP_W] = y.astype(o_ref.dtype)


def _poolsgu(z, batch, seq, pool_wbd, pool_scale, sgu_w, sgu_bias):
    tt = _row_tile(seq, 512)
    tps = seq // tt
    zspec = lambda c: pl.BlockSpec((tt, GROUP_W), lambda b, t, c=c: (b * tps + t, c))
    return pl.pallas_call(
        _poolsgu_kernel,
        grid=(batch, tps),
        in_specs=[zspec(COL_P), zspec(COL_SU), zspec(COL_SV),
                  pl.BlockSpec((GROUP_W, GROUP_W), lambda b, t: (0, 0)),
                  pl.BlockSpec((1, GROUP_W), lambda b, t: (0, 0)),
                  pl.BlockSpec((N_HEADS, SGU_CHUNK, SGU_CHUNK), lambda b, t: (0, 0, 0)),
                  pl.BlockSpec((SGU_CHUNK, GROUP_W), lambda b, t: (0, 0))],
        out_specs=pl.BlockSpec((tt, 2 * GROUP_W), lambda b, t: (b * tps + t, 0)),
        out_shape=jax.ShapeDtypeStruct((batch * seq, 2 * GROUP_W), BF16),
        scratch_shapes=[pltpu.VMEM((HALO + tt, GROUP_W), F32)],
        compiler_params=_params("parallel", "arbitrary"),
        name="pool_sgu",
    )(z, z, z, pool_wbd, pool_scale, sgu_w, sgu_bias)


def _mlstm_kernel(q_ref, k_ref, v_ref, o_ref, g_ref, *rest):
    L = MLSTM_CHUNK
    batch = q_ref.shape[0]
    gt_refs = rest[:batch]
    gbc_ref, gbr_ref, nw_ref, y_ref, ct_ref, n_ref, m_ref = rest[batch:]

    @pl.when(pl.program_id(0) == 0)
    def _():
        ct_ref[...] = jnp.zeros(ct_ref.shape, F32)
        n_ref[...] = jnp.zeros(n_ref.shape, F32)
        m_ref[...] = jnp.zeros(m_ref.shape, F32)

    r_i = lax.broadcasted_iota(I32, (L, L), 0)
    c_i = lax.broadcasted_iota(I32, (L, L), 1)
    causal = r_i >= c_i
    tri_l = causal.astype(F32)
    tri_u = (r_i <= c_i).astype(F32)
    lh = _lane_head((L, GROUP_W), 1)
    bd = _lane_head((GROUP_W, GROUP_W), 0) == _lane_head((GROUP_W, GROUP_W), 1)
    same_head = bd.astype(F32)

    ct_b = [ct_ref[b] for b in range(batch)]
    n_b = [n_ref[b] for b in range(batch)]
    m_b = [[m_ref[b, h:h + 1, 0:1] for h in range(N_HEADS)] for b in range(batch)]

    def chunk(c, b):
        ct, nrow, m_heads = ct_b[b], n_b[b], m_b[b]
        rows = slice(c * L, (c + 1) * L)
        q = q_ref[b, rows, :]
        ks = k_ref[b, rows, :] * (HEAD_DIM ** -0.5)
        v = v_ref[b, rows, :]
        g = g_ref[b, rows, :] + gbc_ref[...]
        gt = gt_refs[b][:, rows] + gbr_ref[...]
        bcol = jnp.dot(tri_l, _logsig(g), precision=HIGHEST, preferred_element_type=F32)
        brow = jnp.dot(_logsig(gt), tri_u, precision=HIGHEST, preferred_element_type=F32)
        qb, kb, vb = q.astype(BF16), ks.astype(BF16), v.astype(BF16)

        s_l, inter_l, deni_l, emt_l, wsrc_l, dec_l, mnew_l = [], [], [], [], [], [], []
        for h in range(N_HEADS):
            bc = bcol[:, N_HEADS + h:N_HEADS + h + 1]
            ic = g[:, h:h + 1]
            br = brow[N_HEADS + h:N_HEADS + h + 1, :]
            ir = gt[h:h + 1, :]
            dm = jnp.where(causal, bc - br + ir, NEG)
            a = bc + m_heads[h]
            mt = jnp.maximum(a, jnp.max(dm, axis=1, keepdims=True))
            qm = jnp.where(lh == h, q, 0.0).astype(BF16)
            s = _dot_nt(qm, kb) * jnp.exp(dm - mt)
            m_new = mt[L - 1:L, :]
            s_l.append(s)
            inter_l.append(jnp.exp(a - mt))
            deni_l.append(jnp.sum(s, axis=1, keepdims=True))
            emt_l.append(jnp.exp(-mt))
            wsrc_l.append(jnp.exp(bc[L - 1:L, :] - bc + ic - m_new))
            dec_l.append(jnp.exp(a[L - 1:L, :] - m_new))
            mnew_l.append(m_new)

        r = jnp.dot(jnp.concatenate(s_l, axis=0).astype(BF16), vb, preferred_element_type=F32)
        num_i = _by_head([r[h * L:(h + 1) * L, :] for h in range(N_HEADS)], lh)
        q_c = jnp.dot(qb, ct.astype(BF16), preferred_element_type=F32)
        q_n = jnp.dot(q * nrow, same_head, precision=HIGHEST, preferred_element_type=F32)
        inter = _by_head(inter_l, lh)
        num = num_i + inter * q_c
        den = _by_head(deni_l, lh) + inter * q_n
        hval = num / jnp.maximum(jnp.abs(den), _by_head(emt_l, lh))
        ms = jnp.dot(hval * hval, same_head, precision=HIGHEST,
                     preferred_element_type=F32) * (1.0 / HEAD_DIM)
        y = hval * lax.rsqrt(ms + NORM_EPS) * nw_ref[...]
        y_ref[b, rows, :] = (jax.nn.sigmoid(o_ref[b, rows, :]) * y).astype(y_ref.dtype)

        kw = ks * _by_head(wsrc_l, lh)
        dec = _by_head(dec_l, lh[0:1, :])
        upd = jnp.dot(kw.T.astype(BF16), vb, preferred_element_type=F32)
        ct_b[b] = ct * dec + jnp.where(bd, upd, 0.0)
        n_b[b] = nrow * dec + jnp.sum(kw, axis=0, keepdims=True)
        m_b[b] = mnew_l

    for c in range(q_ref.shape[1] // L):
        for b in range(batch):
            chunk(c, b)

    for b in range(batch):
        ct_ref[b] = ct_b[b]
        n_ref[b] = n_b[b]
        for h in range(N_HEADS):
            m_ref[b, h:h + 1, :] = jnp.broadcast_to(m_b[b][h], (1, m_ref.shape[2]))


def _mlstm(z, gates, gates_t, batch, seq, gate_b, norm_w):
    rows = _row_tile(seq, 2 * MLSTM_CHUNK)
    tps = seq // rows
    z3 = z.reshape(batch, seq, z.shape[1])
    zspec = lambda c: pl.BlockSpec((batch, rows, GROUP_W), lambda t, c=c: (0, t, c))
    gbc = jnp.zeros((1, GATE_W), F32).at[0, :2 * N_HEADS].set(gate_b.reshape(-1))
    gbr = gate_b.reshape(2 * N_HEADS, 1)
    state = lambda r, c: (pl.BlockSpec((batch, r, c), lambda t: (0, 0, 0)),
                          jax.ShapeDtypeStruct((batch, r, c), F32))
    (cs, csh), (ns, nsh), (msp, msh) = state(GROUP_W, GROUP_W), state(1, GROUP_W), state(8, LANE)
    gt_specs = [pl.BlockSpec((2 * N_HEADS, rows), lambda t, b=b: (0, b * tps + t)) for b in range(batch)]
    y, ct, nst, mst = pl.pallas_call(
        _mlstm_kernel,
        grid=(tps,),
        in_specs=[zspec(COL_MQ), zspec(COL_MK), zspec(COL_MV), zspec(COL_MO),
                  pl.BlockSpec((batch, rows, GATE_W), lambda t: (0, t, 0))] + gt_specs + [
                  pl.BlockSpec((1, GATE_W), lambda t: (0, 0)),
                  pl.BlockSpec((2 * N_HEADS, 1), lambda t: (0, 0)),
                  pl.BlockSpec((1, GROUP_W), lambda t: (0, 0))],
        out_specs=[pl.BlockSpec((batch, rows, GROUP_W), lambda t: (0, t, 0)), cs, ns, msp],
        out_shape=[jax.ShapeDtypeStruct((batch, seq, GROUP_W), BF16), csh, nsh, msh],
        compiler_params=_params("arbitrary"),
        name="mlstm",
    )(z3, z3, z3, z3, gates.reshape(batch, seq, GATE_W), *([gates_t] * batch), gbc, gbr, norm_w)
    return y.reshape(batch * seq, GROUP_W), ct, nst, mst


def _pick_top_blocks(s, n_valid_mask):
    blk = lax.broadcasted_iota(I32, s.shape, 0)
    s = jnp.where(n_valid_mask, s, NEG)
    bias = jnp.full(s.shape, NEG, F32)
    for _ in range(MOBA_TOPK):
        mx = jnp.max(s, axis=0, keepdims=True)
        idx = jnp.min(jnp.where(s == mx, blk, s.shape[0]), axis=0, keepdims=True)
        pick = blk == idx
        bias = jnp.where(jnp.logical_and(pick, mx > 0.5 * NEG), 0.0, bias)
        s = jnp.where(pick, NEG_PICKED, s)
    return bias


MOBA_KEY_CHUNK = 64
LOG2E = 1.4426950408889634


def _moba_logits(k_blk, qbd_sc, lt_sc, buf):
    lt_sc[buf] = jnp.dot(k_blk[...], qbd_sc[...], preferred_element_type=F32)


def _moba_softmax_pv(h, buf, vt_blk, lt_sc, p_sc, m_sc, l_sc, acc_sc, bias, key_le_query):
    B = MOBA_BLOCK
    ch = MOBA_KEY_CHUNK
    groups = ch // 8
    first = bias is None
    cols = slice(h * B, (h + 1) * B)
    mx = None
    for c in range(B // ch):
        rows = slice(c * ch, (c + 1) * ch)
        s = lt_sc[buf, rows, cols]
        if first:
            s = jnp.where(key_le_query[rows, :], s, NEG)
            lt_sc[buf, rows, cols] = s
        part = jnp.max(s.reshape(groups, 8, B), axis=0)
        mx = part if mx is None else jnp.maximum(mx, part)
    cand = jnp.max(mx, axis=0, keepdims=True)
    if first:
        m_new = shift = cand
    else:
        m_old = m_sc[h:h + 1, :]
        m_new = jnp.maximum(m_old, cand + bias)
        alpha = jnp.exp2(m_old - m_new)
        shift = m_new - bias
    lsum = None
    for c in range(B // ch):
        rows = slice(c * ch, (c + 1) * ch)
        p = jnp.exp2(lt_sc[buf, rows, cols] - shift)
        part = jnp.sum(p.reshape(groups, 8, B), axis=0)
        lsum = part if lsum is None else lsum + part
        p_sc[h, rows, :] = p.astype(BF16)
    lnew = jnp.sum(lsum, axis=0, keepdims=True)
    hs = slice(h * HEAD_DIM, (h + 1) * HEAD_DIM)
    pv = jnp.dot(vt_blk[hs, :], p_sc[h], preferred_element_type=F32)
    if first:
        l_sc[h:h + 1, :] = lnew
        acc_sc[hs, :] = pv
    else:
        l_sc[h:h + 1, :] = alpha * l_sc[h:h + 1, :] + lnew
        acc_sc[hs, :] = alpha * acc_sc[hs, :] + pv
    m_sc[h:h + 1, :] = m_new


def _moba_kernel(q_ref, kb_ref, vt_ref, km_ref, o_ref, qbd_sc, sel_sc, m_sc, l_sc, acc_sc, lt_sc, p_sc):
    i = pl.program_id(1)
    nb = km_ref.shape[0]
    nbp = sel_sc.shape[0] // N_HEADS
    B = MOBA_BLOCK
    q = q_ref[...]
    km = km_ref[...]
    lh = _lane_head((B, GROUP_W), 1)
    past = lax.broadcasted_iota(I32, (nb, B), 0) < i
    key_le_query = lax.broadcasted_iota(I32, (B, B), 0) <= lax.broadcasted_iota(I32, (B, B), 1)
    stats = (lt_sc, p_sc, m_sc, l_sc, acc_sc)

    q_t = q.T * (HEAD_DIM ** -0.5 * LOG2E)
    row_head = _lane_head((GROUP_W, B), 0)
    for h in range(N_HEADS):
        s_blk = _dot_nt(km, jnp.where(lh == h, q, 0.0), precision=HIGHEST)
        sel_sc[h * nbp:h * nbp + nb, :] = _pick_top_blocks(s_blk, past)
        qbd_sc[:, h * B:(h + 1) * B] = jnp.where(row_head == h, q_t, 0.0).astype(BF16)

    _moba_logits(kb_ref.at[i], qbd_sc, lt_sc, 1)
    _moba_logits(kb_ref.at[0], qbd_sc, lt_sc, 0)
    for h in range(N_HEADS):
        _moba_softmax_pv(h, 1, vt_ref.at[i], *stats, None, key_le_query)

    def step(n, buf):
        _moba_logits(kb_ref.at[jnp.minimum(n + 1, nb - 1)], qbd_sc, lt_sc, 1 - buf)
        for h in range(N_HEADS):
            bias = sel_sc[pl.ds(h * nbp + n, 1), :]
            _moba_softmax_pv(h, buf, vt_ref.at[n], *stats, bias, None)

    def body(k, carry):
        step(2 * k, 0)
        step(2 * k + 1, 1)
        return carry

    lax.fori_loop(0, (i + 1) // 2, body, 0)

    for h in range(N_HEADS):
        hs = slice(h * HEAD_DIM, (h + 1) * HEAD_DIM)
        acc_sc[hs, :] = acc_sc[hs, :] / l_sc[h:h + 1, :]
    o_ref[...] = acc_sc[...].T.astype(o_ref.dtype)


def _moba(q, kb4, vt4, kmean, batch, nb):
    nbp = -(-nb // 8) * 8
    B = MOBA_BLOCK
    blk4 = pl.BlockSpec((None, nb, B, GROUP_W), lambda b, i: (b, 0, 0, 0))
    return pl.pallas_call(
        _moba_kernel,
        grid=(batch, nb),
        in_specs=[pl.BlockSpec((B, GROUP_W), lambda b, i: (b * nb + i, 0)),
                  blk4, blk4,
                  pl.BlockSpec((None, nb, GROUP_W), lambda b, i: (b, 0, 0))],
        out_specs=pl.BlockSpec((B, GROUP_W), lambda b, i: (b * nb + i, 0)),
        out_shape=jax.ShapeDtypeStruct((batch * nb * B, GROUP_W), BF16),
        scratch_shapes=[pltpu.VMEM((GROUP_W, N_HEADS * B), BF16),
                        pltpu.VMEM((N_HEADS * nbp, B), F32),
                        pltpu.VMEM((8, B), F32), pltpu.VMEM((8, B), F32),
                        pltpu.VMEM((GROUP_W, B), F32),
                        pltpu.VMEM((2, B, N_HEADS * B), F32),
                        pltpu.VMEM((N_HEADS, B, B), BF16)],
        compiler_params=_params("parallel", "arbitrary"),
        name="moba",
    )(q, kb4, vt4, kmean)


def _outproj_kernel(x_ref, ma_ref, mc_ref, md_ref, w_ref, o_ref):
    g2 = 2 * GROUP_W
    acc = jnp.dot(ma_ref[...], w_ref[0:g2, :], preferred_element_type=F32)
    acc += jnp.dot(mc_ref[...], w_ref[g2:g2 + GROUP_W, :], preferred_element_type=F32)
    acc += jnp.dot(md_ref[...], w_ref[g2 + GROUP_W:, :], preferred_element_type=F32)
    o_ref[...] = x_ref[...] + acc


def _outproj(x, ma, mc, md, w):
    n, d = x.shape
    tm = _row_tile(n, 512)
    rows = lambda c: pl.BlockSpec((tm, c), lambda i: (i, 0))
    return pl.pallas_call(
        _outproj_kernel,
        grid=(n // tm,),
        in_specs=[rows(d), rows(2 * GROUP_W), rows(GROUP_W), rows(GROUP_W),
                  pl.BlockSpec(w.shape, lambda i: (0, 0))],
        out_specs=rows(d),
        out_shape=jax.ShapeDtypeStruct((n, d), F32),
        compiler_params=_params("parallel"),
        name="outproj",
    )(x, ma, mc, md, w)


def _ff_tile(d_ff):
    return d_ff // 2 if (d_ff // 2) % LANE == 0 else d_ff


def _swiglu_partial(xn, w1_ref, w3_ref, w2_ref):
    a = jnp.dot(xn, w1_ref[...], preferred_element_type=F32)
    b = jnp.dot(xn, w3_ref[...], preferred_element_type=F32)
    hmid = (a * jax.nn.sigmoid(a) * b).astype(BF16)
    return jnp.dot(hmid, w2_ref[...], preferred_element_type=F32)


def _swiglu_full(xn, w1_ref, w3_ref, w2_ref):
    d_ff = w1_ref.shape[1]
    tf = _ff_tile(d_ff)
    acc = None
    for jj in range(d_ff // tf):
        sl = slice(jj * tf, (jj + 1) * tf)
        part = _swiglu_partial(xn, w1_ref.at[:, sl], w3_ref.at[:, sl], w2_ref.at[sl, :])
        acc = part if acc is None else acc + part
    return acc


def _ffn_kernel(x_ref, nw_ref, w1_ref, w3_ref, w2_ref, o_ref):
    x = x_ref[...]
    ms = jnp.mean(x * x, axis=-1, keepdims=True)
    xn = (x * lax.rsqrt(ms + NORM_EPS) * nw_ref[...]).astype(BF16)
    o_ref[...] = x + _swiglu_full(xn, w1_ref, w3_ref, w2_ref)


def _ffn(x, nw, w1, w3, w2):
    n, d = x.shape
    tm = _row_tile(n, 512)
    resident = lambda a: pl.BlockSpec(a.shape, lambda i: (0, 0), pipeline_mode=pl.Buffered(1))
    return pl.pallas_call(
        _ffn_kernel,
        grid=(n // tm,),
        in_specs=[pl.BlockSpec((tm, d), lambda i: (i, 0)),
                  pl.BlockSpec((1, d), lambda i: (0, 0)),
                  resident(w1), resident(w3), resident(w2)],
        out_specs=pl.BlockSpec((tm, d), lambda i: (i, 0)),
        out_shape=jax.ShapeDtypeStruct((n, d), F32),
        compiler_params=_params("parallel"),
        name="ffn_dense",
    )(x, nw, w1, w3, w2)


def _router_kernel(x_ref, nw_ref, rw_ref, rb_ref, xp_ref, g_ref, e_ref):
    x = x_ref[...]
    ms = jnp.mean(x * x, axis=-1, keepdims=True)
    xn = x * lax.rsqrt(ms + NORM_EPS) * nw_ref[...]
    tm, d = x.shape
    half = d // 2
    bits = pltpu.bitcast(xn.astype(BF16).astype(F32), jnp.uint32)
    packed = jnp.bitwise_or(jnp.right_shift(bits[:, :half], jnp.uint32(16)), bits[:, half:])
    nc = half // LANE
    for c in range(nc):
        xp_ref[pl.ds(c, tm, stride=nc), :] = packed[:, c * LANE:(c + 1) * LANE]

    logits = jnp.dot(xn, rw_ref[...], precision=HIGHEST, preferred_element_type=F32) + rb_ref[...]
    lane = lax.broadcasted_iota(I32, logits.shape, 1)
    mx1 = jnp.max(logits, axis=1, keepdims=True)
    i1 = jnp.min(jnp.where(logits == mx1, lane, LANE), axis=1, keepdims=True)
    rest = jnp.where(lane == i1, -jnp.inf, logits)
    mx2 = jnp.max(rest, axis=1, keepdims=True)
    i2 = jnp.min(jnp.where(rest == mx2, lane, LANE), axis=1, keepdims=True)
    e2 = jnp.exp(mx2 - mx1)
    g1 = 1.0 / (1.0 + e2)
    g_ref[...] = jnp.where(lane == 0, g1, jnp.where(lane == 1, e2 * g1, 0.0))
    e_ref[...] = jnp.where(lane == 0, i1, jnp.where(lane == 1, i2, 0))


def _router(x, nw, rw, rb):
    n, d = x.shape
    tm = _row_tile(n, 512)
    rows = lambda c: pl.BlockSpec((tm, c), lambda i: (i, 0))
    full = lambda a: pl.BlockSpec(a.shape, lambda i: (0, 0))
    return pl.pallas_call(
        _router_kernel,
        grid=(n // tm,),
        in_specs=[rows(d), full(nw), full(rw), full(rb)],
        out_specs=[pl.BlockSpec((tm * (d // 2 // LANE), LANE), lambda i: (i, 0)), rows(LANE), rows(LANE)],
        out_shape=[jax.ShapeDtypeStruct((n * (d // 2 // LANE), LANE), jnp.uint32),
                   jax.ShapeDtypeStruct((n, LANE), F32),
                   jax.ShapeDtypeStruct((n, LANE), I32)],
        compiler_params=_params("parallel"),
        name="router",
    )(x, nw, rw, rb)


def _unpack_rows(words):
    lo = pltpu.bitcast(jnp.left_shift(words, jnp.uint32(16)), F32)
    hi = pltpu.bitcast(jnp.bitwise_and(words, jnp.uint32(0xFFFF0000)), F32)
    return jnp.concatenate([lo, hi], axis=1).astype(BF16)


def _gmm_kernel(te_ref, src_ref, nused_ref, xp_hbm, w1_ref, w3_ref, w2_ref, o_ref, xbuf, sem):
    i = pl.program_id(0)
    ntiles = pl.num_programs(0)
    d = w1_ref.shape[0]
    oc = d // LANE
    tm = o_ref.shape[0] // oc
    nc = xbuf.shape[1] // tm
    slot = i % 2

    def issue(tile, sl):
        def body(g, carry):
            for u in range(ISSUE_UNROLL):
                r = g * ISSUE_UNROLL + u
                tok = src_ref[tile * tm + r]
                pltpu.make_async_copy(xp_hbm.at[pl.ds(pl.multiple_of(tok * nc, nc), nc), :],
                                      xbuf.at[sl, pl.ds(pl.multiple_of(r * nc, nc), nc), :],
                                      sem.at[sl]).start(priority=u % 2)
            return carry
        lax.fori_loop(0, tm // ISSUE_UNROLL, body, 0)

    @pl.when(i == 0)
    def _():
        issue(0, 0)

    @pl.when(i + 1 < ntiles)
    def _():
        issue(i + 1, 1 - slot)

    pltpu.make_async_copy(xbuf.at[slot], xbuf.at[slot], sem.at[slot]).wait()

    @pl.when(i < nused_ref[0])
    def _():
        words = jnp.concatenate([xbuf[slot, pl.ds(c, tm, stride=nc), :] for c in range(nc)], axis=1)
        acc = _swiglu_full(_unpack_rows(words), w1_ref, w3_ref, w2_ref)
        for c in range(oc):
            o_ref[pl.ds(c, tm, stride=oc), :] = acc[:, c * LANE:(c + 1) * LANE]

    @pl.when(i >= nused_ref[0])
    def _():
        o_ref[...] = jnp.zeros(o_ref.shape, F32)


def _gmm(xp, tile_expert, src, nused, w1, w3, w2, tm):
    ncap = src.shape[0]
    d = w1.shape[1]
    half = d // 2
    d_ff = w1.shape[2]
    oc = d // LANE
    expert = lambda r, c: pl.BlockSpec((None, r, c), lambda i, te, s, nu: (te[i], 0, 0))
    grid_spec = pltpu.PrefetchScalarGridSpec(
        num_scalar_prefetch=3,
        grid=(ncap // tm,),
        in_specs=[pl.BlockSpec(memory_space=pl.ANY), expert(d, d_ff), expert(d, d_ff), expert(d_ff, d)],
        out_specs=pl.BlockSpec((tm * oc, LANE), lambda i, te, s, nu: (i, 0)),
        scratch_shapes=[pltpu.VMEM((2, tm * (half // LANE), LANE), jnp.uint32),
                        pltpu.SemaphoreType.DMA((2,))])
    return pl.pallas_call(
        _gmm_kernel,
        grid_spec=grid_spec,
        out_shape=jax.ShapeDtypeStruct((ncap * oc, LANE), F32),
        compiler_params=_params("arbitrary"),
        name="moe_gmm",
    )(tile_expert, src, nused, xp, w1, w3, w2)


def _dispatch_plan(e_idx, n_experts, tm):
    n = e_idx.shape[0]
    ef = e_idx[:, :TOP_K].T.reshape(-1)
    onehot = (ef[:, None] == jnp.arange(n_experts, dtype=I32)[None, :]).astype(I32)
    rank = jnp.sum((jnp.cumsum(onehot, axis=0) - 1) * onehot, axis=1)
    counts = jnp.sum(onehot, axis=0)
    padded = ((counts + tm - 1) // tm) * tm
    ends = jnp.cumsum(padded)
    pos = (ends - padded)[ef] + rank
    ncap = TOP_K * n + n_experts * tm
    src = jnp.zeros((ncap,), I32).at[pos].set(jnp.arange(TOP_K * n, dtype=I32) % n,
                                              unique_indices=True, mode="promise_in_bounds")
    tile_start = jnp.arange(ncap // tm, dtype=I32) * tm
    tile_expert = jnp.minimum(jnp.searchsorted(ends, tile_start, side="right"), n_experts - 1).astype(I32)
    nused = (ends[-1] // tm).astype(I32).reshape(1)
    return pos.astype(I32), src, tile_expert, nused


def _combine_kernel(pos_ref, h_ref, g_ref, o_hbm, fw_ref, y_ref, obuf, sem, *, final_norm):
    i = pl.program_id(0)
    ntiles = pl.num_programs(0)
    tc, d = h_ref.shape
    oc = d // LANE
    n = ntiles * tc
    slot = i % 2

    def issue(tile, sl):
        def body(g, carry):
            for u in range(ISSUE_UNROLL):
                r = g * ISSUE_UNROLL + u
                for k in range(TOP_K):
                    p = pos_ref[k * n + tile * tc + r]
                    pltpu.make_async_copy(o_hbm.at[pl.ds(pl.multiple_of(p * oc, oc), oc), :],
                                          obuf.at[sl, k, pl.ds(pl.multiple_of(r * oc, oc), oc), :],
                                          sem.at[sl]).start(priority=k)
            return carry
        lax.fori_loop(0, tc // ISSUE_UNROLL, body, 0)

    @pl.when(i == 0)
    def _():
        issue(0, 0)

    @pl.when(i + 1 < ntiles)
    def _():
        issue(i + 1, 1 - slot)

    pltpu.make_async_copy(obuf.at[slot], obuf.at[slot], sem.at[slot]).wait()
    g = g_ref[...]
    rows = lambda k: jnp.concatenate([obuf[slot, k, pl.ds(c, tc, stride=oc), :] for c in range(oc)], axis=1)
    y = h_ref[...] + g[:, 0:1] * rows(0) + g[:, 1:2] * rows(1)
    if final_norm:
        ms = jnp.mean(y * y, axis=-1, keepdims=True)
        y = y * lax.rsqrt(ms + NORM_EPS) * fw_ref[...]
    y_ref[...] = y


def _combine(pos, h, gates, o_sorted, fw, final_norm):
    n, d = h.shape
    tc = _row_tile(n, 256)
    grid_spec = pltpu.PrefetchScalarGridSpec(
        num_scalar_prefetch=1,
        grid=(n // tc,),
        in_specs=[pl.BlockSpec((tc, d), lambda i, p: (i, 0)),
                  pl.BlockSpec((tc, LANE), lambda i, p: (i, 0)),
                  pl.BlockSpec(memory_space=pl.ANY),
                  pl.BlockSpec((1, d), lambda i, p: (0, 0))],
        out_specs=pl.BlockSpec((tc, d), lambda i, p: (i, 0)),
        scratch_shapes=[pltpu.VMEM((2, TOP_K, tc * (d // LANE), LANE), F32), pltpu.SemaphoreType.DMA((2,))])
    return pl.pallas_call(
        functools.partial(_combine_kernel, final_norm=final_norm),
        grid_spec=grid_spec,
        out_shape=jax.ShapeDtypeStruct((n, d), F32),
        compiler_params=_params("arbitrary"),
        name="moe_combine",
    )(pos, h, gates, o_sorted, fw)


def _moe_small_kernel(x_ref, nw_ref, g_ref, e_ref, w1_ref, w3_ref, w2_ref, fw_ref, y_ref,
                      xn_ref, acc_ref, *, final_norm):
    e = pl.program_id(0)
    j = pl.program_id(1)

    @pl.when(jnp.logical_and(e == 0, j == 0))
    def _():
        x = x_ref[...]
        ms = jnp.mean(x * x, axis=-1, keepdims=True)
        xn_ref[...] = (x * lax.rsqrt(ms + NORM_EPS) * nw_ref[...]).astype(BF16)
        acc_ref[...] = jnp.zeros(acc_ref.shape, F32)

    g = g_ref[...]
    idx = e_ref[...]
    gate = (jnp.where(idx[:, 0:1] == e, g[:, 0:1], 0.0) + jnp.where(idx[:, 1:2] == e, g[:, 1:2], 0.0))
    acc_ref[...] += gate * _swiglu_partial(xn_ref[...], w1_ref, w3_ref, w2_ref)

    @pl.when(jnp.logical_and(e == pl.num_programs(0) - 1, j == pl.num_programs(1) - 1))
    def _():
        y = x_ref[...] + acc_ref[...]
        if final_norm:
            ms = jnp.mean(y * y, axis=-1, keepdims=True)
            y = y * lax.rsqrt(ms + NORM_EPS) * fw_ref[...]
        y_ref[...] = y


def _moe_small(x, nw, gates, e_idx, w1, w3, w2, fw, final_norm):
    n, d = x.shape
    n_experts, _, d_ff = w1.shape
    tf = _ff_tile(d_ff)
    full = lambda a: pl.BlockSpec(a.shape, lambda e, j: (0, 0))
    return pl.pallas_call(
        functools.partial(_moe_small_kernel, final_norm=final_norm),
        grid=(n_experts, d_ff // tf),
        in_specs=[full(x), full(nw), full(gates), full(e_idx),
                  pl.BlockSpec((None, d, tf), lambda e, j: (e, 0, j)),
                  pl.BlockSpec((None, d, tf), lambda e, j: (e, 0, j)),
                  pl.BlockSpec((None, tf, d), lambda e, j: (e, j, 0)),
                  full(fw)],
        out_specs=full(x),
        out_shape=jax.ShapeDtypeStruct((n, d), F32),
        scratch_shapes=[pltpu.VMEM((n, d), BF16), pltpu.VMEM((n, d), F32)],
        compiler_params=_params("arbitrary", "arbitrary"),
        name="moe_small",
    )(x, nw, gates, e_idx, w1, w3, w2, fw)


def _norm_kernel(x_ref, w_ref, o_ref):
    x = x_ref[...]
    ms = jnp.mean(x * x, axis=-1, keepdims=True)
    o_ref[...] = x * lax.rsqrt(ms + NORM_EPS) * w_ref[...]


def _final_norm(x, w):
    n, d = x.shape
    tm = _row_tile(n, 512)
    return pl.pallas_call(
        _norm_kernel,
        grid=(n // tm,),
        in_specs=[pl.BlockSpec((tm, d), lambda i: (i, 0)), pl.BlockSpec((1, d), lambda i: (0, 0))],
        out_specs=pl.BlockSpec((tm, d), lambda i: (i, 0)),
        out_shape=jax.ShapeDtypeStruct((n, d), F32),
        compiler_params=_params("parallel"),
        name="final_norm",
    )(x, w)


def _smix_kernel(p_ref, pf_ref, pw_ref, ps_ref, su_ref, sv_ref, sw0_ref, sb0_ref,
                 q_ref, k_ref, v_ref, ig_ref, fg_ref, c_ref, n_ref, m_ref, mo_ref, nw_ref,
                 ypool_ref, ysgu_ref, yml_ref, cn_ref, nn_ref, mn_ref, *, cnt):
    p = p_ref[...]
    lane_w = _window_of_lane(p.shape, 1, 0)
    acc = p
    wsum = jnp.zeros(p.shape, F32)
    for off in range(1, POOL_WINDOWS[-1] + 1):
        if off in POOL_WINDOWS:
            wsum = jnp.where(lane_w == off, acc, wsum)
        if off <= POOL_BUF:
            acc = acc + pf_ref[HALO - off]
    count = jnp.minimum(lane_w, cnt).astype(F32)
    d = (wsum / count - p).astype(BF16)
    ypool_ref[...] = jnp.dot(d, pw_ref[...], preferred_element_type=F32) * ps_ref[...]

    ysgu_ref[...] = su_ref[...] * (sw0_ref[...] * sv_ref[...] + sb0_ref[...])

    q = q_ref[...]
    ks = k_ref[...] * (HEAD_DIM ** -0.5)
    v = v_ref[...]
    ig = ig_ref[...]
    c = c_ref[...]
    nrow = n_ref[...]
    a = _logsig(fg_ref[...]) + m_ref[...]
    mt = jnp.maximum(a, ig)
    inter = jnp.exp(a - mt)
    e_i = jnp.exp(ig - mt)
    s = jnp.sum(q * ks, axis=-1, keepdims=True) * e_i
    c_q = jnp.sum(c * q, axis=-1, keepdims=True)
    n_q = jnp.sum(nrow * q, axis=-1, keepdims=True)
    num = s * v + inter * c_q
    den = s + inter * n_q
    hval = num / jnp.maximum(jnp.abs(den), jnp.exp(-mt))
    cn_ref[...] = inter * c + (e_i * v) * ks
    nn_ref[...] = inter * nrow + e_i * ks
    mn_ref[...] = mt
    ms = jnp.mean(hval * hval, axis=1, keepdims=True)
    y = hval * lax.rsqrt(ms + NORM_EPS) * nw_ref[...]
    yml_ref[...] = jax.nn.sigmoid(mo_ref[...]) * y


def _smix(args, out_shapes, cnt):
    return pl.pallas_call(
        functools.partial(_smix_kernel, cnt=cnt),
        out_shape=out_shapes,
        compiler_params=pltpu.CompilerParams(vmem_limit_bytes=VMEM_LIMIT),
        name="sample_mixers",
    )(*args)


PAGES_PER_STEP = 32
PAGES_PER_BLOCK = MOBA_BLOCK // PAGE_SIZE


def _paged_kmean_kernel(pt_ref, *refs):
    o_ref = refs[-1]
    j = pl.program_id(1)
    nblk = PAGES_PER_STEP // PAGES_PER_BLOCK

    @pl.when(j == 0)
    def _():
        o_ref[...] = jnp.zeros(o_ref.shape, F32)

    lane = lax.broadcasted_iota(I32, o_ref.shape, 1)
    acc = o_ref[...]
    for bb in range(nblk):
        s = refs[bb * PAGES_PER_BLOCK][...]
        for pg in range(1, PAGES_PER_BLOCK):
            s = s + refs[bb * PAGES_PER_BLOCK + pg][...]
        col = jnp.sum(s, axis=1, keepdims=True) * (1.0 / MOBA_BLOCK)
        acc = jnp.where(lane == j * nblk + bb, col, acc)
    o_ref[...] = acc


def _paged_kmean(cache_t, page_table, page_base):
    bs, n_pages = page_table.shape
    nb = n_pages // PAGES_PER_BLOCK
    page = lambda s: pl.BlockSpec((None, GROUP_W, PAGE_SIZE),
                                  lambda b, j, pt, s=s: (page_base + pt[b, j * PAGES_PER_STEP + s], 0, 0))
    grid_spec = pltpu.PrefetchScalarGridSpec(
        num_scalar_prefetch=1,
        grid=(bs, n_pages // PAGES_PER_STEP),
        in_specs=[page(s) for s in range(PAGES_PER_STEP)],
        out_specs=pl.BlockSpec((None, GROUP_W, nb), lambda b, j, pt: (b, 0, 0)))
    return pl.pallas_call(
        _paged_kmean_kernel,
        grid_spec=grid_spec,
        out_shape=jax.ShapeDtypeStruct((bs, GROUP_W, nb), F32),
        compiler_params=_params("parallel", "arbitrary"),
        name="paged_kmean",
    )(page_table, *([cache_t] * PAGES_PER_STEP))


def _sample_select_kernel(q_ref, kmt_ref, o_ref):
    head_row = lax.broadcasted_iota(I32, (8, GROUP_W), 0) == _lane_head((8, GROUP_W), 1)
    q8 = jnp.where(head_row, q_ref[...], 0.0)
    s = jnp.dot(q8, kmt_ref[...], precision=HIGHEST, preferred_element_type=F32)
    blk = lax.broadcasted_iota(I32, s.shape, 1)
    slot = lax.broadcasted_iota(I32, o_ref.shape, 1)
    out = jnp.zeros(o_ref.shape, I32)
    for r in range(MOBA_TOPK):
        mx = jnp.max(s, axis=1, keepdims=True)
        idx = jnp.min(jnp.where(s == mx, blk, s.shape[1]), axis=1, keepdims=True)
        out = jnp.where(slot == r, idx, out)
        s = jnp.where(blk == idx, NEG_PICKED, s)
    o_ref[...] = out


def _sample_select(q3, kmean_t):
    bs, _, nb = kmean_t.shape
    return pl.pallas_call(
        _sample_select_kernel,
        grid=(bs,),
        in_specs=[pl.BlockSpec((None, 1, GROUP_W), lambda b: (b, 0, 0)),
                  pl.BlockSpec((None, GROUP_W, nb), lambda b: (b, 0, 0))],
        out_specs=pl.BlockSpec((None, 8, LANE), lambda b: (b, 0, 0)),
        out_shape=jax.ShapeDtypeStruct((bs, 8, LANE), I32),
        compiler_params=_params("parallel"),
        name="sample_select",
    )(q3, kmean_t)


N_SEL_PAGES = MOBA_TOPK * PAGES_PER_BLOCK


def _sample_attn_kernel(ph_ref, q_ref, kn_ref, vn_ref, *refs):
    kp, vp, o_ref = refs[:N_SEL_PAGES], refs[N_SEL_PAGES:2 * N_SEL_PAGES], refs[-1]
    q = q_ref[...] * (HEAD_DIM ** -0.5)
    own = jnp.sum(kn_ref[...] * q, axis=0, keepdims=True)
    logits = [jnp.sum(r[...] * q, axis=0, keepdims=True) for r in kp]
    m = own
    for lg in logits:
        m = jnp.maximum(m, jnp.max(lg, axis=1, keepdims=True))
    p_own = jnp.exp(own - m)
    den = p_own
    acc = p_own * vn_ref[...]
    for lg, vr in zip(logits, vp):
        p = jnp.exp(lg - m)
        den = den + jnp.sum(p, axis=1, keepdims=True)
        acc = acc + jnp.sum(vr[...] * p, axis=1, keepdims=True)
    o_ref[...] = acc / den


def _sample_attn(phys, q_col, kn_col, vn_col, cache_kt, cache_vt, page_base):
    bs = q_col.shape[0]
    col = pl.BlockSpec((None, HEAD_DIM, 1), lambda b, h, ph: (b, h, 0))
    page = lambda s: pl.BlockSpec(
        (None, HEAD_DIM, PAGE_SIZE),
        lambda b, h, ph, s=s: (page_base + ph[(b * N_HEADS + h) * N_SEL_PAGES + s], h, 0))
    pages = [page(s) for s in range(N_SEL_PAGES)]
    grid_spec = pltpu.PrefetchScalarGridSpec(
        num_scalar_prefetch=1,
        grid=(bs, N_HEADS),
        in_specs=[col, col, col] + pages + pages,
        out_specs=col)
    return pl.pallas_call(
        _sample_attn_kernel,
        grid_spec=grid_spec,
        out_shape=jax.ShapeDtypeStruct((bs, GROUP_W, 1), F32),
        compiler_params=_params("parallel", "parallel"),
        name="sample_attn",
    )(phys, q_col, kn_col, vn_col, *([cache_kt] * N_SEL_PAGES), *([cache_vt] * N_SEL_PAGES))


def _prep_layer(w_in, w_out, pool_w, pool_scale, sgu_w, sgu_b, norm_w):
    d = w_in.shape[0]
    g = GROUP_W
    gate0 = 7 * g
    att0 = gate0 + 2 * N_HEADS
    pad = jnp.zeros((d, GATE_W - 2 * N_HEADS), w_in.dtype)
    w_z = jnp.concatenate([w_in[:, :gate0], w_in[:, att0:], w_in[:, gate0:att0], pad], axis=1).astype(BF16)
    wbd = jnp.zeros((g, g), F32)
    for i in range(len(POOL_WINDOWS)):
        sl = slice(i * POOL_GROUP, (i + 1) * POOL_GROUP)
        wbd = wbd.at[sl, sl].set(pool_w[i])
    return dict(
        w_z=w_z, w_out=w_out.astype(BF16), pool_wbd=wbd.astype(BF16),
        pool_scale=pool_scale.reshape(1, g),
        sgu_w=sgu_w, sgu_bias=jnp.repeat(sgu_b.T, HEAD_DIM, axis=1),
        sgu_w0=jnp.repeat(sgu_w[:, 0, 0], HEAD_DIM).reshape(1, g),
        sgu_b0=jnp.repeat(sgu_b[:, 0], HEAD_DIM).reshape(1, g),
        norm_w=norm_w.reshape(1, g))


def _head_cols(a, bs):
    return a.reshape(bs * N_HEADS, HEAD_DIM, 1)


def _head_rows(a, bs):
    return a.reshape(bs * N_HEADS, 1, HEAD_DIM)


def _mixers_prompt(x, batch, seq, nmw, lp, gate_b, rope_tabs, layer, depth, knew, vnew):
    g = GROUP_W
    nb = seq // MOBA_BLOCK
    zm, q, gates, gates_t, kb, vt, kmean, knew, vnew = _inproj_prompt(
        x, nmw, lp["w_z"], rope_tabs, batch, seq, layer, depth, knew, vnew)
    m_ab = _poolsgu(zm, batch, seq, lp["pool_wbd"], lp["pool_scale"], lp["sgu_w"], lp["sgu_bias"])
    m_c, ct, nst, mst = _mlstm(zm, gates, gates_t, batch, seq, gate_b, lp["norm_w"])
    blocks = lambda a: a.reshape(batch, nb, MOBA_BLOCK, g)
    m_d = _moba(q, blocks(kb), blocks(vt), kmean, batch, nb)
    h = _outproj(x, m_ab, m_c, m_d, lp["w_out"])

    c_new = jnp.stack([ct[:, i * HEAD_DIM:(i + 1) * HEAD_DIM, i * HEAD_DIM:(i + 1) * HEAD_DIM]
                       for i in range(N_HEADS)], axis=1)
    state = (zm[:, :g].reshape(batch, seq, g)[:, seq - POOL_BUF:],
             jnp.swapaxes(c_new, -1, -2), nst.reshape(batch, N_HEADS, HEAD_DIM), mst[:, :N_HEADS, 0])
    return h, state, knew, vnew


def _mixers_sample(x, past_len, nmw, lp, gate_b, rope_tabs, pool_state, c_st, n_st, m_st,
                   cache_k, cache_v, page_table, page_base):
    bs = x.shape[0]
    g = GROUP_W
    z = _inproj(x, nmw, lp["w_z"], rope_tabs)
    col = lambda c: z[:, c * g:(c + 1) * g]
    p, sv = col(COL_P), col(COL_SV)
    gates = z[:, 10 * g:10 * g + 2 * N_HEADS] + gate_b.reshape(1, -1)
    bh = bs * N_HEADS
    prefix = jnp.concatenate([jnp.zeros((1, bs, g), F32), jnp.swapaxes(pool_state, 0, 1)], axis=0)
    args = (p, prefix, lp["pool_wbd"], lp["pool_scale"], col(COL_SU), sv, lp["sgu_w0"], lp["sgu_b0"],
            _head_rows(col(COL_MQ), bs), _head_rows(col(COL_MK), bs), _head_cols(col(COL_MV), bs),
            gates[:, :N_HEADS].reshape(bh, 1, 1), gates[:, N_HEADS:].reshape(bh, 1, 1),
            c_st.reshape(bh, HEAD_DIM, HEAD_DIM), n_st.reshape(bh, 1, HEAD_DIM), m_st.reshape(bh, 1, 1),
            _head_cols(col(COL_MO), bs), _head_cols(jnp.tile(lp["norm_w"], (bs, 1)), bs))
    sds = jax.ShapeDtypeStruct
    outs = (sds((bs, g), F32), sds((bs, g), F32), sds((bh, HEAD_DIM, 1), F32),
            sds((bh, HEAD_DIM, HEAD_DIM), F32), sds((bh, 1, HEAD_DIM), F32), sds((bh, 1, 1), F32))
    y_pool, y_sgu, y_ml, c_new, n_new, m_new = _smix(args, outs, min(past_len + 1, POOL_WINDOWS[-1]))

    q_col, k_col, v_col = (col(c).reshape(bs, g, 1) for c in (COL_AQ, COL_AK, COL_AV))
    kmean_t = _paged_kmean(cache_k, page_table, page_base)
    sel = _sample_select(col(COL_AQ).reshape(bs, 1, g), kmean_t)[:, :N_HEADS, :MOBA_TOPK]
    pages = sel[..., None] * PAGES_PER_BLOCK + jnp.arange(PAGES_PER_BLOCK, dtype=I32)
    phys = jnp.take_along_axis(page_table, pages.reshape(bs, -1), axis=1).reshape(-1)
    y_at = _sample_attn(phys, q_col, k_col, v_col, cache_k, cache_v, page_base).reshape(bs, g)

    m_ab = jnp.concatenate([y_pool, y_sgu], axis=1).astype(BF16)
    h = _outproj(x, m_ab, y_ml.reshape(bs, g).astype(BF16), y_at.astype(BF16), lp["w_out"])

    heads = lambda a: a.reshape(bs, 1, N_HEADS, HEAD_DIM)
    state = (heads(col(COL_AK)), heads(col(COL_AV)),
             jnp.concatenate([pool_state[:, 1:], p[:, None, :]], axis=1), sv.reshape(bs, 1, g),
             c_new.reshape(bs, N_HEADS, HEAD_DIM, HEAD_DIM), n_new.reshape(bs, N_HEADS, HEAD_DIM),
             m_new.reshape(bs, N_HEADS))
    return h, state


def kernel(x_prompt, x_sample, cache_k, cache_v, page_table, state_pool, state_mlstm_c, state_mlstm_n,
           state_mlstm_m, norm_mix_w, norm_ffn_w, final_norm_w, w_in, w_out, pool_w, pool_scale, sgu_w,
           sgu_b, mlstm_gate_b, mlstm_norm_w, ffn_w1, ffn_w3, ffn_w2, router_w, router_b,
           moe_w1, moe_w3, moe_w2):
    batch, seq, d = x_prompt.shape
    bs = x_sample.shape[0]
    depth = w_in.shape[0]
    n_pool = cache_k.shape[1]
    n_pages = page_table.shape[1]
    past_len = n_pages * PAGE_SIZE
    n_experts = router_w.shape[-1]
    assert x_sample.shape[1] == 1 and d == N_MIXERS * GROUP_W
    assert seq % MOBA_BLOCK == 0 and seq % (2 * MLSTM_CHUNK) == 0 and seq >= POOL_BUF
    assert past_len % MOBA_BLOCK == 0 and past_len // MOBA_BLOCK >= MOBA_TOPK
    assert n_pages % PAGES_PER_STEP == 0

    rope_p = _rope_tables(jnp.arange(seq, dtype=I32))
    rope_s = _rope_tables(jnp.full((bs,), past_len, I32))
    pages_t = lambda c: jnp.transpose(c, (0, 1, 3, 4, 2)).reshape(depth * n_pool, GROUP_W, PAGE_SIZE)
    ck, cv = pages_t(cache_k), pages_t(cache_v)
    fw = final_norm_w.reshape(1, d)
    tm_moe = 512

    hp = x_prompt.reshape(batch * seq, d)
    hs = x_sample.reshape(bs, d)
    st_p, st_s = [], []
    knew = jnp.zeros((depth, batch, GROUP_W, seq), F32)
    vnew = jnp.zeros((depth, batch, GROUP_W, seq), F32)
    for l in range(depth):
        lp = _prep_layer(w_in[l], w_out[l], pool_w[l], pool_scale[l], sgu_w[l], sgu_b[l], mlstm_norm_w[l])
        nmw = norm_mix_w[l].reshape(1, d)
        nfw = norm_ffn_w[l].reshape(1, d)
        hp, sp, knew, vnew = _mixers_prompt(hp, batch, seq, nmw, lp, mlstm_gate_b[l], rope_p,
                                            l, depth, knew, vnew)
        hs, ss = _mixers_sample(hs, past_len, nmw, lp, mlstm_gate_b[l], rope_s, state_pool[l],
                                state_mlstm_c[l], state_mlstm_n[l], state_mlstm_m[l],
                                ck, cv, page_table, l * n_pool)
        st_p.append(sp)
        st_s.append(ss)
        last = l == depth - 1
        j = l // 2
        if l % 2 == 0:
            w1, w3, w2 = ffn_w1[j].astype(BF16), ffn_w3[j].astype(BF16), ffn_w2[j].astype(BF16)
            hp = _ffn(hp, nfw, w1, w3, w2)
            hs = _ffn(hs, nfw, w1, w3, w2)
            if last:
                hp, hs = _final_norm(hp, fw), _final_norm(hs, fw)
        else:
            w1, w3, w2 = moe_w1[j].astype(BF16), moe_w3[j].astype(BF16), moe_w2[j].astype(BF16)
            rw = jnp.zeros((d, LANE), F32).at[:, :n_experts].set(router_w[j])
            rb = jnp.full((1, LANE), -jnp.inf, F32).at[0, :n_experts].set(router_b[j])
            xp, gates, e_idx = _router(hp, nfw, rw, rb)
            pos, src, tile_expert, nused = _dispatch_plan(e_idx, n_experts, tm_moe)
            o_sorted = _gmm(xp, tile_expert, src, nused, w1, w3, w2, tm_moe)
            hp = _combine(pos, hp, gates, o_sorted, fw, last)
            _, gates_s, e_s = _router(hs, nfw, rw, rb)
            hs = _moe_small(hs, nfw, gates_s, e_s, w1, w3, w2, fw, last)

    stack = lambda sts, i: jnp.stack([s[i] for s in sts])
    kv_leaf = lambda a: jnp.transpose(a.reshape(depth, batch, N_HEADS, HEAD_DIM, seq), (0, 1, 4, 2, 3))
    return (hp.reshape(batch, seq, d), hs.reshape(bs, 1, d),
            kv_leaf(knew), kv_leaf(vnew), stack(st_s, 0), stack(st_s, 1),
            stack(st_p, 0), stack(st_s, 2), stack(st_s, 3),
            stack(st_p, 1), stack(st_p, 2), stack(st_p, 3),
            stack(st_s, 4), stack(st_s, 5), stack(st_s, 6))
```

```python
import functools

import numpy as np
import jax
import jax.numpy as jnp
from jax import lax
from jax.experimental import pallas as pl
from jax.experimental.pallas import tpu as pltpu

F32 = jnp.float32
BF16 = jnp.bfloat16
I32 = jnp.int32
HIGHEST = lax.Precision.HIGHEST

N_MIXERS = 4
HEAD_DIM = 64
N_HEADS = 4
GROUP_W = N_HEADS * HEAD_DIM
POOL_WINDOWS = (2, 4, 8, 16)
POOL_GROUP = GROUP_W // len(POOL_WINDOWS)
POOL_BUF = max(POOL_WINDOWS) - 1
HALO = POOL_BUF + 1
SGU_CHUNK = 128
MLSTM_CHUNK = 64
MOBA_BLOCK = 256
MOBA_TOPK = 3
PAGE_SIZE = 128
ROPE_THETA = 500000.0
ROT_DIM = HEAD_DIM // 4
TOP_K = 2
NORM_EPS = 1e-6
NEG = -1e30
NEG_PICKED = -3e38

LANE = 128
GATE_W = LANE
Z_W = 10 * GROUP_W + GATE_W
Z_TN = 7 * LANE
COL_P, COL_SU, COL_SV, COL_MQ, COL_MK, COL_MV, COL_MO, COL_AQ, COL_AK, COL_AV = range(10)
COL_GATE = (10 * GROUP_W) // GATE_W
VMEM_LIMIT = 56 * 1024 * 1024
ISSUE_UNROLL = 8


def _params(*sem):
    return pltpu.CompilerParams(dimension_semantics=sem, vmem_limit_bytes=VMEM_LIMIT)


def _row_tile(n, pref):
    return pref if n % pref == 0 else n


def _lane_head(shape, dim):
    return lax.broadcasted_iota(I32, shape, dim) // HEAD_DIM


def _by_head(vals, lh):
    out = vals[N_HEADS - 1]
    for h in range(N_HEADS - 2, -1, -1):
        out = jnp.where(lh == h, vals[h], out)
    return out


def _dot_nt(a, b, precision=None):
    return lax.dot_general(a, b, (((1,), (1,)), ((), ())), precision=precision,
                           preferred_element_type=F32)


def _logsig(x):
    return jnp.minimum(x, 0.0) - jnp.log1p(jnp.exp(-jnp.abs(x)))


def _inproj_kernel(x_ref, nw_ref, w_ref, cos_ref, sa_ref, sb_ref, z_ref, xn_ref):
    j = pl.program_id(1)

    @pl.when(j == 0)
    def _():
        x = x_ref[...]
        ms = jnp.mean(x * x, axis=-1, keepdims=True)
        xn_ref[...] = (x * lax.rsqrt(ms + NORM_EPS) * nw_ref[...]).astype(BF16)

    z = jnp.dot(xn_ref[...], w_ref[...], preferred_element_type=F32)

    @pl.when(j < 2)
    def _():
        z_ref[...] = z

    @pl.when(j == 2)
    def _():
        c, sa, sb = cos_ref[...], sa_ref[...], sb_ref[...]
        half = ROT_DIM // 2
        for o in (0, GROUP_W):
            t = z[:, o:o + GROUP_W]
            z_ref[:, o:o + GROUP_W] = (t * c + pltpu.roll(t, GROUP_W - half, 1) * sa
                                       + pltpu.roll(t, half, 1) * sb)
        z_ref[:, 2 * GROUP_W:] = z[:, 2 * GROUP_W:]


def _inproj(x, nw, w, rope_tabs):
    n, d = x.shape
    tm = _row_tile(n, 512)
    tab_tiles = rope_tabs[0].shape[0] // tm
    tab_spec = pl.BlockSpec((tm, GROUP_W), lambda i, j: (i % tab_tiles, 0))
    return pl.pallas_call(
        _inproj_kernel,
        grid=(n // tm, Z_W // Z_TN),
        in_specs=[pl.BlockSpec((tm, d), lambda i, j: (i, 0)),
                  pl.BlockSpec((1, d), lambda i, j: (0, 0)),
                  pl.BlockSpec((d, Z_TN), lambda i, j: (0, j)),
                  tab_spec, tab_spec, tab_spec],
        out_specs=pl.BlockSpec((tm, Z_TN), lambda i, j: (i, j)),
        out_shape=jax.ShapeDtypeStruct((n, Z_W), F32),
        scratch_shapes=[pltpu.VMEM((tm, d), BF16)],
        compiler_params=_params("parallel", "arbitrary"),
        name="inproj",
    )(x, nw, w, *rope_tabs)


def _rope(t, c, sa, sb):
    half = ROT_DIM // 2
    return t * c + pltpu.roll(t, GROUP_W - half, 1) * sa + pltpu.roll(t, half, 1) * sb


def _inproj_prompt_kernel(x_ref, nw_ref, w_ref, cos_ref, sa_ref, sb_ref, kprev_hbm, vprev_hbm,
                          zm_ref, q_ref, g_ref, gt_ref, kb_ref, vt_ref, km_ref, knew_ref, vnew_ref,
                          *, tiles_per_seq):
    del kprev_hbm, vprev_hbm
    i = pl.program_id(0)
    tm = x_ref.shape[0]
    g = GROUP_W

    ti = i % tiles_per_seq

    @pl.when(ti == 0)
    def _():
        km_ref[...] = jnp.zeros(km_ref.shape, F32)

    x = x_ref[...]
    ms = jnp.mean(x * x, axis=-1, keepdims=True)
    xn = (x * lax.rsqrt(ms + NORM_EPS) * nw_ref[...]).astype(BF16)
    for jj in range(2):
        zm_ref[:, jj * Z_TN:(jj + 1) * Z_TN] = jnp.dot(xn, w_ref[:, jj * Z_TN:(jj + 1) * Z_TN],
                                                       preferred_element_type=F32)
    z = jnp.dot(xn, w_ref[:, 2 * Z_TN:], preferred_element_type=F32)

    c, sa, sb = cos_ref[...], sa_ref[...], sb_ref[...]
    q_ref[...] = _rope(z[:, 0:g], c, sa, sb)
    k = _rope(z[:, g:2 * g], c, sa, sb)
    v = z[:, 2 * g:3 * g]
    gates = z[:, 3 * g:]
    g_ref[...] = gates
    gt_ref[...] = gates.T[0:2 * N_HEADS, :]
    nblk = tm // MOBA_BLOCK
    kb_ref[...] = k.astype(BF16).reshape(nblk, MOBA_BLOCK, g)
    knew_ref[...] = k.T
    v_t = v.T
    vnew_ref[...] = v_t
    km = km_ref[...]
    row = lax.broadcasted_iota(I32, km.shape, 0)
    for b in range(nblk):
        cols = slice(b * MOBA_BLOCK, (b + 1) * MOBA_BLOCK)
        vt_ref[b] = v_t[:, cols].astype(BF16)
        mean = jnp.sum(k[cols, :], axis=0, keepdims=True) * (1.0 / MOBA_BLOCK)
        km = jnp.where(row == ti * nblk + b, mean, km)
    km_ref[...] = km


def _inproj_prompt(x, nw, w, rope_tabs, batch, seq, layer, depth, knew, vnew):
    n, d = x.shape
    tm = 2 * MOBA_BLOCK
    assert seq % tm == 0
    tps = seq // tm
    nb = seq // MOBA_BLOCK
    nblk = tm // MOBA_BLOCK
    g = GROUP_W
    tab_spec = pl.BlockSpec((tm, g), lambda i: (i % tps, 0))
    slab = pl.BlockSpec((None, None, g, tm), lambda i: (layer, i // tps, 0, i % tps))
    blocks = pl.BlockSpec((nblk, MOBA_BLOCK, g), lambda i: (i, 0, 0))
    rows = lambda c: pl.BlockSpec((tm, c), lambda i: (i, 0))
    sds = jax.ShapeDtypeStruct
    slab_shape = sds((depth, batch, g, seq), F32)
    any_spec = pl.BlockSpec(memory_space=pl.ANY)
    return pl.pallas_call(
        functools.partial(_inproj_prompt_kernel, tiles_per_seq=tps),
        grid=(n // tm,),
        in_specs=[rows(d),
                  pl.BlockSpec((1, d), lambda i: (0, 0)),
                  pl.BlockSpec((d, Z_W), lambda i: (0, 0)),
                  tab_spec, tab_spec, tab_spec, any_spec, any_spec],
        out_specs=[rows(2 * Z_TN), rows(g), rows(GATE_W),
                   pl.BlockSpec((2 * N_HEADS, tm), lambda i: (0, i)),
                   blocks, blocks,
                   pl.BlockSpec((None, nb, g), lambda i: (i // tps, 0, 0)),
                   slab, slab],
        out_shape=[sds((n, 2 * Z_TN), F32), sds((n, g), F32), sds((n, GATE_W), F32),
                   sds((2 * N_HEADS, n), F32),
                   sds((batch * nb, MOBA_BLOCK, g), BF16), sds((batch * nb, g, MOBA_BLOCK), BF16),
                   sds((batch, nb, g), F32), slab_shape, slab_shape],
        input_output_aliases={6: 7, 7: 8},
        compiler_params=_params("arbitrary"),
        name="inproj_prompt",
    )(x, nw, w, *rope_tabs, knew, vnew)


def _rope_tables(pos):
    half = ROT_DIM // 2
    inv = 1.0 / (ROPE_THETA ** (jnp.arange(half, dtype=F32) / half))
    ang = pos.astype(F32)[:, None] * inv[None, :]
    cos, sin = jnp.cos(ang), jnp.sin(ang)
    t = pos.shape[0]
    rest = HEAD_DIM - ROT_DIM
    one = jnp.ones((t, rest), F32)
    zero = jnp.zeros((t, rest), F32)
    zh = jnp.zeros((t, half), F32)
    c = jnp.concatenate([cos, cos, one], axis=1)
    sa = jnp.concatenate([-sin, zh, zero], axis=1)
    sb = jnp.concatenate([zh, sin, zero], axis=1)
    tile = lambda a: jnp.tile(a, (1, N_HEADS))
    return tile(c), tile(sa), tile(sb)


def _window_of_lane(shape, dim, lo):
    g = (lax.broadcasted_iota(I32, shape, dim) + lo) // POOL_GROUP
    w = jnp.full(shape, POOL_WINDOWS[-1], I32)
    for i in range(len(POOL_WINDOWS) - 2, -1, -1):
        w = jnp.where(g == i, POOL_WINDOWS[i], w)
    return w


def _poolsgu_kernel(p_ref, su_ref, sv_ref, pw_ref, ps_ref, sw_ref, sb_ref, o_ref, ebuf):
    t = pl.program_id(1)
    tt = p_ref.shape[0]

    @pl.when(t == 0)
    def _():
        ebuf[0:HALO, :] = jnp.zeros((HALO, GROUP_W), F32)

    @pl.when(t > 0)
    def _():
        ebuf[0:HALO, :] = ebuf[tt:tt + HALO, :]

    ebuf[HALO:HALO + tt, :] = p_ref[...]

    pos1 = lax.broadcasted_iota(I32, (tt, LANE), 0) + t * tt + 1
    halves = []
    for half in range(2):
        lo = half * LANE
        wsmall, wbig = POOL_WINDOWS[2 * half], POOL_WINDOWS[2 * half + 1]
        e0 = ebuf[HALO:HALO + tt, lo:lo + LANE]
        acc = e0
        for off in range(1, wsmall):
            acc = acc + ebuf[HALO - off:HALO - off + tt, lo:lo + LANE]
        small = acc
        for off in range(wsmall, wbig):
            acc = acc + ebuf[HALO - off:HALO - off + tt, lo:lo + LANE]
        first = lax.broadcasted_iota(I32, (tt, LANE), 1) < POOL_GROUP
        wsum = jnp.where(first, small, acc)
        cnt = jnp.minimum(pos1, _window_of_lane((tt, LANE), 1, lo)).astype(F32)
        halves.append(wsum / cnt - e0)
    d = jnp.concatenate(halves, axis=1).astype(BF16)
    y_pool = jnp.dot(d, pw_ref[...], preferred_element_type=F32) * ps_ref[...]
    o_ref[:, 0:GROUP_W] = y_pool.astype(o_ref.dtype)

    row = lax.broadcasted_iota(I32, (SGU_CHUNK, SGU_CHUNK), 0)
    col = lax.broadcasted_iota(I32, (SGU_CHUNK, SGU_CHUNK), 1)
    ws = [jnp.where(row >= col, sw_ref[h], 0.0).astype(BF16) for h in range(N_HEADS)]
    lh = _lane_head((SGU_CHUNK, GROUP_W), 1)
    for c in range(tt // SGU_CHUNK):
        rows = slice(c * SGU_CHUNK, (c + 1) * SGU_CHUNK)
        v = sv_ref[rows, :].astype(BF16)
        mixed = _by_head([jnp.dot(ws[h], v, preferred_element_type=F32) for h in range(N_HEADS)], lh)
        y = su_ref[rows, :] * (mixed + sb_ref[...])
        o_ref[rows, GROUP_W:2 * GROUP_W] = y.astype(o_ref.dtype)


def _poolsgu(z, batch, seq, pool_wbd, pool_scale, sgu_w, sgu_bias):
    tt = _row_tile(seq, 512)
    tps = seq // tt
    zspec = lambda c: pl.BlockSpec((tt, GROUP_W), lambda b, t, c=c: (b * tps + t, c))
    return pl.pallas_call(
        _poolsgu_kernel,
        grid=(batch, tps),
        in_specs=[zspec(COL_P), zspec(COL_SU), zspec(COL_SV),
                  pl.BlockSpec((GROUP_W, GROUP_W), lambda b, t: (0, 0)),
                  pl.BlockSpec((1, GROUP_W), lambda b, t: (0, 0)),
                  pl.BlockSpec((N_HEADS, SGU_CHUNK, SGU_CHUNK), lambda b, t: (0, 0, 0)),
                  pl.BlockSpec((SGU_CHUNK, GROUP_W), lambda b, t: (0, 0))],
        out_specs=pl.BlockSpec((tt, 2 * GROUP_W), lambda b, t: (b * tps + t, 0)),
        out_shape=jax.ShapeDtypeStruct((batch * seq, 2 * GROUP_W), BF16),
        scratch_shapes=[pltpu.VMEM((HALO + tt, GROUP_W), F32)],
        compiler_params=_params("parallel", "arbitrary"),
        name="pool_sgu",
    )(z, z, z, pool_wbd, pool_scale, sgu_w, sgu_bias)


def _mlstm_kernel(q_ref, k_ref, v_ref, o_ref, g_ref, *rest):
    L = MLSTM_CHUNK
    batch = q_ref.shape[0]
    gt_refs = rest[:batch]
    gbc_ref, gbr_ref, nw_ref, y_ref, ct_ref, n_ref, m_ref = rest[batch:]

    @pl.when(pl.program_id(0) == 0)
    def _():
        ct_ref[...] = jnp.zeros(ct_ref.shape, F32)
        n_ref[...] = jnp.zeros(n_ref.shape, F32)
        m_ref[...] = jnp.zeros(m_ref.shape, F32)

    r_i = lax.broadcasted_iota(I32, (L, L), 0)
    c_i = lax.broadcasted_iota(I32, (L, L), 1)
    causal = r_i >= c_i
    tri_l = causal.astype(F32)
    tri_u = (r_i <= c_i).astype(F32)
    lh = _lane_head((L, GROUP_W), 1)
    bd = _lane_head((GROUP_W, GROUP_W), 0) == _lane_head((GROUP_W, GROUP_W), 1)
    same_head = bd.astype(F32)

    ct_b = [ct_ref[b] for b in range(batch)]
    n_b = [n_ref[b] for b in range(batch)]
    m_b = [[m_ref[b, h:h + 1, 0:1] for h in range(N_HEADS)] for b in range(batch)]

    def chunk(c, b):
        ct, nrow, m_heads = ct_b[b], n_b[b], m_b[b]
        rows = slice(c * L, (c + 1) * L)
        q = q_ref[b, rows, :]
        ks = k_ref[b, rows, :] * (HEAD_DIM ** -0.5)
        v = v_ref[b, rows, :]
        g = g_ref[b, rows, :] + gbc_ref[...]
        gt = gt_refs[b][:, rows] + gbr_ref[...]
        bcol = jnp.dot(tri_l, _logsig(g), precision=HIGHEST, preferred_element_type=F32)
        brow = jnp.dot(_logsig(gt), tri_u, precision=HIGHEST, preferred_element_type=F32)
        qb, kb, vb = q.astype(BF16), ks.astype(BF16), v.astype(BF16)

        s_l, inter_l, deni_l, emt_l, wsrc_l, dec_l, mnew_l = [], [], [], [], [], [], []
        for h in range(N_HEADS):
            bc = bcol[:, N_HEADS + h:N_HEADS + h + 1]
            ic = g[:, h:h + 1]
            br = brow[N_HEADS + h:N_HEADS + h + 1, :]
            ir = gt[h:h + 1, :]
            dm = jnp.where(causal, bc - br + ir, NEG)
            a = bc + m_heads[h]
            mt = jnp.maximum(a, jnp.max(dm, axis=1, keepdims=True))
            qm = jnp.where(lh == h, q, 0.0).astype(BF16)
            s = _dot_nt(qm, kb) * jnp.exp(dm - mt)
            m_new = mt[L - 1:L, :]
            s_l.append(s)
            inter_l.append(jnp.exp(a - mt))
            deni_l.append(jnp.sum(s, axis=1, keepdims=True))
            emt_l.append(jnp.exp(-mt))
            wsrc_l.append(jnp.exp(bc[L - 1:L, :] - bc + ic - m_new))
            dec_l.append(jnp.exp(a[L - 1:L, :] - m_new))
            mnew_l.append(m_new)

        r = jnp.dot(jnp.concatenate(s_l, axis=0).astype(BF16), vb, preferred_element_type=F32)
        num_i = _by_head([r[h * L:(h + 1) * L, :] for h in range(N_HEADS)], lh)
        q_c = jnp.dot(qb, ct.astype(BF16), preferred_element_type=F32)
        q_n = jnp.dot(q * nrow, same_head, precision=HIGHEST, preferred_element_type=F32)
        inter = _by_head(inter_l, lh)
        num = num_i + inter * q_c
        den = _by_head(deni_l, lh) + inter * q_n
        hval = num / jnp.maximum(jnp.abs(den), _by_head(emt_l, lh))
        ms = jnp.dot(hval * hval, same_head, precision=HIGHEST,
                     preferred_element_type=F32) * (1.0 / HEAD_DIM)
        y = hval * lax.rsqrt(ms + NORM_EPS) * nw_ref[...]
        y_ref[b, rows, :] = (jax.nn.sigmoid(o_ref[b, rows, :]) * y).astype(y_ref.dtype)

        kw = ks * _by_head(wsrc_l, lh)
        dec = _by_head(dec_l, lh[0:1, :])
        upd = jnp.dot(kw.T.astype(BF16), vb, preferred_element_type=F32)
        ct_b[b] = ct * dec + jnp.where(bd, upd, 0.0)
        n_b[b] = nrow * dec + jnp.sum(kw, axis=0, keepdims=True)
        m_b[b] = mnew_l

    for c in range(q_ref.shape[1] // L):
        for b in range(batch):
            chunk(c, b)

    for b in range(batch):
        ct_ref[b] = ct_b[b]
        n_ref[b] = n_b[b]
        for h in range(N_HEADS):
            m_ref[b, h:h + 1, :] = jnp.broadcast_to(m_b[b][h], (1, m_ref.shape[2]))


def _mlstm(z, gates, gates_t, batch, seq, gate_b, norm_w):
    rows = _row_tile(seq, 2 * MLSTM_CHUNK)
    tps = seq // rows
    z3 = z.reshape(batch, seq, z.shape[1])
    zspec = lambda c: pl.BlockSpec((batch, rows, GROUP_W), lambda t, c=c: (0, t, c))
    gbc = jnp.zeros((1, GATE_W), F32).at[0, :2 * N_HEADS].set(gate_b.reshape(-1))
    gbr = gate_b.reshape(2 * N_HEADS, 1)
    state = lambda r, c: (pl.BlockSpec((batch, r, c), lambda t: (0, 0, 0)),
                          jax.ShapeDtypeStruct((batch, r, c), F32))
    (cs, csh), (ns, nsh), (msp, msh) = state(GROUP_W, GROUP_W), state(1, GROUP_W), state(8, LANE)
    gt_specs = [pl.BlockSpec((2 * N_HEADS, rows), lambda t, b=b: (0, b * tps + t)) for b in range(batch)]
    y, ct, nst, mst = pl.pallas_call(
        _mlstm_kernel,
        grid=(tps,),
        in_specs=[zspec(COL_MQ), zspec(COL_MK), zspec(COL_MV), zspec(COL_MO),
                  pl.BlockSpec((batch, rows, GATE_W), lambda t: (0, t, 0))] + gt_specs + [
                  pl.BlockSpec((1, GATE_W), lambda t: (0, 0)),
                  pl.BlockSpec((2 * N_HEADS, 1), lambda t: (0, 0)),
                  pl.BlockSpec((1, GROUP_W), lambda t: (0, 0))],
        out_specs=[pl.BlockSpec((batch, rows, GROUP_W), lambda t: (0, t, 0)), cs, ns, msp],
        out_shape=[jax.ShapeDtypeStruct((batch, seq, GROUP_W), BF16), csh, nsh, msh],
        compiler_params=_params("arbitrary"),
        name="mlstm",
    )(z3, z3, z3, z3, gates.reshape(batch, seq, GATE_W), *([gates_t] * batch), gbc, gbr, norm_w)
    return y.reshape(batch * seq, GROUP_W), ct, nst, mst


def _pick_top_blocks(s, n_valid_mask):
    blk = lax.broadcasted_iota(I32, s.shape, 0)
    s = jnp.where(n_valid_mask, s, NEG)
    bias = jnp.full(s.shape, NEG, F32)
    for _ in range(MOBA_TOPK):
        mx = jnp.max(s, axis=0, keepdims=True)
        idx = jnp.min(jnp.where(s == mx, blk, s.shape[0]), axis=0, keepdims=True)
        pick = blk == idx
        bias = jnp.where(jnp.logical_and(pick, mx > 0.5 * NEG), 0.0, bias)
        s = jnp.where(pick, NEG_PICKED, s)
    return bias


MOBA_KEY_CHUNK = 64
LOG2E = 1.4426950408889634


def _moba_logits(k_blk, qbd_sc, lt_sc, mx_sc, buf, key_le_query=None):
    B = MOBA_BLOCK
    lt = jnp.dot(k_blk[...], qbd_sc[...], preferred_element_type=F32)
    if key_le_query is not None:
        lt = jnp.where(jnp.concatenate([key_le_query] * N_HEADS, axis=1), lt, NEG)
    lt_sc[buf] = lt
    mx_sc[buf] = jnp.max(lt.reshape(B // 8, 8, N_HEADS * B), axis=0)


def _moba_softmax_pv(h, buf, vt_blk, lt_sc, mx_sc, p_sc, m_sc, l_sc, acc_sc, bias):
    B = MOBA_BLOCK
    ch = MOBA_KEY_CHUNK
    groups = ch // 8
    first = bias is None
    cols = slice(h * B, (h + 1) * B)
    cand = jnp.max(mx_sc[buf, :, cols], axis=0, keepdims=True)
    if first:
        m_new = shift = cand
    else:
        m_old = m_sc[h:h + 1, :]
        m_new = jnp.maximum(m_old, cand + bias)
        alpha = jnp.exp2(m_old - m_new)
        shift = m_new - bias
    lsum = None
    for c in range(B // ch):
        rows = slice(c * ch, (c + 1) * ch)
        p = jnp.exp2(lt_sc[buf, rows, cols] - shift)
        part = jnp.sum(p.reshape(groups, 8, B), axis=0)
        lsum = part if lsum is None else lsum + part
        p_sc[h, rows, :] = p.astype(BF16)
    lnew = jnp.sum(lsum, axis=0, keepdims=True)
    hs = slice(h * HEAD_DIM, (h + 1) * HEAD_DIM)
    pv = jnp.dot(vt_blk[hs, :], p_sc[h], preferred_element_type=F32)
    if first:
        l_sc[h:h + 1, :] = lnew
        acc_sc[hs, :] = pv
    else:
        l_sc[h:h + 1, :] = alpha * l_sc[h:h + 1, :] + lnew
        acc_sc[hs, :] = alpha * acc_sc[hs, :] + pv
    m_sc[h:h + 1, :] = m_new


def _moba_kernel(q_ref, kb_ref, vt_ref, km_ref, o_ref, qbd_sc, sel_sc, m_sc, l_sc, acc_sc, lt_sc, mx_sc, p_sc):
    i = pl.program_id(1)
    nb = km_ref.shape[0]
    nbp = sel_sc.shape[0] // N_HEADS
    B = MOBA_BLOCK
    q = q_ref[...]
    km = km_ref[...]
    lh = _lane_head((B, GROUP_W), 1)
    past = lax.broadcasted_iota(I32, (nb, B), 0) < i
    key_le_query = lax.broadcasted_iota(I32, (B, B), 0) <= lax.broadcasted_iota(I32, (B, B), 1)
    stats = (lt_sc, mx_sc, p_sc, m_sc, l_sc, acc_sc)

    q_t = q.T * (HEAD_DIM ** -0.5 * LOG2E)
    row_head = _lane_head((GROUP_W, B), 0)
    for h in range(N_HEADS):
        s_blk = _dot_nt(km, jnp.where(lh == h, q, 0.0), precision=HIGHEST)
        sel_sc[h * nbp:h * nbp + nb, :] = _pick_top_blocks(s_blk, past)
        qbd_sc[:, h * B:(h + 1) * B] = jnp.where(row_head == h, q_t, 0.0).astype(BF16)

    _moba_logits(kb_ref.at[i], qbd_sc, lt_sc, mx_sc, 1, key_le_query)
    _moba_logits(kb_ref.at[0], qbd_sc, lt_sc, mx_sc, 0)
    for h in range(N_HEADS):
        _moba_softmax_pv(h, 1, vt_ref.at[i], *stats, None)

    def step(n, buf):
        _moba_logits(kb_ref.at[jnp.minimum(n + 1, nb - 1)], qbd_sc, lt_sc, mx_sc, 1 - buf)
        for h in range(N_HEADS):
            bias = sel_sc[pl.ds(h * nbp + n, 1), :]
            _moba_softmax_pv(h, buf, vt_ref.at[n], *stats, bias)

    def body(k, carry):
        step(2 * k, 0)
        step(2 * k + 1, 1)
        return carry

    lax.fori_loop(0, (i + 1) // 2, body, 0)

    for h in range(N_HEADS):
        hs = slice(h * HEAD_DIM, (h + 1) * HEAD_DIM)
        acc_sc[hs, :] = acc_sc[hs, :] / l_sc[h:h + 1, :]
    o_ref[...] = acc_sc[...].T.astype(o_ref.dtype)


def _moba(q, kb4, vt4, kmean, batch, nb):
    nbp = -(-nb // 8) * 8
    B = MOBA_BLOCK
    blk4 = pl.BlockSpec((None, nb, B, GROUP_W), lambda b, i: (b, 0, 0, 0))
    return pl.pallas_call(
        _moba_kernel,
        grid=(batch, nb),
        in_specs=[pl.BlockSpec((B, GROUP_W), lambda b, i: (b * nb + i, 0)),
                  blk4, blk4,
                  pl.BlockSpec((None, nb, GROUP_W), lambda b, i: (b, 0, 0))],
        out_specs=pl.BlockSpec((B, GROUP_W), lambda b, i: (b * nb + i, 0)),
        out_shape=jax.ShapeDtypeStruct((batch * nb * B, GROUP_W), BF16),
        scratch_shapes=[pltpu.VMEM((GROUP_W, N_HEADS * B), BF16),
                        pltpu.VMEM((N_HEADS * nbp, B), F32),
                        pltpu.VMEM((8, B), F32), pltpu.VMEM((8, B), F32),
                        pltpu.VMEM((GROUP_W, B), F32),
                        pltpu.VMEM((2, B, N_HEADS * B), F32),
                        pltpu.VMEM((2, 8, N_HEADS * B), F32),
                        pltpu.VMEM((N_HEADS, B, B), BF16)],
        compiler_params=_params("parallel", "arbitrary"),
        name="moba",
    )(q, kb4, vt4, kmean)


def _outproj_kernel(x_ref, ma_ref, mc_ref, md_ref, w_ref, o_ref):
    g2 = 2 * GROUP_W
    acc = jnp.dot(ma_ref[...], w_ref[0:g2, :], preferred_element_type=F32)
    acc += jnp.dot(mc_ref[...], w_ref[g2:g2 + GROUP_W, :], preferred_element_type=F32)
    acc += jnp.dot(md_ref[...], w_ref[g2 + GROUP_W:, :], preferred_element_type=F32)
    o_ref[...] = x_ref[...] + acc


def _outproj(x, ma, mc, md, w):
    n, d = x.shape
    tm = _row_tile(n, 512)
    rows = lambda c: pl.BlockSpec((tm, c), lambda i: (i, 0))
    return pl.pallas_call(
        _outproj_kernel,
        grid=(n // tm,),
        in_specs=[rows(d), rows(2 * GROUP_W), rows(GROUP_W), rows(GROUP_W),
                  pl.BlockSpec(w.shape, lambda i: (0, 0))],
        out_specs=rows(d),
        out_shape=jax.ShapeDtypeStruct((n, d), F32),
        compiler_params=_params("parallel"),
        name="outproj",
    )(x, ma, mc, md, w)


def _ff_tile(d_ff):
    return d_ff // 2 if (d_ff // 2) % LANE == 0 else d_ff


def _swiglu_partial(xn, w1_ref, w3_ref, w2_ref):
    a = jnp.dot(xn, w1_ref[...], preferred_element_type=F32)
    b = jnp.dot(xn, w3_ref[...], preferred_element_type=F32)
    hmid = (a * jax.nn.sigmoid(a) * b).astype(BF16)
    return jnp.dot(hmid, w2_ref[...], preferred_element_type=F32)


def _swiglu_full(xn, w1_ref, w3_ref, w2_ref):
    d_ff = w1_ref.shape[1]
    tf = _ff_tile(d_ff)
    acc = None
    for jj in range(d_ff // tf):
        sl = slice(jj * tf, (jj + 1) * tf)
        part = _swiglu_partial(xn, w1_ref.at[:, sl], w3_ref.at[:, sl], w2_ref.at[sl, :])
        acc = part if acc is None else acc + part
    return acc


def _ffn_kernel(x_ref, nw_ref, w1_ref, w3_ref, w2_ref, o_ref):
    x = x_ref[...]
    ms = jnp.mean(x * x, axis=-1, keepdims=True)
    xn = (x * lax.rsqrt(ms + NORM_EPS) * nw_ref[...]).astype(BF16)
    o_ref[...] = x + _swiglu_full(xn, w1_ref, w3_ref, w2_ref)


def _ffn(x, nw, w1, w3, w2):
    n, d = x.shape
    tm = _row_tile(n, 512)
    resident = lambda a: pl.BlockSpec(a.shape, lambda i: (0, 0), pipeline_mode=pl.Buffered(1))
    return pl.pallas_call(
        _ffn_kernel,
        grid=(n // tm,),
        in_specs=[pl.BlockSpec((tm, d), lambda i: (i, 0)),
                  pl.BlockSpec((1, d), lambda i: (0, 0)),
                  resident(w1), resident(w3), resident(w2)],
        out_specs=pl.BlockSpec((tm, d), lambda i: (i, 0)),
        out_shape=jax.ShapeDtypeStruct((n, d), F32),
        compiler_params=_params("parallel"),
        name="ffn_dense",
    )(x, nw, w1, w3, w2)


def _router_kernel(x_ref, nw_ref, rw_ref, rb_ref, xp_ref, g_ref, e_ref):
    x = x_ref[...]
    ms = jnp.mean(x * x, axis=-1, keepdims=True)
    xn = x * lax.rsqrt(ms + NORM_EPS) * nw_ref[...]
    tm, d = x.shape
    half = d // 2
    bits = pltpu.bitcast(xn.astype(BF16).astype(F32), jnp.uint32)
    packed = jnp.bitwise_or(jnp.right_shift(bits[:, :half], jnp.uint32(16)), bits[:, half:])
    nc = half // LANE
    for c in range(nc):
        xp_ref[pl.ds(c, tm, stride=nc), :] = packed[:, c * LANE:(c + 1) * LANE]

    logits = jnp.dot(xn, rw_ref[...], precision=HIGHEST, preferred_element_type=F32) + rb_ref[...]
    lane = lax.broadcasted_iota(I32, logits.shape, 1)
    mx1 = jnp.max(logits, axis=1, keepdims=True)
    i1 = jnp.min(jnp.where(logits == mx1, lane, LANE), axis=1, keepdims=True)
    rest = jnp.where(lane == i1, -jnp.inf, logits)
    mx2 = jnp.max(rest, axis=1, keepdims=True)
    i2 = jnp.min(jnp.where(rest == mx2, lane, LANE), axis=1, keepdims=True)
    e2 = jnp.exp(mx2 - mx1)
    g1 = 1.0 / (1.0 + e2)
    g_ref[...] = jnp.where(lane == 0, g1, jnp.where(lane == 1, e2 * g1, 0.0))
    e_ref[...] = jnp.where(lane == 0, i1, jnp.where(lane == 1, i2, 0))


def _router(x, nw, rw, rb):
    n, d = x.shape
    tm = _row_tile(n, 512)
    rows = lambda c: pl.BlockSpec((tm, c), lambda i: (i, 0))
    full = lambda a: pl.BlockSpec(a.shape, lambda i: (0, 0))
    return pl.pallas_call(
        _router_kernel,
        grid=(n // tm,),
        in_specs=[rows(d), full(nw), full(rw), full(rb)],
        out_specs=[pl.BlockSpec((tm * (d // 2 // LANE), LANE), lambda i: (i, 0)), rows(LANE), rows(LANE)],
        out_shape=[jax.ShapeDtypeStruct((n * (d // 2 // LANE), LANE), jnp.uint32),
                   jax.ShapeDtypeStruct((n, LANE), F32),
                   jax.ShapeDtypeStruct((n, LANE), I32)],
        compiler_params=_params("parallel"),
        name="router",
    )(x, nw, rw, rb)


def _unpack_rows(words):
    lo = pltpu.bitcast(jnp.left_shift(words, jnp.uint32(16)), F32)
    hi = pltpu.bitcast(jnp.bitwise_and(words, jnp.uint32(0xFFFF0000)), F32)
    return jnp.concatenate([lo, hi], axis=1).astype(BF16)


def _gmm_kernel(te_ref, src_ref, nused_ref, xp_hbm, w1_ref, w3_ref, w2_ref, o_ref, xbuf, sem):
    i = pl.program_id(0)
    ntiles = pl.num_programs(0)
    d = w1_ref.shape[0]
    oc = d // LANE
    tm = o_ref.shape[0] // oc
    nc = xbuf.shape[1] // tm
    slot = i % 2

    def issue(tile, sl):
        def body(g, carry):
            for u in range(ISSUE_UNROLL):
                r = g * ISSUE_UNROLL + u
                tok = src_ref[tile * tm + r]
                pltpu.make_async_copy(xp_hbm.at[pl.ds(pl.multiple_of(tok * nc, nc), nc), :],
                                      xbuf.at[sl, pl.ds(pl.multiple_of(r * nc, nc), nc), :],
                                      sem.at[sl]).start(priority=u % 2)
            return carry
        lax.fori_loop(0, tm // ISSUE_UNROLL, body, 0)

    @pl.when(i == 0)
    def _():
        issue(0, 0)

    @pl.when(i + 1 < ntiles)
    def _():
        issue(i + 1, 1 - slot)

    pltpu.make_async_copy(xbuf.at[slot], xbuf.at[slot], sem.at[slot]).wait()

    @pl.when(i < nused_ref[0])
    def _():
        words = jnp.concatenate([xbuf[slot, pl.ds(c, tm, stride=nc), :] for c in range(nc)], axis=1)
        acc = _swiglu_full(_unpack_rows(words), w1_ref, w3_ref, w2_ref)
        for c in range(oc):
            o_ref[pl.ds(c, tm, stride=oc), :] = acc[:, c * LANE:(c + 1) * LANE]

    @pl.when(i >= nused_ref[0])
    def _():
        o_ref[...] = jnp.zeros(o_ref.shape, F32)


def _gmm(xp, tile_expert, src, nused, w1, w3, w2, tm):
    ncap = src.shape[0]
    d = w1.shape[1]
    half = d // 2
    d_ff = w1.shape[2]
    oc = d // LANE
    expert = lambda r, c: pl.BlockSpec((None, r, c), lambda i, te, s, nu: (te[i], 0, 0))
    grid_spec = pltpu.PrefetchScalarGridSpec(
        num_scalar_prefetch=3,
        grid=(ncap // tm,),
        in_specs=[pl.BlockSpec(memory_space=pl.ANY), expert(d, d_ff), expert(d, d_ff), expert(d_ff, d)],
        out_specs=pl.BlockSpec((tm * oc, LANE), lambda i, te, s, nu: (i, 0)),
        scratch_shapes=[pltpu.VMEM((2, tm * (half // LANE), LANE), jnp.uint32),
                        pltpu.SemaphoreType.DMA((2,))])
    return pl.pallas_call(
        _gmm_kernel,
        grid_spec=grid_spec,
        out_shape=jax.ShapeDtypeStruct((ncap * oc, LANE), F32),
        compiler_params=_params("arbitrary"),
        name="moe_gmm",
    )(tile_expert, src, nused, xp, w1, w3, w2)


def _dispatch_plan(e_idx, n_experts, tm):
    n = e_idx.shape[0]
    ef = e_idx[:, :TOP_K].T.reshape(-1)
    onehot = (ef[:, None] == jnp.arange(n_experts, dtype=I32)[None, :]).astype(I32)
    rank = jnp.sum((jnp.cumsum(onehot, axis=0) - 1) * onehot, axis=1)
    counts = jnp.sum(onehot, axis=0)
    padded = ((counts + tm - 1) // tm) * tm
    ends = jnp.cumsum(padded)
    pos = (ends - padded)[ef] + rank
    ncap = TOP_K * n + n_experts * tm
    src = jnp.zeros((ncap,), I32).at[pos].set(jnp.arange(TOP_K * n, dtype=I32) % n,
                                              unique_indices=True, mode="promise_in_bounds")
    tile_start = jnp.arange(ncap // tm, dtype=I32) * tm
    tile_expert = jnp.minimum(jnp.searchsorted(ends, tile_start, side="right"), n_experts - 1).astype(I32)
    nused = (ends[-1] // tm).astype(I32).reshape(1)
    return pos.astype(I32), src, tile_expert, nused


def _combine_kernel(pos_ref, h_ref, g_ref, o_hbm, fw_ref, y_ref, obuf, sem, *, final_norm):
    i = pl.program_id(0)
    ntiles = pl.num_programs(0)
    tc, d = h_ref.shape
    oc = d // LANE
    n = ntiles * tc
    slot = i % 2

    def issue(tile, sl):
        def body(g, carry):
            for u in range(ISSUE_UNROLL):
                r = g * ISSUE_UNROLL + u
                for k in range(TOP_K):
                    p = pos_ref[k * n + tile * tc + r]
                    pltpu.make_async_copy(o_hbm.at[pl.ds(pl.multiple_of(p * oc, oc), oc), :],
                                          obuf.at[sl, k, pl.ds(pl.multiple_of(r * oc, oc), oc), :],
                                          sem.at[sl]).start(priority=k)
            return carry
        lax.fori_loop(0, tc // ISSUE_UNROLL, body, 0)

    @pl.when(i == 0)
    def _():
        issue(0, 0)

    @pl.when(i + 1 < ntiles)
    def _():
        issue(i + 1, 1 - slot)

    pltpu.make_async_copy(obuf.at[slot], obuf.at[slot], sem.at[slot]).wait()
    g = g_ref[...]
    rows = lambda k: jnp.concatenate([obuf[slot, k, pl.ds(c, tc, stride=oc), :] for c in range(oc)], axis=1)
    y = h_ref[...] + g[:, 0:1] * rows(0) + g[:, 1:2] * rows(1)
    if final_norm:
        ms = jnp.mean(y * y, axis=-1, keepdims=True)
        y = y * lax.rsqrt(ms + NORM_EPS) * fw_ref[...]
    y_ref[...] = y


def _combine(pos, h, gates, o_sorted, fw, final_norm):
    n, d = h.shape
    tc = _row_tile(n, 256)
    grid_spec = pltpu.PrefetchScalarGridSpec(
        num_scalar_prefetch=1,
        grid=(n // tc,),
        in_specs=[pl.BlockSpec((tc, d), lambda i, p: (i, 0)),
                  pl.BlockSpec((tc, LANE), lambda i, p: (i, 0)),
                  pl.BlockSpec(memory_space=pl.ANY),
                  pl.BlockSpec((1, d), lambda i, p: (0, 0))],
        out_specs=pl.BlockSpec((tc, d), lambda i, p: (i, 0)),
        scratch_shapes=[pltpu.VMEM((2, TOP_K, tc * (d // LANE), LANE), F32), pltpu.SemaphoreType.DMA((2,))])
    return pl.pallas_call(
        functools.partial(_combine_kernel, final_norm=final_norm),
        grid_spec=grid_spec,
        out_shape=jax.ShapeDtypeStruct((n, d), F32),
        compiler_params=_params("arbitrary"),
        name="moe_combine",
    )(pos, h, gates, o_sorted, fw)


def _moe_small_kernel(x_ref, nw_ref, g_ref, e_ref, w1_ref, w3_ref, w2_ref, fw_ref, y_ref,
                      xn_ref, acc_ref, *, final_norm):
    e = pl.program_id(0)
    j = pl.program_id(1)

    @pl.when(jnp.logical_and(e == 0, j == 0))
    def _():
        x = x_ref[...]
        ms = jnp.mean(x * x, axis=-1, keepdims=True)
        xn_ref[...] = (x * lax.rsqrt(ms + NORM_EPS) * nw_ref[...]).astype(BF16)
        acc_ref[...] = jnp.zeros(acc_ref.shape, F32)

    g = g_ref[...]
    idx = e_ref[...]
    gate = (jnp.where(idx[:, 0:1] == e, g[:, 0:1], 0.0) + jnp.where(idx[:, 1:2] == e, g[:, 1:2], 0.0))
    acc_ref[...] += gate * _swiglu_partial(xn_ref[...], w1_ref, w3_ref, w2_ref)

    @pl.when(jnp.logical_and(e == pl.num_programs(0) - 1, j == pl.num_programs(1) - 1))
    def _():
        y = x_ref[...] + acc_ref[...]
        if final_norm:
            ms = jnp.mean(y * y, axis=-1, keepdims=True)
            y = y * lax.rsqrt(ms + NORM_EPS) * fw_ref[...]
        y_ref[...] = y


def _moe_small(x, nw, gates, e_idx, w1, w3, w2, fw, final_norm):
    n, d = x.shape
    n_experts, _, d_ff = w1.shape
    tf = _ff_tile(d_ff)
    full = lambda a: pl.BlockSpec(a.shape, lambda e, j: (0, 0))
    return pl.pallas_call(
        functools.partial(_moe_small_kernel, final_norm=final_norm),
        grid=(n_experts, d_ff // tf),
        in_specs=[full(x), full(nw), full(gates), full(e_idx),
                  pl.BlockSpec((None, d, tf), lambda e, j: (e, 0, j)),
                  pl.BlockSpec((None, d, tf), lambda e, j: (e, 0, j)),
                  pl.BlockSpec((None, tf, d), lambda e, j: (e, j, 0)),
                  full(fw)],
        out_specs=full(x),
        out_shape=jax.ShapeDtypeStruct((n, d), F32),
        scratch_shapes=[pltpu.VMEM((n, d), BF16), pltpu.VMEM((n, d), F32)],
        compiler_params=_params("arbitrary", "arbitrary"),
        name="moe_small",
    )(x, nw, gates, e_idx, w1, w3, w2, fw)


def _norm_kernel(x_ref, w_ref, o_ref):
    x = x_ref[...]
    ms = jnp.mean(x * x, axis=-1, keepdims=True)
    o_ref[...] = x * lax.rsqrt(ms + NORM_EPS) * w_ref[...]


def _final_norm(x, w):
    n, d = x.shape
    tm = _row_tile(n, 512)
    return pl.pallas_call(
        _norm_kernel,
        grid=(n // tm,),
        in_specs=[pl.BlockSpec((tm, d), lambda i: (i, 0)), pl.BlockSpec((1, d), lambda i: (0, 0))],
        out_specs=pl.BlockSpec((tm, d), lambda i: (i, 0)),
        out_shape=jax.ShapeDtypeStruct((n, d), F32),
        compiler_params=_params("parallel"),
        name="final_norm",
    )(x, w)


def _smix_kernel(p_ref, pf_ref, pw_ref, ps_ref, su_ref, sv_ref, sw0_ref, sb0_ref,
                 q_ref, k_ref, v_ref, ig_ref, fg_ref, c_ref, n_ref, m_ref, mo_ref, nw_ref,
                 ypool_ref, ysgu_ref, yml_ref, cn_ref, nn_ref, mn_ref, *, cnt):
    p = p_ref[...]
    lane_w = _window_of_lane(p.shape, 1, 0)
    acc = p
    wsum = jnp.zeros(p.shape, F32)
    for off in range(1, POOL_WINDOWS[-1] + 1):
        if off in POOL_WINDOWS:
            wsum = jnp.where(lane_w == off, acc, wsum)
        if off <= POOL_BUF:
            acc = acc + pf_ref[HALO - off]
    count = jnp.minimum(lane_w, cnt).astype(F32)
    d = (wsum / count - p).astype(BF16)
    ypool_ref[...] = jnp.dot(d, pw_ref[...], preferred_element_type=F32) * ps_ref[...]

    ysgu_ref[...] = su_ref[...] * (sw0_ref[...] * sv_ref[...] + sb0_ref[...])

    q = q_ref[...]
    ks = k_ref[...] * (HEAD_DIM ** -0.5)
    v = v_ref[...]
    ig = ig_ref[...]
    c = c_ref[...]
    nrow = n_ref[...]
    a = _logsig(fg_ref[...]) + m_ref[...]
    mt = jnp.maximum(a, ig)
    inter = jnp.exp(a - mt)
    e_i = jnp.exp(ig - mt)
    s = jnp.sum(q * ks, axis=-1, keepdims=True) * e_i
    c_q = jnp.sum(c * q, axis=-1, keepdims=True)
    n_q = jnp.sum(nrow * q, axis=-1, keepdims=True)
    num = s * v + inter * c_q
    den = s + inter * n_q
    hval = num / jnp.maximum(jnp.abs(den), jnp.exp(-mt))
    cn_ref[...] = inter * c + (e_i * v) * ks
    nn_ref[...] = inter * nrow + e_i * ks
    mn_ref[...] = mt
    ms = jnp.mean(hval * hval, axis=1, keepdims=True)
    y = hval * lax.rsqrt(ms + NORM_EPS) * nw_ref[...]
    yml_ref[...] = jax.nn.sigmoid(mo_ref[...]) * y


def _smix(args, out_shapes, cnt):
    return pl.pallas_call(
        functools.partial(_smix_kernel, cnt=cnt),
        out_shape=out_shapes,
        compiler_params=pltpu.CompilerParams(vmem_limit_bytes=VMEM_LIMIT),
        name="sample_mixers",
    )(*args)


PAGES_PER_STEP = 32
PAGES_PER_BLOCK = MOBA_BLOCK // PAGE_SIZE


def _paged_kmean_kernel(pt_ref, *refs):
    o_ref = refs[-1]
    j = pl.program_id(1)
    nblk = PAGES_PER_STEP // PAGES_PER_BLOCK

    @pl.when(j == 0)
    def _():
        o_ref[...] = jnp.zeros(o_ref.shape, F32)

    lane = lax.broadcasted_iota(I32, o_ref.shape, 1)
    acc = o_ref[...]
    for bb in range(nblk):
        s = refs[bb * PAGES_PER_BLOCK][...]
        for pg in range(1, PAGES_PER_BLOCK):
            s = s + refs[bb * PAGES_PER_BLOCK + pg][...]
        col = jnp.sum(s, axis=1, keepdims=True) * (1.0 / MOBA_BLOCK)
        acc = jnp.where(lane == j * nblk + bb, col, acc)
    o_ref[...] = acc


def _paged_kmean(cache_t, page_table, page_base):
    bs, n_pages = page_table.shape
    nb = n_pages // PAGES_PER_BLOCK
    page = lambda s: pl.BlockSpec((None, GROUP_W, PAGE_SIZE),
                                  lambda b, j, pt, s=s: (page_base + pt[b, j * PAGES_PER_STEP + s], 0, 0))
    grid_spec = pltpu.PrefetchScalarGridSpec(
        num_scalar_prefetch=1,
        grid=(bs, n_pages // PAGES_PER_STEP),
        in_specs=[page(s) for s in range(PAGES_PER_STEP)],
        out_specs=pl.BlockSpec((None, GROUP_W, nb), lambda b, j, pt: (b, 0, 0)))
    return pl.pallas_call(
        _paged_kmean_kernel,
        grid_spec=grid_spec,
        out_shape=jax.ShapeDtypeStruct((bs, GROUP_W, nb), F32),
        compiler_params=_params("parallel", "arbitrary"),
        name="paged_kmean",
    )(page_table, *([cache_t] * PAGES_PER_STEP))


def _sample_select_kernel(q_ref, kmt_ref, o_ref):
    head_row = lax.broadcasted_iota(I32, (8, GROUP_W), 0) == _lane_head((8, GROUP_W), 1)
    q8 = jnp.where(head_row, q_ref[...], 0.0)
    s = jnp.dot(q8, kmt_ref[...], precision=HIGHEST, preferred_element_type=F32)
    blk = lax.broadcasted_iota(I32, s.shape, 1)
    slot = lax.broadcasted_iota(I32, o_ref.shape, 1)
    out = jnp.zeros(o_ref.shape, I32)
    for r in range(MOBA_TOPK):
        mx = jnp.max(s, axis=1, keepdims=True)
        idx = jnp.min(jnp.where(s == mx, blk, s.shape[1]), axis=1, keepdims=True)
        out = jnp.where(slot == r, idx, out)
        s = jnp.where(blk == idx, NEG_PICKED, s)
    o_ref[...] = out


def _sample_select(q3, kmean_t):
    bs, _, nb = kmean_t.shape
    return pl.pallas_call(
        _sample_select_kernel,
        grid=(bs,),
        in_specs=[pl.BlockSpec((None, 1, GROUP_W), lambda b: (b, 0, 0)),
                  pl.BlockSpec((None, GROUP_W, nb), lambda b: (b, 0, 0))],
        out_specs=pl.BlockSpec((None, 8, LANE), lambda b: (b, 0, 0)),
        out_shape=jax.ShapeDtypeStruct((bs, 8, LANE), I32),
        compiler_params=_params("parallel"),
        name="sample_select",
    )(q3, kmean_t)


N_SEL_PAGES = MOBA_TOPK * PAGES_PER_BLOCK


def _sample_attn_kernel(ph_ref, q_ref, kn_ref, vn_ref, *refs):
    n_pg = N_HEADS * N_SEL_PAGES
    kp, vp, o_ref = refs[:n_pg], refs[n_pg:2 * n_pg], refs[-1]
    for h in range(N_HEADS):
        hs = slice(h * HEAD_DIM, (h + 1) * HEAD_DIM)
        pages = slice(h * N_SEL_PAGES, (h + 1) * N_SEL_PAGES)
        q = q_ref[hs, :] * (HEAD_DIM ** -0.5)
        own = jnp.sum(kn_ref[hs, :] * q, axis=0, keepdims=True)
        logits = [jnp.sum(r[...] * q, axis=0, keepdims=True) for r in kp[pages]]
        m = own
        for lg in logits:
            m = jnp.maximum(m, jnp.max(lg, axis=1, keepdims=True))
        p_own = jnp.exp(own - m)
        den = p_own
        acc = p_own * vn_ref[hs, :]
        for lg, vr in zip(logits, vp[pages]):
            p = jnp.exp(lg - m)
            den = den + jnp.sum(p, axis=1, keepdims=True)
            acc = acc + jnp.sum(vr[...] * p, axis=1, keepdims=True)
        o_ref[hs, :] = acc / den


def _sample_attn(phys, q_col, kn_col, vn_col, cache_kt, cache_vt, page_base):
    bs = q_col.shape[0]
    col = pl.BlockSpec((None, GROUP_W, 1), lambda b, ph: (b, 0, 0))
    page = lambda h, s: pl.BlockSpec(
        (None, HEAD_DIM, PAGE_SIZE),
        lambda b, ph, h=h, s=s: (page_base + ph[(b * N_HEADS + h) * N_SEL_PAGES + s], h, 0))
    pages = [page(h, s) for h in range(N_HEADS) for s in range(N_SEL_PAGES)]
    grid_spec = pltpu.PrefetchScalarGridSpec(
        num_scalar_prefetch=1,
        grid=(bs,),
        in_specs=[col, col, col] + pages + pages,
        out_specs=col)
    return pl.pallas_call(
        _sample_attn_kernel,
        grid_spec=grid_spec,
        out_shape=jax.ShapeDtypeStruct((bs, GROUP_W, 1), F32),
        compiler_params=_params("parallel"),
        name="sample_attn",
    )(phys, q_col, kn_col, vn_col, *([cache_kt] * len(pages)), *([cache_vt] * len(pages)))


def _prep_layer(w_in, w_out, pool_w, pool_scale, sgu_w, sgu_b, norm_w):
    d = w_in.shape[0]
    g = GROUP_W
    gate0 = 7 * g
    att0 = gate0 + 2 * N_HEADS
    pad = jnp.zeros((d, GATE_W - 2 * N_HEADS), w_in.dtype)
    w_z = jnp.concatenate([w_in[:, :gate0], w_in[:, att0:], w_in[:, gate0:att0], pad], axis=1).astype(BF16)
    wbd = jnp.zeros((g, g), F32)
    for i in range(len(POOL_WINDOWS)):
        sl = slice(i * POOL_GROUP, (i + 1) * POOL_GROUP)
        wbd = wbd.at[sl, sl].set(pool_w[i])
    return dict(
        w_z=w_z, w_out=w_out.astype(BF16), pool_wbd=wbd.astype(BF16),
        pool_scale=pool_scale.reshape(1, g),
        sgu_w=sgu_w, sgu_bias=jnp.repeat(sgu_b.T, HEAD_DIM, axis=1),
        sgu_w0=jnp.repeat(sgu_w[:, 0, 0], HEAD_DIM).reshape(1, g),
        sgu_b0=jnp.repeat(sgu_b[:, 0], HEAD_DIM).reshape(1, g),
        norm_w=norm_w.reshape(1, g))


def _head_cols(a, bs):
    return a.reshape(bs * N_HEADS, HEAD_DIM, 1)


def _head_rows(a, bs):
    return a.reshape(bs * N_HEADS, 1, HEAD_DIM)


def _mixers_prompt(x, batch, seq, nmw, lp, gate_b, rope_tabs, layer, depth, knew, vnew):
    g = GROUP_W
    nb = seq // MOBA_BLOCK
    zm, q, gates, gates_t, kb, vt, kmean, knew, vnew = _inproj_prompt(
        x, nmw, lp["w_z"], rope_tabs, batch, seq, layer, depth, knew, vnew)
    m_ab = _poolsgu(zm, batch, seq, lp["pool_wbd"], lp["pool_scale"], lp["sgu_w"], lp["sgu_bias"])
    m_c, ct, nst, mst = _mlstm(zm, gates, gates_t, batch, seq, gate_b, lp["norm_w"])
    blocks = lambda a: a.reshape(batch, nb, MOBA_BLOCK, g)
    m_d = _moba(q, blocks(kb), blocks(vt), kmean, batch, nb)
    h = _outproj(x, m_ab, m_c, m_d, lp["w_out"])

    c_new = jnp.stack([ct[:, i * HEAD_DIM:(i + 1) * HEAD_DIM, i * HEAD_DIM:(i + 1) * HEAD_DIM]
                       for i in range(N_HEADS)], axis=1)
    state = (zm[:, :g].reshape(batch, seq, g)[:, seq - POOL_BUF:],
             jnp.swapaxes(c_new, -1, -2), nst.reshape(batch, N_HEADS, HEAD_DIM), mst[:, :N_HEADS, 0])
    return h, state, knew, vnew


def _mixers_sample(x, past_len, nmw, lp, gate_b, rope_tabs, pool_state, c_st, n_st, m_st,
                   cache_k, cache_v, page_table, page_base):
    bs = x.shape[0]
    g = GROUP_W
    z = _inproj(x, nmw, lp["w_z"], rope_tabs)
    col = lambda c: z[:, c * g:(c + 1) * g]
    p, sv = col(COL_P), col(COL_SV)
    gates = z[:, 10 * g:10 * g + 2 * N_HEADS] + gate_b.reshape(1, -1)
    bh = bs * N_HEADS
    prefix = jnp.concatenate([jnp.zeros((1, bs, g), F32), jnp.swapaxes(pool_state, 0, 1)], axis=0)
    args = (p, prefix, lp["pool_wbd"], lp["pool_scale"], col(COL_SU), sv, lp["sgu_w0"], lp["sgu_b0"],
            _head_rows(col(COL_MQ), bs), _head_rows(col(COL_MK), bs), _head_cols(col(COL_MV), bs),
            gates[:, :N_HEADS].reshape(bh, 1, 1), gates[:, N_HEADS:].reshape(bh, 1, 1),
            c_st.reshape(bh, HEAD_DIM, HEAD_DIM), n_st.reshape(bh, 1, HEAD_DIM), m_st.reshape(bh, 1, 1),
            _head_cols(col(COL_MO), bs), _head_cols(jnp.tile(lp["norm_w"], (bs, 1)), bs))
    sds = jax.ShapeDtypeStruct
    outs = (sds((bs, g), F32), sds((bs, g), F32), sds((bh, HEAD_DIM, 1), F32),
            sds((bh, HEAD_DIM, HEAD_DIM), F32), sds((bh, 1, HEAD_DIM), F32), sds((bh, 1, 1), F32))
    y_pool, y_sgu, y_ml, c_new, n_new, m_new = _smix(args, outs, min(past_len + 1, POOL_WINDOWS[-1]))

    q_col, k_col, v_col = (col(c).reshape(bs, g, 1) for c in (COL_AQ, COL_AK, COL_AV))
    kmean_t = _paged_kmean(cache_k, page_table, page_base)
    sel = _sample_select(col(COL_AQ).reshape(bs, 1, g), kmean_t)[:, :N_HEADS, :MOBA_TOPK]
    pages = sel[..., None] * PAGES_PER_BLOCK + jnp.arange(PAGES_PER_BLOCK, dtype=I32)
    phys = jnp.take_along_axis(page_table, pages.reshape(bs, -1), axis=1).reshape(-1)
    y_at = _sample_attn(phys, q_col, k_col, v_col, cache_k, cache_v, page_base).reshape(bs, g)

    m_ab = jnp.concatenate([y_pool, y_sgu], axis=1).astype(BF16)
    h = _outproj(x, m_ab, y_ml.reshape(bs, g).astype(BF16), y_at.astype(BF16), lp["w_out"])

    heads = lambda a: a.reshape(bs, 1, N_HEADS, HEAD_DIM)
    state = (heads(col(COL_AK)), heads(col(COL_AV)),
             jnp.concatenate([pool_state[:, 1:], p[:, None, :]], axis=1), sv.reshape(bs, 1, g),
             c_new.reshape(bs, N_HEADS, HEAD_DIM, HEAD_DIM), n_new.reshape(bs, N_HEADS, HEAD_DIM),
             m_new.reshape(bs, N_HEADS))
    return h, state


def kernel(x_prompt, x_sample, cache_k, cache_v, page_table, state_pool, state_mlstm_c, state_mlstm_n,
           state_mlstm_m, norm_mix_w, norm_ffn_w, final_norm_w, w_in, w_out, pool_w, pool_scale, sgu_w,
           sgu_b, mlstm_gate_b, mlstm_norm_w, ffn_w1, ffn_w3, ffn_w2, router_w, router_b,
           moe_w1, moe_w3, moe_w2):
    batch, seq, d = x_prompt.shape
    bs = x_sample.shape[0]
    depth = w_in.shape[0]
    n_pool = cache_k.shape[1]
    n_pages = page_table.shape[1]
    past_len = n_pages * PAGE_SIZE
    n_experts = router_w.shape[-1]
    assert x_sample.shape[1] == 1 and d == N_MIXERS * GROUP_W
    assert seq % MOBA_BLOCK == 0 and seq % (2 * MLSTM_CHUNK) == 0 and seq >= POOL_BUF
    assert past_len % MOBA_BLOCK == 0 and past_len // MOBA_BLOCK >= MOBA_TOPK
    assert n_pages % PAGES_PER_STEP == 0

    rope_p = _rope_tables(jnp.arange(seq, dtype=I32))
    rope_s = _rope_tables(jnp.full((bs,), past_len, I32))
    pages_t = lambda c: jnp.transpose(c, (0, 1, 3, 4, 2)).reshape(depth * n_pool, GROUP_W, PAGE_SIZE)
    ck, cv = pages_t(cache_k), pages_t(cache_v)
    fw = final_norm_w.reshape(1, d)
    tm_moe = 512

    hp = x_prompt.reshape(batch * seq, d)
    hs = x_sample.reshape(bs, d)
    st_p, st_s = [], []
    knew = jnp.zeros((depth, batch, GROUP_W, seq), F32)
    vnew = jnp.zeros((depth, batch, GROUP_W, seq), F32)
    for l in range(depth):
        lp = _prep_layer(w_in[l], w_out[l], pool_w[l], pool_scale[l], sgu_w[l], sgu_b[l], mlstm_norm_w[l])
        nmw = norm_mix_w[l].reshape(1, d)
        nfw = norm_ffn_w[l].reshape(1, d)
        hp, sp, knew, vnew = _mixers_prompt(hp, batch, seq, nmw, lp, mlstm_gate_b[l], rope_p,
                                            l, depth, knew, vnew)
        hs, ss = _mixers_sample(hs, past_len, nmw, lp, mlstm_gate_b[l], rope_s, state_pool[l],
                                state_mlstm_c[l], state_mlstm_n[l], state_mlstm_m[l],
                                ck, cv, page_table, l * n_pool)
        st_p.append(sp)
        st_s.append(ss)
        last = l == depth - 1
        j = l // 2
        if l % 2 == 0:
            w1, w3, w2 = ffn_w1[j].astype(BF16), ffn_w3[j].astype(BF16), ffn_w2[j].astype(BF16)
            hp = _ffn(hp, nfw, w1, w3, w2)
            hs = _ffn(hs, nfw, w1, w3, w2)
            if last:
                hp, hs = _final_norm(hp, fw), _final_norm(hs, fw)
        else:
            w1, w3, w2 = moe_w1[j].astype(BF16), moe_w3[j].astype(BF16), moe_w2[j].astype(BF16)
            rw = jnp.zeros((d, LANE), F32).at[:, :n_experts].set(router_w[j])
            rb = jnp.full((1, LANE), -jnp.inf, F32).at[0, :n_experts].set(router_b[j])
            xp, gates, e_idx = _router(hp, nfw, rw, rb)
            pos, src, tile_expert, nused = _dispatch_plan(e_idx, n_experts, tm_moe)
            o_sorted = _gmm(xp, tile_expert, src, nused, w1, w3, w2, tm_moe)
            hp = _combine(pos, hp, gates, o_sorted, fw, last)
            _, gates_s, e_s = _router(hs, nfw, rw, rb)
            hs = _moe_small(hs, nfw, gates_s, e_s, w1, w3, w2, fw, last)

    stack = lambda sts, i: jnp.stack([s[i] for s in sts])
    kv_leaf = lambda a: jnp.transpose(a.reshape(depth, batch, N_HEADS, HEAD_DIM, seq), (0, 1, 4, 2, 3))
    return (hp.reshape(batch, seq, d), hs.reshape(bs, 1, d),
            kv_leaf(knew), kv_leaf(vnew), stack(st_s, 0), stack(st_s, 1),
            stack(st_p, 0), stack(st_s, 2), stack(st_s, 3),
            stack(st_p, 1), stack(st_p, 2), stack(st_p, 3),
            stack(st_s, 4), stack(st_s, 5), stack(st_s, 6))
```

```python
import functools

import numpy as np
import jax
import jax.numpy as jnp
from jax import lax
from jax.experimental import pallas as pl
from jax.experimental.pallas import tpu as pltpu

F32 = jnp.float32
BF16 = jnp.bfloat16
I32 = jnp.int32
HIGHEST = lax.Precision.HIGHEST

N_MIXERS = 4
HEAD_DIM = 64
N_HEADS = 4
GROUP_W = N_HEADS * HEAD_DIM
POOL_WINDOWS = (2, 4, 8, 16)
POOL_GROUP = GROUP_W // len(POOL_WINDOWS)
POOL_BUF = max(POOL_WINDOWS) - 1
HALO = POOL_BUF + 1
SGU_CHUNK = 128
MLSTM_CHUNK = 64
MOBA_BLOCK = 256
MOBA_TOPK = 3
PAGE_SIZE = 128
ROPE_THETA = 500000.0
ROT_DIM = HEAD_DIM // 4
TOP_K = 2
NORM_EPS = 1e-6
NEG = -1e30
NEG_PICKED = -3e38

LANE = 128
GATE_W = LANE
Z_W = 10 * GROUP_W + GATE_W
Z_TN = 7 * LANE
COL_P, COL_SU, COL_SV, COL_MQ, COL_MK, COL_MV, COL_MO, COL_AQ, COL_AK, COL_AV = range(10)
COL_GATE = (10 * GROUP_W) // GATE_W
VMEM_LIMIT = 56 * 1024 * 1024
ISSUE_UNROLL = 8


def _params(*sem):
    return pltpu.CompilerParams(dimension_semantics=sem, vmem_limit_bytes=VMEM_LIMIT)


def _row_tile(n, pref):
    return pref if n % pref == 0 else n


def _lane_head(shape, dim):
    return lax.broadcasted_iota(I32, shape, dim) // HEAD_DIM


def _by_head(vals, lh):
    out = vals[N_HEADS - 1]
    for h in range(N_HEADS - 2, -1, -1):
        out = jnp.where(lh == h, vals[h], out)
    return out


def _dot_nt(a, b, precision=None):
    return lax.dot_general(a, b, (((1,), (1,)), ((), ())), precision=precision,
                           preferred_element_type=F32)


def _logsig(x):
    return jnp.minimum(x, 0.0) - jnp.log1p(jnp.exp(-jnp.abs(x)))


def _inproj_kernel(x_ref, nw_ref, w_ref, cos_ref, sa_ref, sb_ref, z_ref, xn_ref):
    j = pl.program_id(1)

    @pl.when(j == 0)
    def _():
        x = x_ref[...]
        ms = jnp.mean(x * x, axis=-1, keepdims=True)
        xn_ref[...] = (x * lax.rsqrt(ms + NORM_EPS) * nw_ref[...]).astype(BF16)

    z = jnp.dot(xn_ref[...], w_ref[...], preferred_element_type=F32)

    @pl.when(j < 2)
    def _():
        z_ref[...] = z

    @pl.when(j == 2)
    def _():
        c, sa, sb = cos_ref[...], sa_ref[...], sb_ref[...]
        half = ROT_DIM // 2
        for o in (0, GROUP_W):
            t = z[:, o:o + GROUP_W]
            z_ref[:, o:o + GROUP_W] = (t * c + pltpu.roll(t, GROUP_W - half, 1) * sa
                                       + pltpu.roll(t, half, 1) * sb)
        z_ref[:, 2 * GROUP_W:] = z[:, 2 * GROUP_W:]


def _inproj(x, nw, w, rope_tabs):
    n, d = x.shape
    tm = _row_tile(n, 512)
    tab_tiles = rope_tabs[0].shape[0] // tm
    tab_spec = pl.BlockSpec((tm, GROUP_W), lambda i, j: (i % tab_tiles, 0))
    return pl.pallas_call(
        _inproj_kernel,
        grid=(n // tm, Z_W // Z_TN),
        in_specs=[pl.BlockSpec((tm, d), lambda i, j: (i, 0)),
                  pl.BlockSpec((1, d), lambda i, j: (0, 0)),
                  pl.BlockSpec((d, Z_TN), lambda i, j: (0, j)),
                  tab_spec, tab_spec, tab_spec],
        out_specs=pl.BlockSpec((tm, Z_TN), lambda i, j: (i, j)),
        out_shape=jax.ShapeDtypeStruct((n, Z_W), F32),
        scratch_shapes=[pltpu.VMEM((tm, d), BF16)],
        compiler_params=_params("parallel", "arbitrary"),
        name="inproj",
    )(x, nw, w, *rope_tabs)


def _rope(t, c, sa, sb):
    half = ROT_DIM // 2
    return t * c + pltpu.roll(t, GROUP_W - half, 1) * sa + pltpu.roll(t, half, 1) * sb


def _inproj_prompt_kernel(x_ref, nw_ref, w_ref, cos_ref, sa_ref, sb_ref, kprev_hbm, vprev_hbm,
                          zm_ref, q_ref, g_ref, gt_ref, kb_ref, vt_ref, km_ref, knew_ref, vnew_ref,
                          *, tiles_per_seq):
    del kprev_hbm, vprev_hbm
    i = pl.program_id(0)
    tm = x_ref.shape[0]
    g = GROUP_W

    ti = i % tiles_per_seq

    @pl.when(ti == 0)
    def _():
        km_ref[...] = jnp.zeros(km_ref.shape, F32)

    x = x_ref[...]
    ms = jnp.mean(x * x, axis=-1, keepdims=True)
    xn = (x * lax.rsqrt(ms + NORM_EPS) * nw_ref[...]).astype(BF16)
    for jj in range(2):
        zm_ref[:, jj * Z_TN:(jj + 1) * Z_TN] = jnp.dot(xn, w_ref[:, jj * Z_TN:(jj + 1) * Z_TN],
                                                       preferred_element_type=F32)
    z = jnp.dot(xn, w_ref[:, 2 * Z_TN:], preferred_element_type=F32)

    c, sa, sb = cos_ref[...], sa_ref[...], sb_ref[...]
    q_ref[...] = _rope(z[:, 0:g], c, sa, sb)
    k = _rope(z[:, g:2 * g], c, sa, sb)
    v = z[:, 2 * g:3 * g]
    gates = z[:, 3 * g:]
    g_ref[...] = gates
    gt_ref[...] = gates.T[0:2 * N_HEADS, :]
    nblk = tm // MOBA_BLOCK
    kb_ref[...] = k.astype(BF16).reshape(nblk, MOBA_BLOCK, g)
    knew_ref[...] = k.T
    v_t = v.T
    vnew_ref[...] = v_t
    km = km_ref[...]
    row = lax.broadcasted_iota(I32, km.shape, 0)
    for b in range(nblk):
        cols = slice(b * MOBA_BLOCK, (b + 1) * MOBA_BLOCK)
        vt_ref[b] = v_t[:, cols].astype(BF16)
        mean = jnp.sum(k[cols, :], axis=0, keepdims=True) * (1.0 / MOBA_BLOCK)
        km = jnp.where(row == ti * nblk + b, mean, km)
    km_ref[...] = km


def _inproj_prompt(x, nw, w, rope_tabs, batch, seq, layer, depth, knew, vnew):
    n, d = x.shape
    tm = 2 * MOBA_BLOCK
    assert seq % tm == 0
    tps = seq // tm
    nb = seq // MOBA_BLOCK
    nblk = tm // MOBA_BLOCK
    g = GROUP_W
    tab_spec = pl.BlockSpec((tm, g), lambda i: (i % tps, 0))
    slab = pl.BlockSpec((None, None, g, tm), lambda i: (layer, i // tps, 0, i % tps))
    blocks = pl.BlockSpec((nblk, MOBA_BLOCK, g), lambda i: (i, 0, 0))
    rows = lambda c: pl.BlockSpec((tm, c), lambda i: (i, 0))
    sds = jax.ShapeDtypeStruct
    slab_shape = sds((depth, batch, g, seq), F32)
    any_spec = pl.BlockSpec(memory_space=pl.ANY)
    return pl.pallas_call(
        functools.partial(_inproj_prompt_kernel, tiles_per_seq=tps),
        grid=(n // tm,),
        in_specs=[rows(d),
                  pl.BlockSpec((1, d), lambda i: (0, 0)),
                  pl.BlockSpec((d, Z_W), lambda i: (0, 0)),
                  tab_spec, tab_spec, tab_spec, any_spec, any_spec],
        out_specs=[rows(2 * Z_TN), rows(g), rows(GATE_W),
                   pl.BlockSpec((2 * N_HEADS, tm), lambda i: (0, i)),
                   blocks, blocks,
                   pl.BlockSpec((None, nb, g), lambda i: (i // tps, 0, 0)),
                   slab, slab],
        out_shape=[sds((n, 2 * Z_TN), F32), sds((n, g), F32), sds((n, GATE_W), F32),
                   sds((2 * N_HEADS, n), F32),
                   sds((batch * nb, MOBA_BLOCK, g), BF16), sds((batch * nb, g, MOBA_BLOCK), BF16),
                   sds((batch, nb, g), F32), slab_shape, slab_shape],
        input_output_aliases={6: 7, 7: 8},
        compiler_params=_params("arbitrary"),
        name="inproj_prompt",
    )(x, nw, w, *rope_tabs, knew, vnew)


def _rope_tables(pos):
    half = ROT_DIM // 2
    inv = 1.0 / (ROPE_THETA ** (jnp.arange(half, dtype=F32) / half))
    ang = pos.astype(F32)[:, None] * inv[None, :]
    cos, sin = jnp.cos(ang), jnp.sin(ang)
    t = pos.shape[0]
    rest = HEAD_DIM - ROT_DIM
    one = jnp.ones((t, rest), F32)
    zero = jnp.zeros((t, rest), F32)
    zh = jnp.zeros((t, half), F32)
    c = jnp.concatenate([cos, cos, one], axis=1)
    sa = jnp.concatenate([-sin, zh, zero], axis=1)
    sb = jnp.concatenate([zh, sin, zero], axis=1)
    tile = lambda a: jnp.tile(a, (1, N_HEADS))
    return tile(c), tile(sa), tile(sb)


def _window_of_lane(shape, dim, lo):
    g = (lax.broadcasted_iota(I32, shape, dim) + lo) // POOL_GROUP
    w = jnp.full(shape, POOL_WINDOWS[-1], I32)
    for i in range(len(POOL_WINDOWS) - 2, -1, -1):
        w = jnp.where(g == i, POOL_WINDOWS[i], w)
    return w


def _poolsgu_kernel(p_ref, su_ref, sv_ref, pw_ref, ps_ref, sw_ref, sb_ref, o_ref, ebuf):
    t = pl.program_id(1)
    tt = p_ref.shape[0]

    @pl.when(t == 0)
    def _():
        ebuf[0:HALO, :] = jnp.zeros((HALO, GROUP_W), F32)

    @pl.when(t > 0)
    def _():
        ebuf[0:HALO, :] = ebuf[tt:tt + HALO, :]

    ebuf[HALO:HALO + tt, :] = p_ref[...]

    pos1 = lax.broadcasted_iota(I32, (tt, LANE), 0) + t * tt + 1
    halves = []
    for half in range(2):
        lo = half * LANE
        wsmall, wbig = POOL_WINDOWS[2 * half], POOL_WINDOWS[2 * half + 1]
        e0 = ebuf[HALO:HALO + tt, lo:lo + LANE]
        acc = e0
        for off in range(1, wsmall):
            acc = acc + ebuf[HALO - off:HALO - off + tt, lo:lo + LANE]
        small = acc
        for off in range(wsmall, wbig):
            acc = acc + ebuf[HALO - off:HALO - off + tt, lo:lo + LANE]
        first = lax.broadcasted_iota(I32, (tt, LANE), 1) < POOL_GROUP
        wsum = jnp.where(first, small, acc)
        cnt = jnp.minimum(pos1, _window_of_lane((tt, LANE), 1, lo)).astype(F32)
        halves.append(wsum / cnt - e0)
    d = jnp.concatenate(halves, axis=1).astype(BF16)
    y_pool = jnp.dot(d, pw_ref[...], preferred_element_type=F32) * ps_ref[...]
    o_ref[:, 0:GROUP_W] = y_pool.astype(o_ref.dtype)

    row = lax.broadcasted_iota(I32, (SGU_CHUNK, SGU_CHUNK), 0)
    col = lax.broadcasted_iota(I32, (SGU_CHUNK, SGU_CHUNK), 1)
    ws = [jnp.where(row >= col, sw_ref[h], 0.0).astype(BF16) for h in range(N_HEADS)]
    lh = _lane_head((SGU_CHUNK, GROUP_W), 1)
    for c in range(tt // SGU_CHUNK):
        rows = slice(c * SGU_CHUNK, (c + 1) * SGU_CHUNK)
        v = sv_ref[rows, :].astype(BF16)
        mixed = _by_head([jnp.dot(ws[h], v, preferred_element_type=F32) for h in range(N_HEADS)], lh)
        y = su_ref[rows, :] * (mixed + sb_ref[...])
        o_ref[rows, GROUP_W:2 * GROUP_W] = y.astype(o_ref.dtype)


def _poolsgu(z, batch, seq, pool_wbd, pool_scale, sgu_w, sgu_bias):
    tt = _row_tile(seq, 512)
    tps = seq // tt
    zspec = lambda c: pl.BlockSpec((tt, GROUP_W), lambda b, t, c=c: (b * tps + t, c))
    return pl.pallas_call(
        _poolsgu_kernel,
        grid=(batch, tps),
        in_specs=[zspec(COL_P), zspec(COL_SU), zspec(COL_SV),
                  pl.BlockSpec((GROUP_W, GROUP_W), lambda b, t: (0, 0)),
                  pl.BlockSpec((1, GROUP_W), lambda b, t: (0, 0)),
                  pl.BlockSpec((N_HEADS, SGU_CHUNK, SGU_CHUNK), lambda b, t: (0, 0, 0)),
                  pl.BlockSpec((SGU_CHUNK, GROUP_W), lambda b, t: (0, 0))],
        out_specs=pl.BlockSpec((tt, 2 * GROUP_W), lambda b, t: (b * tps + t, 0)),
        out_shape=jax.ShapeDtypeStruct((batch * seq, 2 * GROUP_W), BF16),
        scratch_shapes=[pltpu.VMEM((HALO + tt, GROUP_W), F32)],
        compiler_params=_params("parallel", "arbitrary"),
        name="pool_sgu",
    )(z, z, z, pool_wbd, pool_scale, sgu_w, sgu_bias)


def _mlstm_kernel(q_ref, k_ref, v_ref, o_ref, g_ref, *rest):
    L = MLSTM_CHUNK
    batch = q_ref.shape[0]
    gt_refs = rest[:batch]
    gbc_ref, gbr_ref, nw_ref, y_ref, ct_ref, n_ref, m_ref = rest[batch:]

    @pl.when(pl.program_id(0) == 0)
    def _():
        ct_ref[...] = jnp.zeros(ct_ref.shape, F32)
        n_ref[...] = jnp.zeros(n_ref.shape, F32)
        m_ref[...] = jnp.zeros(m_ref.shape, F32)

    r_i = lax.broadcasted_iota(I32, (L, L), 0)
    c_i = lax.broadcasted_iota(I32, (L, L), 1)
    causal = r_i >= c_i
    tri_l = causal.astype(F32)
    tri_u = (r_i <= c_i).astype(F32)
    lh = _lane_head((L, GROUP_W), 1)
    bd = _lane_head((GROUP_W, GROUP_W), 0) == _lane_head((GROUP_W, GROUP_W), 1)
    same_head = bd.astype(F32)

    ct_b = [ct_ref[b] for b in range(batch)]
    n_b = [n_ref[b] for b in range(batch)]
    m_b = [[m_ref[b, h:h + 1, 0:1] for h in range(N_HEADS)] for b in range(batch)]

    def chunk(c, b):
        ct, nrow, m_heads = ct_b[b], n_b[b], m_b[b]
        rows = slice(c * L, (c + 1) * L)
        q = q_ref[b, rows, :]
        ks = k_ref[b, rows, :] * (HEAD_DIM ** -0.5)
        v = v_ref[b, rows, :]
        g = g_ref[b, rows, :] + gbc_ref[...]
        gt = gt_refs[b][:, rows] + gbr_ref[...]
        bcol = jnp.dot(tri_l, _logsig(g), precision=HIGHEST, preferred_element_type=F32)
        brow = jnp.dot(_logsig(gt), tri_u, precision=HIGHEST, preferred_element_type=F32)
        qb, kb, vb = q.astype(BF16), ks.astype(BF16), v.astype(BF16)

        s_l, inter_l, deni_l, emt_l, wsrc_l, dec_l, mnew_l = [], [], [], [], [], [], []
        for h in range(N_HEADS):
            bc = bcol[:, N_HEADS + h:N_HEADS + h + 1]
            ic = g[:, h:h + 1]
            br = brow[N_HEADS + h:N_HEADS + h + 1, :]
            ir = gt[h:h + 1, :]
            dm = jnp.where(causal, bc - br + ir, NEG)
            a = bc + m_heads[h]
            mt = jnp.maximum(a, jnp.max(dm, axis=1, keepdims=True))
            qm = jnp.where(lh == h, q, 0.0).astype(BF16)
            s = _dot_nt(qm, kb) * jnp.exp(dm - mt)
            m_new = mt[L - 1:L, :]
            s_l.append(s)
            inter_l.append(jnp.exp(a - mt))
            deni_l.append(jnp.sum(s, axis=1, keepdims=True))
            emt_l.append(jnp.exp(-mt))
            wsrc_l.append(jnp.exp(bc[L - 1:L, :] - bc + ic - m_new))
            dec_l.append(jnp.exp(a[L - 1:L, :] - m_new))
            mnew_l.append(m_new)

        r = jnp.dot(jnp.concatenate(s_l, axis=0).astype(BF16), vb, preferred_element_type=F32)
        num_i = _by_head([r[h * L:(h + 1) * L, :] for h in range(N_HEADS)], lh)
        q_c = jnp.dot(qb, ct.astype(BF16), preferred_element_type=F32)
        q_n = jnp.dot(q * nrow, same_head, precision=HIGHEST, preferred_element_type=F32)
        inter = _by_head(inter_l, lh)
        num = num_i + inter * q_c
        den = _by_head(deni_l, lh) + inter * q_n
        hval = num / jnp.maximum(jnp.abs(den), _by_head(emt_l, lh))
        ms = jnp.dot(hval * hval, same_head, precision=HIGHEST,
                     preferred_element_type=F32) * (1.0 / HEAD_DIM)
        y = hval * lax.rsqrt(ms + NORM_EPS) * nw_ref[...]
        y_ref[b, rows, :] = (jax.nn.sigmoid(o_ref[b, rows, :]) * y).astype(y_ref.dtype)

        kw = ks * _by_head(wsrc_l, lh)
        dec = _by_head(dec_l, lh[0:1, :])
        upd = jnp.dot(kw.T.astype(BF16), vb, preferred_element_type=F32)
        ct_b[b] = ct * dec + jnp.where(bd, upd, 0.0)
        n_b[b] = nrow * dec + jnp.sum(kw, axis=0, keepdims=True)
        m_b[b] = mnew_l

    for c in range(q_ref.shape[1] // L):
        for b in range(batch):
            chunk(c, b)

    for b in range(batch):
        ct_ref[b] = ct_b[b]
        n_ref[b] = n_b[b]
        for h in range(N_HEADS):
            m_ref[b, h:h + 1, :] = jnp.broadcast_to(m_b[b][h], (1, m_ref.shape[2]))


def _mlstm(z, gates, gates_t, batch, seq, gate_b, norm_w):
    rows = _row_tile(seq, 2 * MLSTM_CHUNK)
    tps = seq // rows
    z3 = z.reshape(batch, seq, z.shape[1])
    zspec = lambda c: pl.BlockSpec((batch, rows, GROUP_W), lambda t, c=c: (0, t, c))
    gbc = jnp.zeros((1, GATE_W), F32).at[0, :2 * N_HEADS].set(gate_b.reshape(-1))
    gbr = gate_b.reshape(2 * N_HEADS, 1)
    state = lambda r, c: (pl.BlockSpec((batch, r, c), lambda t: (0, 0, 0)),
                          jax.ShapeDtypeStruct((batch, r, c), F32))
    (cs, csh), (ns, nsh), (msp, msh) = state(GROUP_W, GROUP_W), state(1, GROUP_W), state(8, LANE)
    gt_specs = [pl.BlockSpec((2 * N_HEADS, rows), lambda t, b=b: (0, b * tps + t)) for b in range(batch)]
    y, ct, nst, mst = pl.pallas_call(
        _mlstm_kernel,
        grid=(tps,),
        in_specs=[zspec(COL_MQ), zspec(COL_MK), zspec(COL_MV), zspec(COL_MO),
                  pl.BlockSpec((batch, rows, GATE_W), lambda t: (0, t, 0))] + gt_specs + [
                  pl.BlockSpec((1, GATE_W), lambda t: (0, 0)),
                  pl.BlockSpec((2 * N_HEADS, 1), lambda t: (0, 0)),
                  pl.BlockSpec((1, GROUP_W), lambda t: (0, 0))],
        out_specs=[pl.BlockSpec((batch, rows, GROUP_W), lambda t: (0, t, 0)), cs, ns, msp],
        out_shape=[jax.ShapeDtypeStruct((batch, seq, GROUP_W), BF16), csh, nsh, msh],
        compiler_params=_params("arbitrary"),
        name="mlstm",
    )(z3, z3, z3, z3, gates.reshape(batch, seq, GATE_W), *([gates_t] * batch), gbc, gbr, norm_w)
    return y.reshape(batch * seq, GROUP_W), ct, nst, mst


def _pick_top_blocks(s, n_valid_mask):
    blk = lax.broadcasted_iota(I32, s.shape, 0)
    s = jnp.where(n_valid_mask, s, NEG)
    bias = jnp.full(s.shape, NEG, F32)
    for _ in range(MOBA_TOPK):
        mx = jnp.max(s, axis=0, keepdims=True)
        idx = jnp.min(jnp.where(s == mx, blk, s.shape[0]), axis=0, keepdims=True)
        pick = blk == idx
        bias = jnp.where(jnp.logical_and(pick, mx > 0.5 * NEG), 0.0, bias)
        s = jnp.where(pick, NEG_PICKED, s)
    return bias


MOBA_KEY_CHUNK = 64
LOG2E = 1.4426950408889634


def _moba_logits(k_blk, qbd_sc, lt_sc, mx_sc, buf, key_le_query=None):
    B = MOBA_BLOCK
    lt = jnp.dot(k_blk[...], qbd_sc[...], preferred_element_type=F32)
    if key_le_query is not None:
        lt = jnp.where(jnp.concatenate([key_le_query] * N_HEADS, axis=1), lt, NEG)
    lt_sc[buf] = lt
    mx_sc[buf] = jnp.max(lt.reshape(B // 8, 8, N_HEADS * B), axis=0)


def _moba_softmax_pv(h, buf, vt_blk, lt_sc, mx_sc, p_sc, m_sc, l_sc, acc_sc, bias):
    B = MOBA_BLOCK
    ch = MOBA_KEY_CHUNK
    groups = ch // 8
    first = bias is None
    cols = slice(h * B, (h + 1) * B)
    cand = jnp.max(mx_sc[buf, :, cols], axis=0, keepdims=True)
    if first:
        m_new = shift = cand
    else:
        m_old = m_sc[h:h + 1, :]
        m_new = jnp.maximum(m_old, cand + bias)
        alpha = jnp.exp2(m_old - m_new)
        shift = m_new - bias
    lsum = None
    for c in range(B // ch):
        rows = slice(c * ch, (c + 1) * ch)
        p = jnp.exp2(lt_sc[buf, rows, cols] - shift)
        part = jnp.sum(p.reshape(groups, 8, B), axis=0)
        lsum = part if lsum is None else lsum + part
        p_sc[h, rows, :] = p.astype(BF16)
    lnew = jnp.sum(lsum, axis=0, keepdims=True)
    hs = slice(h * HEAD_DIM, (h + 1) * HEAD_DIM)
    pv = jnp.dot(vt_blk[hs, :], p_sc[h], preferred_element_type=F32)
    if first:
        l_sc[h:h + 1, :] = lnew
        acc_sc[hs, :] = pv
    else:
        l_sc[h:h + 1, :] = alpha * l_sc[h:h + 1, :] + lnew
        acc_sc[hs, :] = alpha * acc_sc[hs, :] + pv
    m_sc[h:h + 1, :] = m_new


def _moba_kernel(q_ref, kb_ref, vt_ref, km_ref, o_ref, qbd_sc, sel_sc, m_sc, l_sc, acc_sc, lt_sc, mx_sc, p_sc):
    i = pl.program_id(1)
    nb = km_ref.shape[0]
    nbp = sel_sc.shape[0] // N_HEADS
    B = MOBA_BLOCK
    q = q_ref[...]
    km = km_ref[...]
    past =lax.broadcasted_iota(I32, (nb, B), 0) < i
    key_le_query = lax.broadcasted_iota(I32, (B, B), 0) <= lax.broadcasted_iota(I32, (B, B), 1)
    stats = (lt_sc, mx_sc, p_sc, m_sc, l_sc, acc_sc)

    q_t = q.T
    q_ts = q_t * (HEAD_DIM ** -0.5 * LOG2E)
    row_head = _lane_head((GROUP_W, B), 0)
    lh_nb = _lane_head((nb, GROUP_W), 1)
    km_heads = jnp.concatenate([jnp.where(lh_nb == h, km, 0.0) for h in range(N_HEADS)], axis=0)
    s_all = jnp.dot(km_heads, q_t, precision=HIGHEST, preferred_element_type=F32)
    for h in range(N_HEADS):
        sel_sc[h * nbp:h * nbp + nb, :] = _pick_top_blocks(s_all[h * nb:(h + 1) * nb, :], past)
        qbd_sc[:, h * B:(h + 1) * B] = jnp.where(row_head == h, q_ts, 0.0).astype(BF16)

    _moba_logits(kb_ref.at[i], qbd_sc, lt_sc, mx_sc, 1, key_le_query)
    _moba_logits(kb_ref.at[0], qbd_sc, lt_sc, mx_sc, 0)
    for h in range(N_HEADS):
        _moba_softmax_pv(h, 1, vt_ref.at[i], *stats, None)

    def step(n, buf):
        _moba_logits(kb_ref.at[jnp.minimum(n + 1, nb - 1)], qbd_sc, lt_sc, mx_sc, 1 - buf)
        for h in range(N_HEADS):
            bias = sel_sc[pl.ds(h * nbp + n, 1), :]
            _moba_softmax_pv(h, buf, vt_ref.at[n], *stats, bias)

    def body(k, carry):
        step(2 * k, 0)
        step(2 * k + 1, 1)
        return carry

    lax.fori_loop(0, (i + 1) // 2, body, 0)

    for h in range(N_HEADS):
        hs = slice(h * HEAD_DIM, (h + 1) * HEAD_DIM)
        acc_sc[hs, :] = acc_sc[hs, :] / l_sc[h:h + 1, :]
    o_ref[...] = acc_sc[...].T.astype(o_ref.dtype)


def _moba(q, kb4, vt4, kmean, batch, nb):
    nbp = -(-nb // 8) * 8
    B = MOBA_BLOCK
    blk4 = pl.BlockSpec((None, nb, B, GROUP_W), lambda b, i: (b, 0, 0, 0))
    return pl.pallas_call(
        _moba_kernel,
        grid=(batch, nb),
        in_specs=[pl.BlockSpec((B, GROUP_W), lambda b, i: (b * nb + i, 0)),
                  blk4, blk4,
                  pl.BlockSpec((None, nb, GROUP_W), lambda b, i: (b, 0, 0))],
        out_specs=pl.BlockSpec((B, GROUP_W), lambda b, i: (b * nb + i, 0)),
        out_shape=jax.ShapeDtypeStruct((batch * nb * B, GROUP_W), BF16),
        scratch_shapes=[pltpu.VMEM((GROUP_W, N_HEADS * B), BF16),
                        pltpu.VMEM((N_HEADS * nbp, B), F32),
                        pltpu.VMEM((8, B), F32), pltpu.VMEM((8, B), F32),
                        pltpu.VMEM((GROUP_W, B), F32),
                        pltpu.VMEM((2, B, N_HEADS * B), F32),
                        pltpu.VMEM((2, 8, N_HEADS * B), F32),
                        pltpu.VMEM((N_HEADS, B, B), BF16)],
        compiler_params=_params("parallel", "arbitrary"),
        name="moba",
    )(q, kb4, vt4, kmean)


def _outproj_kernel(x_ref, ma_ref, mc_ref, md_ref, w_ref, o_ref):
    g2 = 2 * GROUP_W
    acc = jnp.dot(ma_ref[...], w_ref[0:g2, :], preferred_element_type=F32)
    acc += jnp.dot(mc_ref[...], w_ref[g2:g2 + GROUP_W, :], preferred_element_type=F32)
    acc += jnp.dot(md_ref[...], w_ref[g2 + GROUP_W:, :], preferred_element_type=F32)
    o_ref[...] = x_ref[...] + acc


def _outproj(x, ma, mc, md, w):
    n, d = x.shape
    tm = _row_tile(n, 512)
    rows = lambda c: pl.BlockSpec((tm, c), lambda i: (i, 0))
    return pl.pallas_call(
        _outproj_kernel,
        grid=(n // tm,),
        in_specs=[rows(d), rows(2 * GROUP_W), rows(GROUP_W), rows(GROUP_W),
                  pl.BlockSpec(w.shape, lambda i: (0, 0))],
        out_specs=rows(d),
        out_shape=jax.ShapeDtypeStruct((n, d), F32),
        compiler_params=_params("parallel"),
        name="outproj",
    )(x, ma, mc, md, w)


def _ff_tile(d_ff):
    return d_ff // 2 if (d_ff // 2) % LANE == 0 else d_ff


def _swiglu_partial(xn, w1_ref, w3_ref, w2_ref):
    a = jnp.dot(xn, w1_ref[...], preferred_element_type=F32)
    b = jnp.dot(xn, w3_ref[...], preferred_element_type=F32)
    hmid = (a * jax.nn.sigmoid(a) * b).astype(BF16)
    return jnp.dot(hmid, w2_ref[...], preferred_element_type=F32)


def _swiglu_full(xn, w1_ref, w3_ref, w2_ref):
    d_ff = w1_ref.shape[1]
    tf = _ff_tile(d_ff)
    acc = None
    for jj in range(d_ff // tf):
        sl = slice(jj * tf, (jj + 1) * tf)
        part = _swiglu_partial(xn, w1_ref.at[:, sl], w3_ref.at[:, sl], w2_ref.at[sl, :])
        acc = part if acc is None else acc + part
    return acc


def _ffn_kernel(x_ref, nw_ref, w1_ref, w3_ref, w2_ref, o_ref):
    x = x_ref[...]
    ms = jnp.mean(x * x, axis=-1, keepdims=True)
    xn = (x * lax.rsqrt(ms + NORM_EPS) * nw_ref[...]).astype(BF16)
    o_ref[...] = x + _swiglu_full(xn, w1_ref, w3_ref, w2_ref)


def _ffn(x, nw, w1, w3, w2):
    n, d = x.shape
    tm = _row_tile(n, 512)
    resident = lambda a: pl.BlockSpec(a.shape, lambda i: (0, 0), pipeline_mode=pl.Buffered(1))
    return pl.pallas_call(
        _ffn_kernel,
        grid=(n // tm,),
        in_specs=[pl.BlockSpec((tm, d), lambda i: (i, 0)),
                  pl.BlockSpec((1, d), lambda i: (0, 0)),
                  resident(w1), resident(w3), resident(w2)],
        out_specs=pl.BlockSpec((tm, d), lambda i: (i, 0)),
        out_shape=jax.ShapeDtypeStruct((n, d), F32),
        compiler_params=_params("parallel"),
        name="ffn_dense",
    )(x, nw, w1, w3, w2)


def _router_kernel(x_ref, nw_ref, rw_ref, rb_ref, xp_ref, g_ref, e_ref):
    x = x_ref[...]
    ms = jnp.mean(x * x, axis=-1, keepdims=True)
    xn = x * lax.rsqrt(ms + NORM_EPS) * nw_ref[...]
    tm, d = x.shape
    half = d // 2
    x_hi = xn.astype(BF16)
    bits = pltpu.bitcast(x_hi.astype(F32), jnp.uint32)
    packed = jnp.bitwise_or(jnp.right_shift(bits[:, :half], jnp.uint32(16)), bits[:, half:])
    nc = half // LANE
    for c in range(nc):
        xp_ref[pl.ds(c, tm, stride=nc), :] = packed[:, c * LANE:(c + 1) * LANE]

    rw = rw_ref[...]
    w_hi = rw.astype(BF16)
    w_lo = (rw - w_hi.astype(F32)).astype(BF16)
    x_lo = (xn - x_hi.astype(F32)).astype(BF16)
    dot = lambda a, b: jnp.dot(a, b, preferred_element_type=F32)
    logits = dot(x_hi, w_hi) + (dot(x_hi, w_lo) + dot(x_lo, w_hi)) + rb_ref[...]
    lane = lax.broadcasted_iota(I32, logits.shape, 1)
    mx1 = jnp.max(logits, axis=1, keepdims=True)
    i1 = jnp.min(jnp.where(logits == mx1, lane, LANE), axis=1, keepdims=True)
    rest = jnp.where(lane == i1, -jnp.inf, logits)
    mx2 = jnp.max(rest, axis=1, keepdims=True)
    i2 = jnp.min(jnp.where(rest == mx2, lane, LANE), axis=1, keepdims=True)
    e2 = jnp.exp(mx2 - mx1)
    g1 = 1.0 / (1.0 + e2)
    g_ref[...] = jnp.where(lane == 0, g1, jnp.where(lane == 1, e2 * g1, 0.0))
    e_ref[...] = jnp.where(lane == 0, i1, jnp.where(lane == 1, i2, 0))


def _router(x, nw, rw, rb):
    n, d = x.shape
    tm = _row_tile(n, 512)
    rows = lambda c: pl.BlockSpec((tm, c), lambda i: (i, 0))
    full = lambda a: pl.BlockSpec(a.shape, lambda i: (0, 0))
    return pl.pallas_call(
        _router_kernel,
        grid=(n // tm,),
        in_specs=[rows(d), full(nw), full(rw), full(rb)],
        out_specs=[pl.BlockSpec((tm * (d // 2 // LANE), LANE), lambda i: (i, 0)), rows(LANE), rows(LANE)],
        out_shape=[jax.ShapeDtypeStruct((n * (d // 2 // LANE), LANE), jnp.uint32),
                   jax.ShapeDtypeStruct((n, LANE), F32),
                   jax.ShapeDtypeStruct((n, LANE), I32)],
        compiler_params=_params("parallel"),
        name="router",
    )(x, nw, rw, rb)


def _unpack_rows(words):
    lo = pltpu.bitcast(jnp.left_shift(words, jnp.uint32(16)), F32)
    hi = pltpu.bitcast(jnp.bitwise_and(words, jnp.uint32(0xFFFF0000)), F32)
    return jnp.concatenate([lo, hi], axis=1).astype(BF16)


def _gmm_kernel(te_ref, src_ref, nused_ref, xp_hbm, w1_ref, w3_ref, w2_ref, o_ref, xbuf, sem):
    i = pl.program_id(0)
    ntiles = pl.num_programs(0)
    d = w1_ref.shape[0]
    oc = d // LANE
    tm = o_ref.shape[0] // oc
    nc = xbuf.shape[1] // tm
    slot = i % 2

    def issue(tile, sl):
        def body(g, carry):
            for u in range(ISSUE_UNROLL):
                r = g * ISSUE_UNROLL + u
                tok = src_ref[tile * tm + r]
                pltpu.make_async_copy(xp_hbm.at[pl.ds(pl.multiple_of(tok * nc, nc), nc), :],
                                      xbuf.at[sl, pl.ds(pl.multiple_of(r * nc, nc), nc), :],
                                      sem.at[sl]).start(priority=u % 2)
            return carry
        lax.fori_loop(0, tm // ISSUE_UNROLL, body, 0)

    @pl.when(i == 0)
    def _():
        issue(0, 0)

    @pl.when(i + 1 < ntiles)
    def _():
        issue(i + 1, 1 - slot)

    pltpu.make_async_copy(xbuf.at[slot], xbuf.at[slot], sem.at[slot]).wait()

    @pl.when(i < nused_ref[0])
    def _():
        words = jnp.concatenate([xbuf[slot, pl.ds(c, tm, stride=nc), :] for c in range(nc)], axis=1)
        acc = _swiglu_full(_unpack_rows(words), w1_ref, w3_ref, w2_ref)
        for c in range(oc):
            o_ref[pl.ds(c, tm, stride=oc), :] = acc[:, c * LANE:(c + 1) * LANE]

    @pl.when(i >= nused_ref[0])
    def _():
        o_ref[...] = jnp.zeros(o_ref.shape, F32)


def _gmm(xp, tile_expert, src, nused, w1, w3, w2, tm):
    ncap = src.shape[0]
    d = w1.shape[1]
    half = d // 2
    d_ff = w1.shape[2]
    oc = d // LANE
    expert = lambda r, c: pl.BlockSpec((None, r, c), lambda i, te, s, nu: (te[i], 0, 0))
    grid_spec = pltpu.PrefetchScalarGridSpec(
        num_scalar_prefetch=3,
        grid=(ncap // tm,),
        in_specs=[pl.BlockSpec(memory_space=pl.ANY), expert(d, d_ff), expert(d, d_ff), expert(d_ff, d)],
        out_specs=pl.BlockSpec((tm * oc, LANE), lambda i, te, s, nu: (i, 0)),
        scratch_shapes=[pltpu.VMEM((2, tm * (half // LANE), LANE), jnp.uint32),
                        pltpu.SemaphoreType.DMA((2,))])
    return pl.pallas_call(
        _gmm_kernel,
        grid_spec=grid_spec,
        out_shape=jax.ShapeDtypeStruct((ncap * oc, LANE), F32),
        compiler_params=_params("arbitrary"),
        name="moe_gmm",
    )(tile_expert, src, nused, xp, w1, w3, w2)


def _dispatch_plan(e_idx, n_experts, tm):
    n = e_idx.shape[0]
    ef = e_idx[:, :TOP_K].T.reshape(-1)
    onehot = (ef[:, None] == jnp.arange(n_experts, dtype=I32)[None, :]).astype(I32)
    rank = jnp.sum((jnp.cumsum(onehot, axis=0) - 1) * onehot, axis=1)
    counts = jnp.sum(onehot, axis=0)
    padded = ((counts + tm - 1) // tm) * tm
    ends = jnp.cumsum(padded)
    pos = (ends - padded)[ef] + rank
    ncap = TOP_K * n + n_experts * tm
    src = jnp.zeros((ncap,), I32).at[pos].set(jnp.arange(TOP_K * n, dtype=I32) % n,
                                              unique_indices=True, mode="promise_in_bounds")
    tile_start = jnp.arange(ncap // tm, dtype=I32) * tm
    tile_expert = jnp.minimum(jnp.searchsorted(ends, tile_start, side="right"), n_experts - 1).astype(I32)
    nused = (ends[-1] // tm).astype(I32).reshape(1)
    return pos.astype(I32), src, tile_expert, nused


def _combine_kernel(pos_ref, h_ref, g_ref, o_hbm, fw_ref, y_ref, obuf, sem, *, final_norm):
    i = pl.program_id(0)
    ntiles = pl.num_programs(0)
    tc, d = h_ref.shape
    oc = d // LANE
    n = ntiles * tc
    slot = i % 2

    def issue(tile, sl):
        def body(g, carry):
            for u in range(ISSUE_UNROLL):
                r = g * ISSUE_UNROLL + u
                for k in range(TOP_K):
                    p = pos_ref[k * n + tile * tc + r]
                    pltpu.make_async_copy(o_hbm.at[pl.ds(pl.multiple_of(p * oc, oc), oc), :],
                                          obuf.at[sl, k, pl.ds(pl.multiple_of(r * oc, oc), oc), :],
                                          sem.at[sl]).start(priority=k)
            return carry
        lax.fori_loop(0, tc // ISSUE_UNROLL, body, 0)

    @pl.when(i == 0)
    def _():
        issue(0, 0)

    @pl.when(i + 1 < ntiles)
    def _():
        issue(i + 1, 1 - slot)

    pltpu.make_async_copy(obuf.at[slot], obuf.at[slot], sem.at[slot]).wait()
    g = g_ref[...]
    rows = lambda k: jnp.concatenate([obuf[slot, k, pl.ds(c, tc, stride=oc), :] for c in range(oc)], axis=1)
    y = h_ref[...] + g[:, 0:1] * rows(0) + g[:, 1:2] * rows(1)
    if final_norm:
        ms = jnp.mean(y * y, axis=-1, keepdims=True)
        y = y * lax.rsqrt(ms + NORM_EPS) * fw_ref[...]
    y_ref[...] = y


def _combine(pos, h, gates, o_sorted, fw, final_norm):
    n, d = h.shape
    tc = _row_tile(n, 256)
    grid_spec = pltpu.PrefetchScalarGridSpec(
        num_scalar_prefetch=1,
        grid=(n // tc,),
        in_specs=[pl.BlockSpec((tc, d), lambda i, p: (i, 0)),
                  pl.BlockSpec((tc, LANE), lambda i, p: (i, 0)),
                  pl.BlockSpec(memory_space=pl.ANY),
                  pl.BlockSpec((1, d), lambda i, p: (0, 0))],
        out_specs=pl.BlockSpec((tc, d), lambda i, p: (i, 0)),
        scratch_shapes=[pltpu.VMEM((2, TOP_K, tc * (d // LANE), LANE), F32), pltpu.SemaphoreType.DMA((2,))])
    return pl.pallas_call(
        functools.partial(_combine_kernel, final_norm=final_norm),
        grid_spec=grid_spec,
        out_shape=jax.ShapeDtypeStruct((n, d), F32),
        compiler_params=_params("arbitrary"),
        name="moe_combine",
    )(pos, h, gates, o_sorted, fw)


def _moe_small_kernel(x_ref, nw_ref, g_ref, e_ref, w1_ref, w3_ref, w2_ref, fw_ref, y_ref,
                      xn_ref, acc_ref, *, final_norm):
    e = pl.program_id(0)
    j = pl.program_id(1)

    @pl.when(jnp.logical_and(e == 0, j == 0))
    def _():
        x = x_ref[...]
        ms = jnp.mean(x * x, axis=-1, keepdims=True)
        xn_ref[...] = (x * lax.rsqrt(ms + NORM_EPS) * nw_ref[...]).astype(BF16)
        acc_ref[...] = jnp.zeros(acc_ref.shape, F32)

    g = g_ref[...]
    idx = e_ref[...]
    gate = (jnp.where(idx[:, 0:1] == e, g[:, 0:1], 0.0) + jnp.where(idx[:, 1:2] == e, g[:, 1:2], 0.0))
    acc_ref[...] += gate * _swiglu_partial(xn_ref[...], w1_ref, w3_ref, w2_ref)

    @pl.when(jnp.logical_and(e == pl.num_programs(0) - 1, j == pl.num_programs(1) - 1))
    def _():
        y = x_ref[...] + acc_ref[...]
        if final_norm:
            ms = jnp.mean(y * y, axis=-1, keepdims=True)
            y = y * lax.rsqrt(ms + NORM_EPS) * fw_ref[...]
        y_ref[...] = y


def _moe_small(x, nw, gates, e_idx, w1, w3, w2, fw, final_norm):
    n, d = x.shape
    n_experts, _, d_ff = w1.shape
    tf = _ff_tile(d_ff)
    full = lambda a: pl.BlockSpec(a.shape, lambda e, j: (0, 0))
    return pl.pallas_call(
        functools.partial(_moe_small_kernel, final_norm=final_norm),
        grid=(n_experts, d_ff // tf),
        in_specs=[full(x), full(nw), full(gates), full(e_idx),
                  pl.BlockSpec((None, d, tf), lambda e, j: (e, 0, j)),
                  pl.BlockSpec((None, d, tf), lambda e, j: (e, 0, j)),
                  pl.BlockSpec((None, tf, d), lambda e, j: (e, j, 0)),
                  full(fw)],
        out_specs=full(x),
        out_shape=jax.ShapeDtypeStruct((n, d), F32),
        scratch_shapes=[pltpu.VMEM((n, d), BF16), pltpu.VMEM((n, d), F32)],
        compiler_params=_params("arbitrary", "arbitrary"),
        name="moe_small",
    )(x, nw, gates, e_idx, w1, w3, w2, fw)


def _norm_kernel(x_ref, w_ref, o_ref):
    x = x_ref[...]
    ms = jnp.mean(x * x, axis=-1, keepdims=True)
    o_ref[...] = x * lax.rsqrt(ms + NORM_EPS) * w_ref[...]


def _final_norm(x, w):
    n, d = x.shape
    tm = _row_tile(n, 512)
    return pl.pallas_call(
        _norm_kernel,
        grid=(n // tm,),
        in_specs=[pl.BlockSpec((tm, d), lambda i: (i, 0)), pl.BlockSpec((1, d), lambda i: (0, 0))],
        out_specs=pl.BlockSpec((tm, d), lambda i: (i, 0)),
        out_shape=jax.ShapeDtypeStruct((n, d), F32),
        compiler_params=_params("parallel"),
        name="final_norm",
    )(x, w)


def _smix_kernel(p_ref, pf_ref, pw_ref, ps_ref, su_ref, sv_ref, sw0_ref, sb0_ref,
                 q_ref, k_ref, v_ref, ig_ref, fg_ref, c_ref, n_ref, m_ref, mo_ref, nw_ref,
                 ypool_ref, ysgu_ref, yml_ref, cn_ref, nn_ref, mn_ref, *, cnt):
    p = p_ref[...]
    lane_w = _window_of_lane(p.shape, 1, 0)
    acc = p
    wsum = jnp.zeros(p.shape, F32)
    for off in range(1, POOL_WINDOWS[-1] + 1):
        if off in POOL_WINDOWS:
            wsum = jnp.where(lane_w == off, acc, wsum)
        if off <= POOL_BUF:
            acc = acc + pf_ref[HALO - off]
    count = jnp.minimum(lane_w, cnt).astype(F32)
    d = (wsum / count - p).astype(BF16)
    ypool_ref[...] = jnp.dot(d, pw_ref[...], preferred_element_type=F32) * ps_ref[...]

    ysgu_ref[...] = su_ref[...] * (sw0_ref[...] * sv_ref[...] + sb0_ref[...])

    q = q_ref[...]
    ks = k_ref[...] * (HEAD_DIM ** -0.5)
    v = v_ref[...]
    ig = ig_ref[...]
    c = c_ref[...]
    nrow = n_ref[...]
    a = _logsig(fg_ref[...]) + m_ref[...]
    mt = jnp.maximum(a, ig)
    inter = jnp.exp(a - mt)
    e_i = jnp.exp(ig - mt)
    s = jnp.sum(q * ks, axis=-1, keepdims=True) * e_i
    c_q = jnp.sum(c * q, axis=-1, keepdims=True)
    n_q = jnp.sum(nrow * q, axis=-1, keepdims=True)
    num = s * v + inter * c_q
    den = s + inter * n_q
    hval = num / jnp.maximum(jnp.abs(den), jnp.exp(-mt))
    cn_ref[...] = inter * c + (e_i * v) * ks
    nn_ref[...] = inter * nrow + e_i * ks
    mn_ref[...] = mt
    ms = jnp.mean(hval * hval, axis=1, keepdims=True)
    y = hval * lax.rsqrt(ms + NORM_EPS) * nw_ref[...]
    yml_ref[...] = jax.nn.sigmoid(mo_ref[...]) * y


def _smix(args, out_shapes, cnt):
    return pl.pallas_call(
        functools.partial(_smix_kernel, cnt=cnt),
        out_shape=out_shapes,
        compiler_params=pltpu.CompilerParams(vmem_limit_bytes=VMEM_LIMIT),
        name="sample_mixers",
    )(*args)


PAGES_PER_STEP = 64
PAGES_PER_BLOCK = MOBA_BLOCK // PAGE_SIZE


def _paged_kmean_kernel(pt_ref, *refs):
    o_ref = refs[-1]
    j = pl.program_id(1)
    nblk = PAGES_PER_STEP // PAGES_PER_BLOCK

    @pl.when(j == 0)
    def _():
        o_ref[...] = jnp.zeros(o_ref.shape, F32)

    lane = lax.broadcasted_iota(I32, o_ref.shape, 1)
    acc = o_ref[...]
    for bb in range(nblk):
        s = refs[bb * PAGES_PER_BLOCK][...]
        for pg in range(1, PAGES_PER_BLOCK):
            s = s + refs[bb * PAGES_PER_BLOCK + pg][...]
        col = jnp.sum(s, axis=1, keepdims=True) * (1.0 / MOBA_BLOCK)
        acc = jnp.where(lane == j * nblk + bb, col, acc)
    o_ref[...] = acc


def _paged_kmean(cache_t, page_table, page_base):
    bs, n_pages = page_table.shape
    nb = n_pages // PAGES_PER_BLOCK
    page = lambda s: pl.BlockSpec((None, GROUP_W, PAGE_SIZE),
                                  lambda b, j, pt, s=s: (page_base + pt[b, j * PAGES_PER_STEP + s], 0, 0))
    grid_spec = pltpu.PrefetchScalarGridSpec(
        num_scalar_prefetch=1,
        grid=(bs, n_pages // PAGES_PER_STEP),
        in_specs=[page(s) for s in range(PAGES_PER_STEP)],
        out_specs=pl.BlockSpec((None, GROUP_W, nb), lambda b, j, pt: (b, 0, 0)))
    return pl.pallas_call(
        _paged_kmean_kernel,
        grid_spec=grid_spec,
        out_shape=jax.ShapeDtypeStruct((bs, GROUP_W, nb), F32),
        compiler_params=_params("parallel", "arbitrary"),
        name="paged_kmean",
    )(page_table, *([cache_t] * PAGES_PER_STEP))


def _sample_select_kernel(q_ref, kmt_ref, o_ref):
    head_row = lax.broadcasted_iota(I32, (8, GROUP_W), 0) == _lane_head((8, GROUP_W), 1)
    for b in range(q_ref.shape[0]):
        q8 = jnp.where(head_row, q_ref[b], 0.0)
        s = jnp.dot(q8, kmt_ref[b], precision=HIGHEST, preferred_element_type=F32)
        blk = lax.broadcasted_iota(I32, s.shape, 1)
        slot = lax.broadcasted_iota(I32, o_ref.shape[1:], 1)
        out = jnp.zeros(o_ref.shape[1:], I32)
        for r in range(MOBA_TOPK):
            mx = jnp.max(s, axis=1, keepdims=True)
            idx = jnp.min(jnp.where(s == mx, blk, s.shape[1]), axis=1, keepdims=True)
            out = jnp.where(slot == r, idx, out)
            s = jnp.where(blk == idx, NEG_PICKED, s)
        o_ref[b] = out


def _sample_select(q3, kmean_t):
    bs, _, nb = kmean_t.shape
    sb = _row_tile(bs, 8)
    return pl.pallas_call(
        _sample_select_kernel,
        grid=(bs // sb,),
        in_specs=[pl.BlockSpec((sb, 1, GROUP_W), lambda b: (b, 0, 0)),
                  pl.BlockSpec((sb, GROUP_W, nb), lambda b: (b, 0, 0))],
        out_specs=pl.BlockSpec((sb, 8, LANE), lambda b: (b, 0, 0)),
        out_shape=jax.ShapeDtypeStruct((bs, 8, LANE), I32),
        compiler_params=_params("parallel"),
        name="sample_select",
    )(q3, kmean_t)


N_SEL_PAGES = MOBA_TOPK * PAGES_PER_BLOCK


def _sample_attn_kernel(ph_ref, q_ref, kn_ref, vn_ref, *refs):
    n_pg = N_HEADS * N_SEL_PAGES
    kp, vp, o_ref = refs[:n_pg], refs[n_pg:2 * n_pg], refs[-1]
    for h in range(N_HEADS):
        hs = slice(h * HEAD_DIM, (h + 1) * HEAD_DIM)
        pages = slice(h * N_SEL_PAGES, (h + 1) * N_SEL_PAGES)
        q = q_ref[hs, :] * (HEAD_DIM ** -0.5)
        own = jnp.sum(kn_ref[hs, :] * q, axis=0, keepdims=True)
        logits = [jnp.sum(r[...] * q, axis=0, keepdims=True) for r in kp[pages]]
        m = own
        for lg in logits:
            m = jnp.maximum(m, jnp.max(lg, axis=1, keepdims=True))
        p_own = jnp.exp(own - m)
        den = p_own
        acc = p_own * vn_ref[hs, :]
        for lg, vr in zip(logits, vp[pages]):
            p = jnp.exp(lg - m)
            den = den + jnp.sum(p, axis=1, keepdims=True)
            acc = acc + jnp.sum(vr[...] * p, axis=1, keepdims=True)
        o_ref[hs, :] = acc / den


def _sample_attn(phys, q_col, kn_col, vn_col, cache_kt, cache_vt, page_base):
    bs = q_col.shape[0]
    col = pl.BlockSpec((None, GROUP_W, 1), lambda b, ph: (b, 0, 0))
    page = lambda h, s: pl.BlockSpec(
        (None, HEAD_DIM, PAGE_SIZE),
        lambda b, ph, h=h, s=s: (page_base + ph[(b * N_HEADS + h) * N_SEL_PAGES + s], h, 0))
    pages = [page(h, s) for h in range(N_HEADS) for s in range(N_SEL_PAGES)]
    grid_spec = pltpu.PrefetchScalarGridSpec(
        num_scalar_prefetch=1,
        grid=(bs,),
        in_specs=[col, col, col] + pages + pages,
        out_specs=col)
    return pl.pallas_call(
        _sample_attn_kernel,
        grid_spec=grid_spec,
        out_shape=jax.ShapeDtypeStruct((bs, GROUP_W, 1), F32),
        compiler_params=_params("parallel"),
        name="sample_attn",
    )(phys, q_col, kn_col, vn_col, *([cache_kt] * len(pages)), *([cache_vt] * len(pages)))


def _prep_layer(w_in, w_out, pool_w, pool_scale, sgu_w, sgu_b, norm_w):
    d = w_in.shape[0]
    g = GROUP_W
    gate0 = 7 * g
    att0 = gate0 + 2 * N_HEADS
    pad = jnp.zeros((d, GATE_W - 2 * N_HEADS), w_in.dtype)
    w_z = jnp.concatenate([w_in[:, :gate0], w_in[:, att0:], w_in[:, gate0:att0], pad], axis=1).astype(BF16)
    wbd = jnp.zeros((g, g), F32)
    for i in range(len(POOL_WINDOWS)):
        sl = slice(i * POOL_GROUP, (i + 1) * POOL_GROUP)
        wbd = wbd.at[sl, sl].set(pool_w[i])
    return dict(
        w_z=w_z, w_out=w_out.astype(BF16), pool_wbd=wbd.astype(BF16),
        pool_scale=pool_scale.reshape(1, g),
        sgu_w=sgu_w, sgu_bias=jnp.repeat(sgu_b.T, HEAD_DIM, axis=1),
        sgu_w0=jnp.repeat(sgu_w[:, 0, 0], HEAD_DIM).reshape(1, g),
        sgu_b0=jnp.repeat(sgu_b[:, 0], HEAD_DIM).reshape(1, g),
        norm_w=norm_w.reshape(1, g))


def _head_cols(a, bs):
    return a.reshape(bs * N_HEADS, HEAD_DIM, 1)


def _head_rows(a, bs):
    return a.reshape(bs * N_HEADS, 1, HEAD_DIM)


def _mixers_prompt(x, batch, seq, nmw, lp, gate_b, rope_tabs, layer, depth, knew, vnew):
    g = GROUP_W
    nb = seq // MOBA_BLOCK
    zm, q, gates, gates_t, kb, vt, kmean, knew, vnew = _inproj_prompt(
        x, nmw, lp["w_z"], rope_tabs, batch, seq, layer, depth, knew, vnew)
    m_ab = _poolsgu(zm, batch, seq, lp["pool_wbd"], lp["pool_scale"], lp["sgu_w"], lp["sgu_bias"])
    m_c, ct, nst, mst = _mlstm(zm, gates, gates_t, batch, seq, gate_b, lp["norm_w"])
    blocks = lambda a: a.reshape(batch, nb, MOBA_BLOCK, g)
    m_d = _moba(q, blocks(kb), blocks(vt), kmean, batch, nb)
    h = _outproj(x, m_ab, m_c, m_d, lp["w_out"])

    c_new = jnp.stack([ct[:, i * HEAD_DIM:(i + 1) * HEAD_DIM, i * HEAD_DIM:(i + 1) * HEAD_DIM]
                       for i in range(N_HEADS)], axis=1)
    state = (zm[:, :g].reshape(batch, seq, g)[:, seq - POOL_BUF:],
             jnp.swapaxes(c_new, -1, -2), nst.reshape(batch, N_HEADS, HEAD_DIM), mst[:, :N_HEADS, 0])
    return h, state, knew, vnew


def _mixers_sample(x, past_len, nmw, lp, gate_b, rope_tabs, pool_state, c_st, n_st, m_st,
                   cache_k, cache_v, page_table, page_base):
    bs = x.shape[0]
    g = GROUP_W
    z = _inproj(x, nmw, lp["w_z"], rope_tabs)
    col = lambda c: z[:, c * g:(c + 1) * g]
    p, sv = col(COL_P), col(COL_SV)
    gates = z[:, 10 * g:10 * g + 2 * N_HEADS] + gate_b.reshape(1, -1)
    bh = bs * N_HEADS
    prefix = jnp.concatenate([jnp.zeros((1, bs, g), F32), jnp.swapaxes(pool_state, 0, 1)], axis=0)
    args = (p, prefix, lp["pool_wbd"], lp["pool_scale"], col(COL_SU), sv, lp["sgu_w0"], lp["sgu_b0"],
            _head_rows(col(COL_MQ), bs), _head_rows(col(COL_MK), bs), _head_cols(col(COL_MV), bs),
            gates[:, :N_HEADS].reshape(bh, 1, 1), gates[:, N_HEADS:].reshape(bh, 1, 1),
            c_st.reshape(bh, HEAD_DIM, HEAD_DIM), n_st.reshape(bh, 1, HEAD_DIM), m_st.reshape(bh, 1, 1),
            _head_cols(col(COL_MO), bs), _head_cols(jnp.tile(lp["norm_w"], (bs, 1)), bs))
    sds = jax.ShapeDtypeStruct
    outs = (sds((bs, g), F32), sds((bs, g), F32), sds((bh, HEAD_DIM, 1), F32),
            sds((bh, HEAD_DIM, HEAD_DIM), F32), sds((bh, 1, HEAD_DIM), F32), sds((bh, 1, 1), F32))
    y_pool, y_sgu, y_ml, c_new, n_new, m_new = _smix(args, outs, min(past_len + 1, POOL_WINDOWS[-1]))

    q_col, k_col, v_col = (col(c).reshape(bs, g, 1) for c in (COL_AQ, COL_AK, COL_AV))
    kmean_t = _paged_kmean(cache_k, page_table, page_base)
    sel = _sample_select(col(COL_AQ).reshape(bs, 1, g), kmean_t)[:, :N_HEADS, :MOBA_TOPK]
    pages = sel[..., None] * PAGES_PER_BLOCK + jnp.arange(PAGES_PER_BLOCK, dtype=I32)
    phys = jnp.take_along_axis(page_table, pages.reshape(bs, -1), axis=1).reshape(-1)
    y_at = _sample_attn(phys, q_col, k_col, v_col, cache_k, cache_v, page_base).reshape(bs, g)

    m_ab = jnp.concatenate([y_pool, y_sgu], axis=1).astype(BF16)
    h = _outproj(x, m_ab, y_ml.reshape(bs, g).astype(BF16), y_at.astype(BF16), lp["w_out"])

    heads = lambda a: a.reshape(bs, 1, N_HEADS, HEAD_DIM)
    state = (heads(col(COL_AK)), heads(col(COL_AV)),
             jnp.concatenate([pool_state[:, 1:], p[:, None, :]], axis=1), sv.reshape(bs, 1, g),
             c_new.reshape(bs, N_HEADS, HEAD_DIM, HEAD_DIM), n_new.reshape(bs, N_HEADS, HEAD_DIM),
             m_new.reshape(bs, N_HEADS))
    return h, state


def kernel(x_prompt, x_sample, cache_k, cache_v, page_table, state_pool, state_mlstm_c, state_mlstm_n,
           state_mlstm_m, norm_mix_w, norm_ffn_w, final_norm_w, w_in, w_out, pool_w, pool_scale, sgu_w,
           sgu_b, mlstm_gate_b, mlstm_norm_w, ffn_w1, ffn_w3, ffn_w2, router_w, router_b,
           moe_w1, moe_w3, moe_w2):
    batch, seq, d = x_prompt.shape
    bs = x_sample.shape[0]
    depth = w_in.shape[0]
    n_pool = cache_k.shape[1]
    n_pages = page_table.shape[1]
    past_len = n_pages * PAGE_SIZE
    n_experts = router_w.shape[-1]
    assert x_sample.shape[1] == 1 and d == N_MIXERS * GROUP_W
    assert seq % MOBA_BLOCK == 0 and seq % (2 * MLSTM_CHUNK) == 0 and seq >= POOL_BUF
    assert past_len % MOBA_BLOCK == 0 and past_len // MOBA_BLOCK >= MOBA_TOPK
    assert n_pages % PAGES_PER_STEP == 0

    rope_p = _rope_tables(jnp.arange(seq, dtype=I32))
    rope_s = _rope_tables(jnp.full((bs,), past_len, I32))
    pages_t = lambda c: jnp.transpose(c, (0, 1, 3, 4, 2)).reshape(depth * n_pool, GROUP_W, PAGE_SIZE)
    ck, cv = pages_t(cache_k), pages_t(cache_v)
    fw = final_norm_w.reshape(1, d)
    tm_moe = 512

    hp = x_prompt.reshape(batch * seq, d)
    hs = x_sample.reshape(bs, d)
    st_p, st_s = [], []
    knew = jnp.zeros((depth, batch, GROUP_W, seq), F32)
    vnew = jnp.zeros((depth, batch, GROUP_W, seq), F32)
    for l in range(depth):
        lp = _prep_layer(w_in[l], w_out[l], pool_w[l], pool_scale[l], sgu_w[l], sgu_b[l], mlstm_norm_w[l])
        nmw = norm_mix_w[l].reshape(1, d)
        nfw = norm_ffn_w[l].reshape(1, d)
        hp, sp, knew, vnew = _mixers_prompt(hp, batch, seq, nmw, lp, mlstm_gate_b[l], rope_p,
                                            l, depth, knew, vnew)
        hs, ss = _mixers_sample(hs, past_len, nmw, lp, mlstm_gate_b[l], rope_s, state_pool[l],
                                state_mlstm_c[l], state_mlstm_n[l], state_mlstm_m[l],
                                ck, cv, page_table, l * n_pool)
        st_p.append(sp)
        st_s.append(ss)
        last = l == depth - 1
        j = l // 2
        if l % 2 == 0:
            w1, w3, w2 = ffn_w1[j].astype(BF16), ffn_w3[j].astype(BF16), ffn_w2[j].astype(BF16)
            hp = _ffn(hp, nfw, w1, w3, w2)
            hs = _ffn(hs, nfw, w1, w3, w2)
            if last:
                hp, hs = _final_norm(hp, fw), _final_norm(hs, fw)
        else:
            w1, w3, w2 = moe_w1[j].astype(BF16), moe_w3[j].astype(BF16), moe_w2[j].astype(BF16)
            rw = jnp.zeros((d, LANE), F32).at[:, :n_experts].set(router_w[j])
            rb = jnp.full((1, LANE), -jnp.inf, F32).at[0, :n_experts].set(router_b[j])
            xp, gates, e_idx = _router(hp, nfw, rw, rb)
            pos, src, tile_expert, nused = _dispatch_plan(e_idx, n_experts, tm_moe)
            o_sorted = _gmm(xp, tile_expert, src, nused, w1, w3, w2, tm_moe)
            hp = _combine(pos, hp, gates, o_sorted, fw, last)
            _, gates_s, e_s = _router(hs, nfw, rw, rb)
            hs = _moe_small(hs, nfw, gates_s, e_s, w1, w3, w2, fw, last)

    stack = lambda sts, i: jnp.stack([s[i] for s in sts])
    kv_leaf = lambda a: jnp.transpose(a.reshape(depth, batch, N_HEADS, HEAD_DIM, seq), (0, 1, 4, 2, 3))
    return (hp.reshape(batch, seq, d), hs.reshape(bs, 1, d),
            kv_leaf(knew), kv_leaf(vnew), stack(st_s, 0), stack(st_s, 1),
            stack(st_p, 0), stack(st_s, 2), stack(st_s, 3),
            stack(st_p, 1), stack(st_p, 2), stack(st_p, 3),
            stack(st_s, 4), stack(st_s, 5), stack(st_s, 6))
```

```python
import functools

import numpy as np
import jax
import jax.numpy as jnp
from jax import lax
from jax.experimental import pallas as pl
from jax.experimental.pallas import tpu as pltpu

F32 = jnp.float32
BF16 = jnp.bfloat16
I32 = jnp.int32
HIGHEST = lax.Precision.HIGHEST

N_MIXERS = 4
HEAD_DIM = 64
N_HEADS = 4
GROUP_W = N_HEADS * HEAD_DIM
POOL_WINDOWS = (2, 4, 8, 16)
POOL_GROUP = GROUP_W // len(POOL_WINDOWS)
POOL_BUF = max(POOL_WINDOWS) - 1
HALO = POOL_BUF + 1
SGU_CHUNK = 128
MLSTM_CHUNK = 64
MOBA_BLOCK = 256
MOBA_TOPK = 3
PAGE_SIZE = 128
ROPE_THETA = 500000.0
ROT_DIM = HEAD_DIM // 4
TOP_K = 2
NORM_EPS = 1e-6
NEG = -1e30
NEG_PICKED = -3e38

LANE = 128
GATE_W = LANE
Z_W = 10 * GROUP_W + GATE_W
Z_TN = 7 * LANE
COL_P, COL_SU, COL_SV, COL_MQ, COL_MK, COL_MV, COL_MO, COL_AQ, COL_AK, COL_AV = range(10)
COL_GATE = (10 * GROUP_W) // GATE_W
VMEM_LIMIT = 56 * 1024 * 1024
ISSUE_UNROLL = 8


def _params(*sem):
    return pltpu.CompilerParams(dimension_semantics=sem, vmem_limit_bytes=VMEM_LIMIT)


def _row_tile(n, pref):
    return pref if n % pref == 0 else n


def _lane_head(shape, dim):
    return lax.broadcasted_iota(I32, shape, dim) // HEAD_DIM


def _by_head(vals, lh):
    out = vals[N_HEADS - 1]
    for h in range(N_HEADS - 2, -1, -1):
        out = jnp.where(lh == h, vals[h], out)
    return out


def _dot_nt(a, b, precision=None):
    return lax.dot_general(a, b, (((1,), (1,)), ((), ())), precision=precision,
                           preferred_element_type=F32)


def _logsig(x):
    return jnp.minimum(x, 0.0) - jnp.log1p(jnp.exp(-jnp.abs(x)))


def _split_bf16(x, passes):
    terms = []
    for _ in range(passes):
        t = x.astype(BF16)
        terms.append(t)
        x = x - t.astype(F32)
    return terms


def _dot_exact01(x, onehot_bf16, passes, x_is_lhs=True):
    acc = None
    for t in _split_bf16(x, passes):
        part = (jnp.dot(t, onehot_bf16, preferred_element_type=F32) if x_is_lhs
                else jnp.dot(onehot_bf16, t, preferred_element_type=F32))
        acc = part if acc is None else acc + part
    return acc


def _inproj_kernel(x_ref, nw_ref, w_ref, cos_ref, sa_ref, sb_ref, z_ref, xn_ref):
    j = pl.program_id(1)

    @pl.when(j == 0)
    def _():
        x = x_ref[...]
        ms = jnp.mean(x * x, axis=-1, keepdims=True)
        xn_ref[...] = (x * lax.rsqrt(ms + NORM_EPS) * nw_ref[...]).astype(BF16)

    z = jnp.dot(xn_ref[...], w_ref[...], preferred_element_type=F32)

    @pl.when(j < 2)
    def _():
        z_ref[...] = z

    @pl.when(j == 2)
    def _():
        c, sa, sb = cos_ref[...], sa_ref[...], sb_ref[...]
        half = ROT_DIM // 2
        for o in (0, GROUP_W):
            t = z[:, o:o + GROUP_W]
            z_ref[:, o:o + GROUP_W] = (t * c + pltpu.roll(t, GROUP_W - half, 1) * sa
                                       + pltpu.roll(t, half, 1) * sb)
        z_ref[:, 2 * GROUP_W:] = z[:, 2 * GROUP_W:]


def _inproj(x, nw, w, rope_tabs):
    n, d = x.shape
    tm = _row_tile(n, 512)
    tab_tiles = rope_tabs[0].shape[0] // tm
    tab_spec = pl.BlockSpec((tm, GROUP_W), lambda i, j: (i % tab_tiles, 0))
    return pl.pallas_call(
        _inproj_kernel,
        grid=(n // tm, Z_W // Z_TN),
        in_specs=[pl.BlockSpec((tm, d), lambda i, j: (i, 0)),
                  pl.BlockSpec((1, d), lambda i, j: (0, 0)),
                  pl.BlockSpec((d, Z_TN), lambda i, j: (0, j)),
                  tab_spec, tab_spec, tab_spec],
        out_specs=pl.BlockSpec((tm, Z_TN), lambda i, j: (i, j)),
        out_shape=jax.ShapeDtypeStruct((n, Z_W), F32),
        scratch_shapes=[pltpu.VMEM((tm, d), BF16)],
        compiler_params=_params("parallel", "arbitrary"),
        name="inproj",
    )(x, nw, w, *rope_tabs)


def _rope(t, c, sa, sb):
    half = ROT_DIM // 2
    return t * c + pltpu.roll(t, GROUP_W - half, 1) * sa + pltpu.roll(t, half, 1) * sb


def _inproj_prompt_kernel(x_ref, nw_ref, w_ref, cos_ref, sa_ref, sb_ref, kprev_hbm, vprev_hbm,
                          zm_ref, q_ref, g_ref, gt_ref, kb_ref, vt_ref, km_ref, knew_ref, vnew_ref,
                          *, tiles_per_seq):
    del kprev_hbm, vprev_hbm
    i = pl.program_id(0)
    tm = x_ref.shape[0]
    g = GROUP_W

    ti = i % tiles_per_seq

    @pl.when(ti == 0)
    def _():
        km_ref[...] = jnp.zeros(km_ref.shape, F32)

    x = x_ref[...]
    ms = jnp.mean(x * x, axis=-1, keepdims=True)
    xn = (x * lax.rsqrt(ms + NORM_EPS) * nw_ref[...]).astype(BF16)
    for jj in range(2):
        zm_ref[:, jj * Z_TN:(jj + 1) * Z_TN] = jnp.dot(xn, w_ref[:, jj * Z_TN:(jj + 1) * Z_TN],
                                                       preferred_element_type=F32)
    z = jnp.dot(xn, w_ref[:, 2 * Z_TN:], preferred_element_type=F32)

    c, sa, sb = cos_ref[...], sa_ref[...], sb_ref[...]
    q_ref[...] = _rope(z[:, 0:g], c, sa, sb)
    k = _rope(z[:, g:2 * g], c, sa, sb)
    v = z[:, 2 * g:3 * g]
    gates = z[:, 3 * g:]
    g_ref[...] = gates
    gt_ref[...] = gates.T[0:2 * N_HEADS, :]
    nblk = tm // MOBA_BLOCK
    kb_ref[...] = k.astype(BF16).reshape(nblk, MOBA_BLOCK, g)
    knew_ref[...] = k.T
    v_t = v.T
    vnew_ref[...] = v_t
    km = km_ref[...]
    row = lax.broadcasted_iota(I32, km.shape, 0)
    for b in range(nblk):
        cols = slice(b * MOBA_BLOCK, (b + 1) * MOBA_BLOCK)
        vt_ref[b] = v_t[:, cols].astype(BF16)
        mean = jnp.sum(k[cols, :], axis=0, keepdims=True) * (1.0 / MOBA_BLOCK)
        km = jnp.where(row == ti * nblk + b, mean, km)
    km_ref[...] = km


def _inproj_prompt(x, nw, w, rope_tabs, batch, seq, layer, depth, knew, vnew):
    n, d = x.shape
    tm = 2 * MOBA_BLOCK
    assert seq % tm == 0
    tps = seq // tm
    nb = seq // MOBA_BLOCK
    nblk = tm // MOBA_BLOCK
    g = GROUP_W
    tab_spec = pl.BlockSpec((tm, g), lambda i: (i % tps, 0))
    slab = pl.BlockSpec((None, None, g, tm), lambda i: (layer, i // tps, 0, i % tps))
    blocks = pl.BlockSpec((nblk, MOBA_BLOCK, g), lambda i: (i, 0, 0))
    rows = lambda c: pl.BlockSpec((tm, c), lambda i: (i, 0))
    sds = jax.ShapeDtypeStruct
    slab_shape = sds((depth, batch, g, seq), F32)
    any_spec = pl.BlockSpec(memory_space=pl.ANY)
    return pl.pallas_call(
        functools.partial(_inproj_prompt_kernel, tiles_per_seq=tps),
        grid=(n // tm,),
        in_specs=[rows(d),
                  pl.BlockSpec((1, d), lambda i: (0, 0)),
                  pl.BlockSpec((d, Z_W), lambda i: (0, 0)),
                  tab_spec, tab_spec, tab_spec, any_spec, any_spec],
        out_specs=[rows(2 * Z_TN), rows(g), rows(GATE_W),
                   pl.BlockSpec((2 * N_HEADS, tm), lambda i: (0, i)),
                   blocks, blocks,
                   pl.BlockSpec((None, nb, g), lambda i: (i // tps, 0, 0)),
                   slab, slab],
        out_shape=[sds((n, 2 * Z_TN), F32), sds((n, g), F32), sds((n, GATE_W), F32),
                   sds((2 * N_HEADS, n), F32),
                   sds((batch * nb, MOBA_BLOCK, g), BF16), sds((batch * nb, g, MOBA_BLOCK), BF16),
                   sds((batch, nb, g), F32), slab_shape, slab_shape],
        input_output_aliases={6: 7, 7: 8},
        compiler_params=_params("arbitrary"),
        name="inproj_prompt",
    )(x, nw, w, *rope_tabs, knew, vnew)


def _rope_tables(pos):
    half = ROT_DIM // 2
    inv = 1.0 / (ROPE_THETA ** (jnp.arange(half, dtype=F32) / half))
    ang = pos.astype(F32)[:, None] * inv[None, :]
    cos, sin = jnp.cos(ang), jnp.sin(ang)
    t = pos.shape[0]
    rest = HEAD_DIM - ROT_DIM
    one = jnp.ones((t, rest), F32)
    zero = jnp.zeros((t, rest), F32)
    zh = jnp.zeros((t, half), F32)
    c = jnp.concatenate([cos, cos, one], axis=1)
    sa = jnp.concatenate([-sin, zh, zero], axis=1)
    sb = jnp.concatenate([zh, sin, zero], axis=1)
    tile = lambda a: jnp.tile(a, (1, N_HEADS))
    return tile(c), tile(sa), tile(sb)


def _window_of_lane(shape, dim, lo):
    g = (lax.broadcasted_iota(I32, shape, dim) + lo) // POOL_GROUP
    w = jnp.full(shape, POOL_WINDOWS[-1], I32)
    for i in range(len(POOL_WINDOWS) - 2, -1, -1):
        w = jnp.where(g == i, POOL_WINDOWS[i], w)
    return w


def _poolsgu_kernel(p_ref, su_ref, sv_ref, pw_ref, ps_ref, sw_ref, sb_ref, o_ref, ebuf):
    t = pl.program_id(1)
    tt = p_ref.shape[0]

    @pl.when(t == 0)
    def _():
        ebuf[0:HALO, :] = jnp.zeros((HALO, GROUP_W), F32)

    @pl.when(t > 0)
    def _():
        ebuf[0:HALO, :] = ebuf[tt:tt + HALO, :]

    ebuf[HALO:HALO + tt, :] = p_ref[...]

    pos1 = lax.broadcasted_iota(I32, (tt, LANE), 0) + t * tt + 1
    halves = []
    for half in range(2):
        lo = half * LANE
        wsmall, wbig = POOL_WINDOWS[2 * half], POOL_WINDOWS[2 * half + 1]
        e0 = ebuf[HALO:HALO + tt, lo:lo + LANE]
        acc = e0
        for off in range(1, wsmall):
            acc = acc + ebuf[HALO - off:HALO - off + tt, lo:lo + LANE]
        small = acc
        for off in range(wsmall, wbig):
            acc = acc + ebuf[HALO - off:HALO - off + tt, lo:lo + LANE]
        first = lax.broadcasted_iota(I32, (tt, LANE), 1) < POOL_GROUP
        wsum = jnp.where(first, small, acc)
        cnt = jnp.minimum(pos1, _window_of_lane((tt, LANE), 1, lo)).astype(F32)
        halves.append(wsum / cnt - e0)
    d = jnp.concatenate(halves, axis=1).astype(BF16)
    y_pool = jnp.dot(d, pw_ref[...], preferred_element_type=F32) * ps_ref[...]
    o_ref[:, 0:GROUP_W] = y_pool.astype(o_ref.dtype)

    row = lax.broadcasted_iota(I32, (SGU_CHUNK, SGU_CHUNK), 0)
    col = lax.broadcasted_iota(I32, (SGU_CHUNK, SGU_CHUNK), 1)
    ws = [jnp.where(row >= col, sw_ref[h], 0.0).astype(BF16) for h in range(N_HEADS)]
    lh = _lane_head((SGU_CHUNK, GROUP_W), 1)
    for c in range(tt // SGU_CHUNK):
        rows = slice(c * SGU_CHUNK, (c + 1) * SGU_CHUNK)
        v = sv_ref[rows, :].astype(BF16)
        mixed = _by_head([jnp.dot(ws[h], v, preferred_element_type=F32) for h in range(N_HEADS)], lh)
        y = su_ref[rows, :] * (mixed + sb_ref[...])
        o_ref[rows, GROUP_W:2 * GROUP_W] = y.astype(o_ref.dtype)


def _poolsgu(z, batch, seq, pool_wbd, pool_scale, sgu_w, sgu_bias):
    tt = _row_tile(seq, 512)
    tps = seq // tt
    zspec = lambda c: pl.BlockSpec((tt, GROUP_W), lambda b, t, c=c: (b * tps + t, c))
    return pl.pallas_call(
        _poolsgu_kernel,
        grid=(batch, tps),
        in_specs=[zspec(COL_P), zspec(COL_SU), zspec(COL_SV),
                  pl.BlockSpec((GROUP_W, GROUP_W), lambda b, t: (0, 0)),
                  pl.BlockSpec((1, GROUP_W), lambda b, t: (0, 0)),
                  pl.BlockSpec((N_HEADS, SGU_CHUNK, SGU_CHUNK), lambda b, t: (0, 0, 0)),
                  pl.BlockSpec((SGU_CHUNK, GROUP_W), lambda b, t: (0, 0))],
        out_specs=pl.BlockSpec((tt, 2 * GROUP_W), lambda b, t: (b * tps + t, 0)),
        out_shape=jax.ShapeDtypeStruct((batch * seq, 2 * GROUP_W), BF16),
        scratch_shapes=[pltpu.VMEM((HALO + tt, GROUP_W), F32)],
        compiler_params=_params("parallel", "arbitrary"),
        name="pool_sgu",
    )(z, z, z, pool_wbd, pool_scale, sgu_w, sgu_bias)


def _mlstm_kernel(q_ref, k_ref, v_ref, o_ref, g_ref, *rest):
    L = MLSTM_CHUNK
    batch = q_ref.shape[0]
    gt_refs = rest[:batch]
    gbc_ref, gbr_ref, nw_ref, y_ref, ct_ref, n_ref, m_ref = rest[batch:]

    @pl.when(pl.program_id(0) == 0)
    def _():
        ct_ref[...] = jnp.zeros(ct_ref.shape, F32)
        n_ref[...] = jnp.zeros(n_ref.shape, F32)
        m_ref[...] = jnp.zeros(m_ref.shape, F32)

    r_i = lax.broadcasted_iota(I32, (L, L), 0)
    c_i = lax.broadcasted_iota(I32, (L, L), 1)
    causal = r_i >= c_i
    tri_l = jnp.where(causal, 1.0, 0.0).astype(BF16)
    tri_u = jnp.where(r_i <= c_i, 1.0, 0.0).astype(BF16)
    lh = _lane_head((L, GROUP_W), 1)
    bd = _lane_head((GROUP_W, GROUP_W), 0) == _lane_head((GROUP_W, GROUP_W), 1)
    same_head = jnp.where(bd, 1.0, 0.0).astype(BF16)

    ct_b = [ct_ref[b] for b in range(batch)]
    n_b = [n_ref[b] for b in range(batch)]
    m_b = [[m_ref[b, h:h + 1, 0:1] for h in range(N_HEADS)] for b in range(batch)]

    def chunk(c, b):
        ct, nrow, m_heads = ct_b[b], n_b[b], m_b[b]
        rows = slice(c * L, (c + 1) * L)
        q = q_ref[b, rows, :]
        ks = k_ref[b, rows, :] * (HEAD_DIM ** -0.5)
        v = v_ref[b, rows, :]
        g = g_ref[b, rows, :] + gbc_ref[...]
        gt = gt_refs[b][:, rows] + gbr_ref[...]
        bcol = _dot_exact01(_logsig(g), tri_l, 3, x_is_lhs=False)
        brow = _dot_exact01(_logsig(gt), tri_u, 3)
        qb, kb, vb = q.astype(BF16), ks.astype(BF16), v.astype(BF16)

        s_l, inter_l, deni_l, emt_l, wsrc_l, dec_l, mnew_l = [], [], [], [], [], [], []
        for h in range(N_HEADS):
            bc = bcol[:, N_HEADS + h:N_HEADS + h + 1]
            ic = g[:, h:h + 1]
            br = brow[N_HEADS + h:N_HEADS + h + 1, :]
            ir = gt[h:h + 1, :]
            dm = jnp.where(causal, bc - br + ir, NEG)
            a = bc + m_heads[h]
            mt = jnp.maximum(a, jnp.max(dm, axis=1, keepdims=True))
            qm = jnp.where(lh == h, q, 0.0).astype(BF16)
            s = _dot_nt(qm, kb) * jnp.exp(dm - mt)
            m_new = mt[L - 1:L, :]
            s_l.append(s)
            inter_l.append(jnp.exp(a - mt))
            deni_l.append(jnp.sum(s, axis=1, keepdims=True))
            emt_l.append(jnp.exp(-mt))
            wsrc_l.append(jnp.exp(bc[L - 1:L, :] - bc + ic - m_new))
            dec_l.append(jnp.exp(a[L - 1:L, :] - m_new))
            mnew_l.append(m_new)

        r = jnp.dot(jnp.concatenate(s_l, axis=0).astype(BF16), vb, preferred_element_type=F32)
        num_i = _by_head([r[h * L:(h + 1) * L, :] for h in range(N_HEADS)], lh)
        q_c = jnp.dot(qb, ct.astype(BF16), preferred_element_type=F32)
        q_n = _dot_exact01(q * nrow, same_head, 2)
        inter = _by_head(inter_l, lh)
        num = num_i + inter * q_c
        den = _by_head(deni_l, lh) + inter * q_n
        hval = num / jnp.maximum(jnp.abs(den), _by_head(emt_l, lh))
        ms = _dot_exact01(hval * hval, same_head, 2) * (1.0 / HEAD_DIM)
        y = hval * lax.rsqrt(ms + NORM_EPS) * nw_ref[...]
        y_ref[b, rows, :] = (jax.nn.sigmoid(o_ref[b, rows, :]) * y).astype(y_ref.dtype)

        kw = ks * _by_head(wsrc_l, lh)
        dec = _by_head(dec_l, lh[0:1, :])
        upd = jnp.dot(kw.T.astype(BF16), vb, preferred_element_type=F32)
        ct_b[b] = ct * dec + jnp.where(bd, upd, 0.0)
        n_b[b] = nrow * dec + jnp.sum(kw, axis=0, keepdims=True)
        m_b[b] = mnew_l

    for c in range(q_ref.shape[1] // L):
        for b in range(batch):
            chunk(c, b)

    for b in range(batch):
        ct_ref[b] = ct_b[b]
        n_ref[b] = n_b[b]
        for h in range(N_HEADS):
            m_ref[b, h:h + 1, :] = jnp.broadcast_to(m_b[b][h], (1, m_ref.shape[2]))


def _mlstm(z, gates, gates_t, batch, seq, gate_b, norm_w):
    rows = _row_tile(seq, 2 * MLSTM_CHUNK)
    tps = seq // rows
    z3 = z.reshape(batch, seq, z.shape[1])
    zspec = lambda c: pl.BlockSpec((batch, rows, GROUP_W), lambda t, c=c: (0, t, c))
    gbc = jnp.zeros((1, GATE_W), F32).at[0, :2 * N_HEADS].set(gate_b.reshape(-1))
    gbr = gate_b.reshape(2 * N_HEADS, 1)
    state = lambda r, c: (pl.BlockSpec((batch, r, c), lambda t: (0, 0, 0)),
                          jax.ShapeDtypeStruct((batch, r, c), F32))
    (cs, csh), (ns, nsh), (msp, msh) = state(GROUP_W, GROUP_W), state(1, GROUP_W), state(8, LANE)
    gt_specs = [pl.BlockSpec((2 * N_HEADS, rows), lambda t, b=b: (0, b * tps + t)) for b in range(batch)]
    y, ct, nst, mst = pl.pallas_call(
        _mlstm_kernel,
        grid=(tps,),
        in_specs=[zspec(COL_MQ), zspec(COL_MK), zspec(COL_MV), zspec(COL_MO),
                  pl.BlockSpec((batch, rows, GATE_W), lambda t: (0, t, 0))] + gt_specs + [
                  pl.BlockSpec((1, GATE_W), lambda t: (0, 0)),
                  pl.BlockSpec((2 * N_HEADS, 1), lambda t: (0, 0)),
                  pl.BlockSpec((1, GROUP_W), lambda t: (0, 0))],
        out_specs=[pl.BlockSpec((batch, rows, GROUP_W), lambda t: (0, t, 0)), cs, ns, msp],
        out_shape=[jax.ShapeDtypeStruct((batch, seq, GROUP_W), BF16), csh, nsh, msh],
        compiler_params=_params("arbitrary"),
        name="mlstm",
    )(z3, z3, z3, z3, gates.reshape(batch, seq, GATE_W), *([gates_t] * batch), gbc, gbr, norm_w)
    return y.reshape(batch * seq, GROUP_W), ct, nst, mst


def _pick_top_blocks(s, n_valid_mask):
    blk = lax.broadcasted_iota(I32, s.shape, 0)
    s = jnp.where(n_valid_mask, s, NEG)
    bias = jnp.full(s.shape, NEG, F32)
    for _ in range(MOBA_TOPK):
        mx = jnp.max(s, axis=0, keepdims=True)
        idx = jnp.min(jnp.where(s == mx, blk, s.shape[0]), axis=0, keepdims=True)
        pick = blk == idx
        bias = jnp.where(jnp.logical_and(pick, mx > 0.5 * NEG), 0.0, bias)
        s = jnp.where(pick, NEG_PICKED, s)
    return bias


MOBA_KEY_CHUNK = 64
LOG2E = 1.4426950408889634


def _moba_logits(k_blk, qbd_sc, lt_sc, mx_sc, buf, key_le_query=None):
    B = MOBA_BLOCK
    lt = jnp.dot(k_blk[...], qbd_sc[...], preferred_element_type=F32)
    if key_le_query is not None:
        lt = jnp.where(jnp.concatenate([key_le_query] * N_HEADS, axis=1), lt, NEG)
    lt_sc[buf] = lt
    mx_sc[buf] = jnp.max(lt.reshape(B // 8, 8, N_HEADS * B), axis=0)


def _moba_softmax_pv(h, buf, vt_blk, lt_sc, mx_sc, p_sc, m_sc, l_sc, acc_sc, bias):
    B = MOBA_BLOCK
    ch = MOBA_KEY_CHUNK
    groups = ch // 8
    first = bias is None
    cols = slice(h * B, (h + 1) * B)
    cand = jnp.max(mx_sc[buf, :, cols], axis=0, keepdims=True)
    if first:
        m_new = shift = cand
    else:
        m_old = m_sc[h:h + 1, :]
        m_new = jnp.maximum(m_old, cand + bias)
        alpha = jnp.exp2(m_old - m_new)
        shift = m_new - bias
    lsum = None
    for c in range(B // ch):
        rows = slice(c * ch, (c + 1) * ch)
        p = jnp.exp2(lt_sc[buf, rows, cols] - shift)
        part = jnp.sum(p.reshape(groups, 8, B), axis=0)
        lsum = part if lsum is None else lsum + part
        p_sc[h, rows, :] = p.astype(BF16)
    lnew = jnp.sum(lsum, axis=0, keepdims=True)
    hs = slice(h * HEAD_DIM, (h + 1) * HEAD_DIM)
    pv = jnp.dot(vt_blk[hs, :], p_sc[h], preferred_element_type=F32)
    if first:
        l_sc[h:h + 1, :] = lnew
        acc_sc[hs, :] = pv
    else:
        l_sc[h:h + 1, :] = alpha * l_sc[h:h + 1, :] + lnew
        acc_sc[hs, :] = alpha * acc_sc[hs, :] + pv
    m_sc[h:h + 1, :] = m_new


def _moba_kernel(q_ref, kb_ref, vt_ref, km_ref, o_ref, qbd_sc, sel_sc, m_sc, l_sc, acc_sc, lt_sc, mx_sc, p_sc):
    i = pl.program_id(1)
    nb = km_ref.shape[0]
    nbp = sel_sc.shape[0] // N_HEADS
    B = MOBA_BLOCK
    q = q_ref[...]
    km = km_ref[...]
    past =lax.broadcasted_iota(I32, (nb, B), 0) < i
    key_le_query = lax.broadcasted_iota(I32, (B, B), 0) <= lax.broadcasted_iota(I32, (B, B), 1)
    stats = (lt_sc, mx_sc, p_sc, m_sc, l_sc, acc_sc)

    q_t = q.T
    q_ts = q_t * (HEAD_DIM ** -0.5 * LOG2E)
    row_head = _lane_head((GROUP_W, B), 0)
    lh_nb = _lane_head((nb, GROUP_W), 1)
    km_heads = jnp.concatenate([jnp.where(lh_nb == h, km, 0.0) for h in range(N_HEADS)], axis=0)
    s_all = jnp.dot(km_heads, q_t, precision=HIGHEST, preferred_element_type=F32)
    for h in range(N_HEADS):
        sel_sc[h * nbp:h * nbp + nb, :] = _pick_top_blocks(s_all[h * nb:(h + 1) * nb, :], past)
        qbd_sc[:, h * B:(h + 1) * B] = jnp.where(row_head == h, q_ts, 0.0).astype(BF16)

    _moba_logits(kb_ref.at[i], qbd_sc, lt_sc, mx_sc, 1, key_le_query)
    _moba_logits(kb_ref.at[0], qbd_sc, lt_sc, mx_sc, 0)
    for h in range(N_HEADS):
        _moba_softmax_pv(h, 1, vt_ref.at[i], *stats, None)

    def step(n, buf):
        _moba_logits(kb_ref.at[jnp.minimum(n + 1, nb - 1)], qbd_sc, lt_sc, mx_sc, 1 - buf)
        for h in range(N_HEADS):
            bias = sel_sc[pl.ds(h * nbp + n, 1), :]
            _moba_softmax_pv(h, buf, vt_ref.at[n], *stats, bias)

    def body(k, carry):
        step(2 * k, 0)
        step(2 * k + 1, 1)
        return carry

    lax.fori_loop(0, (i + 1) // 2, body, 0)

    for h in range(N_HEADS):
        hs = slice(h * HEAD_DIM, (h + 1) * HEAD_DIM)
        acc_sc[hs, :] = acc_sc[hs, :] / l_sc[h:h + 1, :]
    o_ref[...] = acc_sc[...].T.astype(o_ref.dtype)


def _moba(q, kb4, vt4, kmean, batch, nb):
    nbp = -(-nb // 8) * 8
    B = MOBA_BLOCK
    blk4 = pl.BlockSpec((None, nb, B, GROUP_W), lambda b, i: (b, 0, 0, 0))
    return pl.pallas_call(
        _moba_kernel,
        grid=(batch, nb),
        in_specs=[pl.BlockSpec((B, GROUP_W), lambda b, i: (b * nb + i, 0)),
                  blk4, blk4,
                  pl.BlockSpec((None, nb, GROUP_W), lambda b, i: (b, 0, 0))],
        out_specs=pl.BlockSpec((B, GROUP_W), lambda b, i: (b * nb + i, 0)),
        out_shape=jax.ShapeDtypeStruct((batch * nb * B, GROUP_W), BF16),
        scratch_shapes=[pltpu.VMEM((GROUP_W, N_HEADS * B), BF16),
                        pltpu.VMEM((N_HEADS * nbp, B), F32),
                        pltpu.VMEM((8, B), F32), pltpu.VMEM((8, B), F32),
                        pltpu.VMEM((GROUP_W, B), F32),
                        pltpu.VMEM((2, B, N_HEADS * B), F32),
                        pltpu.VMEM((2, 8, N_HEADS * B), F32),
                        pltpu.VMEM((N_HEADS, B, B), BF16)],
        compiler_params=_params("parallel", "arbitrary"),
        name="moba",
    )(q, kb4, vt4, kmean)


def _outproj_kernel(x_ref, ma_ref, mc_ref, md_ref, w_ref, o_ref):
    g2 = 2 * GROUP_W
    acc = jnp.dot(ma_ref[...], w_ref[0:g2, :], preferred_element_type=F32)
    acc += jnp.dot(mc_ref[...], w_ref[g2:g2 + GROUP_W, :], preferred_element_type=F32)
    acc += jnp.dot(md_ref[...], w_ref[g2 + GROUP_W:, :], preferred_element_type=F32)
    o_ref[...] = x_ref[...] + acc


def _outproj(x, ma, mc, md, w):
    n, d = x.shape
    tm = _row_tile(n, 512)
    rows = lambda c: pl.BlockSpec((tm, c), lambda i: (i, 0))
    return pl.pallas_call(
        _outproj_kernel,
        grid=(n // tm,),
        in_specs=[rows(d), rows(2 * GROUP_W), rows(GROUP_W), rows(GROUP_W),
                  pl.BlockSpec(w.shape, lambda i: (0, 0))],
        out_specs=rows(d),
        out_shape=jax.ShapeDtypeStruct((n, d), F32),
        compiler_params=_params("parallel"),
        name="outproj",
    )(x, ma, mc, md, w)


def _ff_tile(d_ff):
    return d_ff // 2 if (d_ff // 2) % LANE == 0 else d_ff


def _swiglu_partial(xn, w1_ref, w3_ref, w2_ref):
    a = jnp.dot(xn, w1_ref[...], preferred_element_type=F32)
    b = jnp.dot(xn, w3_ref[...], preferred_element_type=F32)
    hmid = (a * jax.nn.sigmoid(a) * b).astype(BF16)
    return jnp.dot(hmid, w2_ref[...], preferred_element_type=F32)


def _swiglu_full(xn, w1_ref, w3_ref, w2_ref):
    d_ff = w1_ref.shape[1]
    tf = _ff_tile(d_ff)
    acc = None
    for jj in range(d_ff // tf):
        sl = slice(jj * tf, (jj + 1) * tf)
        part = _swiglu_partial(xn, w1_ref.at[:, sl], w3_ref.at[:, sl], w2_ref.at[sl, :])
        acc = part if acc is None else acc + part
    return acc


def _ffn_kernel(x_ref, nw_ref, w1_ref, w3_ref, w2_ref, o_ref):
    x = x_ref[...]
    ms = jnp.mean(x * x, axis=-1, keepdims=True)
    xn = (x * lax.rsqrt(ms + NORM_EPS) * nw_ref[...]).astype(BF16)
    o_ref[...] = x + _swiglu_full(xn, w1_ref, w3_ref, w2_ref)


def _ffn(x, nw, w1, w3, w2):
    n, d = x.shape
    tm = _row_tile(n, 512)
    resident = lambda a: pl.BlockSpec(a.shape, lambda i: (0, 0), pipeline_mode=pl.Buffered(1))
    return pl.pallas_call(
        _ffn_kernel,
        grid=(n // tm,),
        in_specs=[pl.BlockSpec((tm, d), lambda i: (i, 0)),
                  pl.BlockSpec((1, d), lambda i: (0, 0)),
                  resident(w1), resident(w3), resident(w2)],
        out_specs=pl.BlockSpec((tm, d), lambda i: (i, 0)),
        out_shape=jax.ShapeDtypeStruct((n, d), F32),
        compiler_params=_params("parallel"),
        name="ffn_dense",
    )(x, nw, w1, w3, w2)


def _router_kernel(x_ref, nw_ref, rw_ref, rb_ref, xp_ref, g_ref, e_ref):
    x = x_ref[...]
    ms = jnp.mean(x * x, axis=-1, keepdims=True)
    xn = x * lax.rsqrt(ms + NORM_EPS) * nw_ref[...]
    tm, d = x.shape
    half = d // 2
    x_hi = xn.astype(BF16)
    bits = pltpu.bitcast(x_hi.astype(F32), jnp.uint32)
    packed = jnp.bitwise_or(jnp.right_shift(bits[:, :half], jnp.uint32(16)), bits[:, half:])
    nc = half // LANE
    for c in range(nc):
        xp_ref[pl.ds(c, tm, stride=nc), :] = packed[:, c * LANE:(c + 1) * LANE]

    rw = rw_ref[...]
    w_hi = rw.astype(BF16)
    w_lo = (rw - w_hi.astype(F32)).astype(BF16)
    x_lo = (xn - x_hi.astype(F32)).astype(BF16)
    dot = lambda a, b: jnp.dot(a, b, preferred_element_type=F32)
    logits = dot(x_hi, w_hi) + (dot(x_hi, w_lo) + dot(x_lo, w_hi)) + rb_ref[...]
    lane = lax.broadcasted_iota(I32, logits.shape, 1)
    mx1 = jnp.max(logits, axis=1, keepdims=True)
    i1 = jnp.min(jnp.where(logits == mx1, lane, LANE), axis=1, keepdims=True)
    rest = jnp.where(lane == i1, -jnp.inf, logits)
    mx2 = jnp.max(rest, axis=1, keepdims=True)
    i2 = jnp.min(jnp.where(rest == mx2, lane, LANE), axis=1, keepdims=True)
    e2 = jnp.exp(mx2 - mx1)
    g1 = 1.0 / (1.0 + e2)
    g_ref[...] = jnp.where(lane == 0, g1, jnp.where(lane == 1, e2 * g1, 0.0))
    e_ref[...] = jnp.where(lane == 0, i1, jnp.where(lane == 1, i2, 0))


def _router(x, nw, rw, rb):
    n, d = x.shape
    tm = _row_tile(n, 512)
    rows = lambda c: pl.BlockSpec((tm, c), lambda i: (i, 0))
    full = lambda a: pl.BlockSpec(a.shape, lambda i: (0, 0))
    return pl.pallas_call(
        _router_kernel,
        grid=(n // tm,),
        in_specs=[rows(d), full(nw), full(rw), full(rb)],
        out_specs=[pl.BlockSpec((tm * (d // 2 // LANE), LANE), lambda i: (i, 0)), rows(LANE), rows(LANE)],
        out_shape=[jax.ShapeDtypeStruct((n * (d // 2 // LANE), LANE), jnp.uint32),
                   jax.ShapeDtypeStruct((n, LANE), F32),
                   jax.ShapeDtypeStruct((n, LANE), I32)],
        compiler_params=_params("parallel"),
        name="router",
    )(x, nw, rw, rb)


def _unpack_rows(words):
    lo = pltpu.bitcast(jnp.left_shift(words, jnp.uint32(16)), F32)
    hi = pltpu.bitcast(jnp.bitwise_and(words, jnp.uint32(0xFFFF0000)), F32)
    return jnp.concatenate([lo, hi], axis=1).astype(BF16)


def _gmm_kernel(te_ref, src_ref, nused_ref, xp_hbm, w1_ref, w3_ref, w2_ref, o_ref, xbuf, sem):
    i = pl.program_id(0)
    ntiles = pl.num_programs(0)
    d = w1_ref.shape[0]
    oc = d // LANE
    tm = o_ref.shape[0] // oc
    nc = xbuf.shape[1] // tm
    slot = i % 2

    def issue(tile, sl):
        def body(g, carry):
            for u in range(ISSUE_UNROLL):
                r = g * ISSUE_UNROLL + u
                tok = src_ref[tile * tm + r]
                pltpu.make_async_copy(xp_hbm.at[pl.ds(pl.multiple_of(tok * nc, nc), nc), :],
                                      xbuf.at[sl, pl.ds(pl.multiple_of(r * nc, nc), nc), :],
                                      sem.at[sl]).start(priority=u % 2)
            return carry
        lax.fori_loop(0, tm // ISSUE_UNROLL, body, 0)

    @pl.when(i == 0)
    def _():
        issue(0, 0)

    @pl.when(i + 1 < ntiles)
    def _():
        issue(i + 1, 1 - slot)

    pltpu.make_async_copy(xbuf.at[slot], xbuf.at[slot], sem.at[slot]).wait()

    @pl.when(i < nused_ref[0])
    def _():
        words = jnp.concatenate([xbuf[slot, pl.ds(c, tm, stride=nc), :] for c in range(nc)], axis=1)
        acc = _swiglu_full(_unpack_rows(words), w1_ref, w3_ref, w2_ref)
        for c in range(oc):
            o_ref[pl.ds(c, tm, stride=oc), :] = acc[:, c * LANE:(c + 1) * LANE]

    @pl.when(i >= nused_ref[0])
    def _():
        o_ref[...] = jnp.zeros(o_ref.shape, F32)


def _gmm(xp, tile_expert, src, nused, w1, w3, w2, tm):
    ncap = src.shape[0]
    d = w1.shape[1]
    half = d // 2
    d_ff = w1.shape[2]
    oc = d // LANE
    expert = lambda r, c: pl.BlockSpec((None, r, c), lambda i, te, s, nu: (te[i], 0, 0))
    grid_spec = pltpu.PrefetchScalarGridSpec(
        num_scalar_prefetch=3,
        grid=(ncap // tm,),
        in_specs=[pl.BlockSpec(memory_space=pl.ANY), expert(d, d_ff), expert(d, d_ff), expert(d_ff, d)],
        out_specs=pl.BlockSpec((tm * oc, LANE), lambda i, te, s, nu: (i, 0)),
        scratch_shapes=[pltpu.VMEM((2, tm * (half // LANE), LANE), jnp.uint32),
                        pltpu.SemaphoreType.DMA((2,))])
    return pl.pallas_call(
        _gmm_kernel,
        grid_spec=grid_spec,
        out_shape=jax.ShapeDtypeStruct((ncap * oc, LANE), F32),
        compiler_params=_params("arbitrary"),
        name="moe_gmm",
    )(tile_expert, src, nused, xp, w1, w3, w2)


def _dispatch_plan(e_idx, n_experts, tm):
    n = e_idx.shape[0]
    ef = e_idx[:, :TOP_K].T.reshape(-1)
    onehot = (ef[:, None] == jnp.arange(n_experts, dtype=I32)[None, :]).astype(I32)
    rank = jnp.sum((jnp.cumsum(onehot, axis=0) - 1) * onehot, axis=1)
    counts = jnp.sum(onehot, axis=0)
    padded = ((counts + tm - 1) // tm) * tm
    ends = jnp.cumsum(padded)
    pos = (ends - padded)[ef] + rank
    ncap = TOP_K * n + n_experts * tm
    src = jnp.zeros((ncap,), I32).at[pos].set(jnp.arange(TOP_K * n, dtype=I32) % n,
                                              unique_indices=True, mode="promise_in_bounds")
    tile_start = jnp.arange(ncap // tm, dtype=I32) * tm
    tile_expert = jnp.minimum(jnp.searchsorted(ends, tile_start, side="right"), n_experts - 1).astype(I32)
    nused = (ends[-1] // tm).astype(I32).reshape(1)
    return pos.astype(I32), src, tile_expert, nused


def _combine_kernel(pos_ref, h_ref, g_ref, o_hbm, fw_ref, y_ref, obuf, sem, *, final_norm):
    i = pl.program_id(0)
    ntiles = pl.num_programs(0)
    tc, d = h_ref.shape
    oc = d // LANE
    n = ntiles * tc
    slot = i % 2

    def issue(tile, sl):
        def body(g, carry):
            for u in range(ISSUE_UNROLL):
                r = g * ISSUE_UNROLL + u
                for k in range(TOP_K):
                    p = pos_ref[k * n + tile * tc + r]
                    pltpu.make_async_copy(o_hbm.at[pl.ds(pl.multiple_of(p * oc, oc), oc), :],
                                          obuf.at[sl, k, pl.ds(pl.multiple_of(r * oc, oc), oc), :],
                                          sem.at[sl]).start(priority=k)
            return carry
        lax.fori_loop(0, tc // ISSUE_UNROLL, body, 0)

    @pl.when(i == 0)
    def _():
        issue(0, 0)

    @pl.when(i + 1 < ntiles)
    def _():
        issue(i + 1, 1 - slot)

    pltpu.make_async_copy(obuf.at[slot], obuf.at[slot], sem.at[slot]).wait()
    g = g_ref[...]
    rows = lambda k: jnp.concatenate([obuf[slot, k, pl.ds(c, tc, stride=oc), :] for c in range(oc)], axis=1)
    y = h_ref[...] + g[:, 0:1] * rows(0) + g[:, 1:2] * rows(1)
    if final_norm:
        ms = jnp.mean(y * y, axis=-1, keepdims=True)
        y = y * lax.rsqrt(ms + NORM_EPS) * fw_ref[...]
    y_ref[...] = y


def _combine(pos, h, gates, o_sorted, fw, final_norm):
    n, d = h.shape
    tc = _row_tile(n, 256)
    grid_spec = pltpu.PrefetchScalarGridSpec(
        num_scalar_prefetch=1,
        grid=(n // tc,),
        in_specs=[pl.BlockSpec((tc, d), lambda i, p: (i, 0)),
                  pl.BlockSpec((tc, LANE), lambda i, p: (i, 0)),
                  pl.BlockSpec(memory_space=pl.ANY),
                  pl.BlockSpec((1, d), lambda i, p: (0, 0))],
        out_specs=pl.BlockSpec((tc, d), lambda i, p: (i, 0)),
        scratch_shapes=[pltpu.VMEM((2, TOP_K, tc * (d // LANE), LANE), F32), pltpu.SemaphoreType.DMA((2,))])
    return pl.pallas_call(
        functools.partial(_combine_kernel, final_norm=final_norm),
        grid_spec=grid_spec,
        out_shape=jax.ShapeDtypeStruct((n, d), F32),
        compiler_params=_params("arbitrary"),
        name="moe_combine",
    )(pos, h, gates, o_sorted, fw)


def _moe_small_kernel(x_ref, nw_ref, g_ref, e_ref, w1_ref, w3_ref, w2_ref, fw_ref, y_ref,
                      xn_ref, acc_ref, *, final_norm):
    e = pl.program_id(0)
    j = pl.program_id(1)

    @pl.when(jnp.logical_and(e == 0, j == 0))
    def _():
        x = x_ref[...]
        ms = jnp.mean(x * x, axis=-1, keepdims=True)
        xn_ref[...] = (x * lax.rsqrt(ms + NORM_EPS) * nw_ref[...]).astype(BF16)
        acc_ref[...] = jnp.zeros(acc_ref.shape, F32)

    g = g_ref[...]
    idx = e_ref[...]
    gate = (jnp.where(idx[:, 0:1] == e, g[:, 0:1], 0.0) + jnp.where(idx[:, 1:2] == e, g[:, 1:2], 0.0))
    acc_ref[...] += gate * _swiglu_partial(xn_ref[...], w1_ref, w3_ref, w2_ref)

    @pl.when(jnp.logical_and(e == pl.num_programs(0) - 1, j == pl.num_programs(1) - 1))
    def _():
        y = x_ref[...] + acc_ref[...]
        if final_norm:
            ms = jnp.mean(y * y, axis=-1, keepdims=True)
            y = y * lax.rsqrt(ms + NORM_EPS) * fw_ref[...]
        y_ref[...] = y


def _moe_small(x, nw, gates, e_idx, w1, w3, w2, fw, final_norm):
    n, d = x.shape
    n_experts, _, d_ff = w1.shape
    tf = _ff_tile(d_ff)
    full = lambda a: pl.BlockSpec(a.shape, lambda e, j: (0, 0))
    return pl.pallas_call(
        functools.partial(_moe_small_kernel, final_norm=final_norm),
        grid=(n_experts, d_ff // tf),
        in_specs=[full(x), full(nw), full(gates), full(e_idx),
                  pl.BlockSpec((None, d, tf), lambda e, j: (e, 0, j)),
                  pl.BlockSpec((None, d, tf), lambda e, j: (e, 0, j)),
                  pl.BlockSpec((None, tf, d), lambda e, j: (e, j, 0)),
                  full(fw)],
        out_specs=full(x),
        out_shape=jax.ShapeDtypeStruct((n, d), F32),
        scratch_shapes=[pltpu.VMEM((n, d), BF16), pltpu.VMEM((n, d), F32)],
        compiler_params=_params("arbitrary", "arbitrary"),
        name="moe_small",
    )(x, nw, gates, e_idx, w1, w3, w2, fw)


def _norm_kernel(x_ref, w_ref, o_ref):
    x = x_ref[...]
    ms = jnp.mean(x * x, axis=-1, keepdims=True)
    o_ref[...] = x * lax.rsqrt(ms + NORM_EPS) * w_ref[...]


def _final_norm(x, w):
    n, d = x.shape
    tm = _row_tile(n, 512)
    return pl.pallas_call(
        _norm_kernel,
        grid=(n // tm,),
        in_specs=[pl.BlockSpec((tm, d), lambda i: (i, 0)), pl.BlockSpec((1, d), lambda i: (0, 0))],
        out_specs=pl.BlockSpec((tm, d), lambda i: (i, 0)),
        out_shape=jax.ShapeDtypeStruct((n, d), F32),
        compiler_params=_params("parallel"),
        name="final_norm",
    )(x, w)


def _smix_kernel(p_ref, pf_ref, pw_ref, ps_ref, su_ref, sv_ref, sw0_ref, sb0_ref,
                 q_ref, k_ref, v_ref, ig_ref, fg_ref, c_ref, n_ref, m_ref, mo_ref, nw_ref,
                 ypool_ref, ysgu_ref, yml_ref, cn_ref, nn_ref, mn_ref, *, cnt):
    p = p_ref[...]
    lane_w = _window_of_lane(p.shape, 1, 0)
    acc = p
    wsum = jnp.zeros(p.shape, F32)
    for off in range(1, POOL_WINDOWS[-1] + 1):
        if off in POOL_WINDOWS:
            wsum = jnp.where(lane_w == off, acc, wsum)
        if off <= POOL_BUF:
            acc = acc + pf_ref[HALO - off]
    count = jnp.minimum(lane_w, cnt).astype(F32)
    d = (wsum / count - p).astype(BF16)
    ypool_ref[...] = jnp.dot(d, pw_ref[...], preferred_element_type=F32) * ps_ref[...]

    ysgu_ref[...] = su_ref[...] * (sw0_ref[...] * sv_ref[...] + sb0_ref[...])

    q = q_ref[...]
    ks = k_ref[...] * (HEAD_DIM ** -0.5)
    v = v_ref[...]
    ig = ig_ref[...]
    c = c_ref[...]
    nrow = n_ref[...]
    a = _logsig(fg_ref[...]) + m_ref[...]
    mt = jnp.maximum(a, ig)
    inter = jnp.exp(a - mt)
    e_i = jnp.exp(ig - mt)
    s = jnp.sum(q * ks, axis=-1, keepdims=True) * e_i
    c_q = jnp.sum(c * q, axis=-1, keepdims=True)
    n_q = jnp.sum(nrow * q, axis=-1, keepdims=True)
    num = s * v + inter * c_q
    den = s + inter * n_q
    hval = num / jnp.maximum(jnp.abs(den), jnp.exp(-mt))
    cn_ref[...] = inter * c + (e_i * v) * ks
    nn_ref[...] = inter * nrow + e_i * ks
    mn_ref[...] = mt
    ms = jnp.mean(hval * hval, axis=1, keepdims=True)
    y = hval * lax.rsqrt(ms + NORM_EPS) * nw_ref[...]
    yml_ref[...] = jax.nn.sigmoid(mo_ref[...]) * y


def _smix(args, out_shapes, cnt):
    return pl.pallas_call(
        functools.partial(_smix_kernel, cnt=cnt),
        out_shape=out_shapes,
        compiler_params=pltpu.CompilerParams(vmem_limit_bytes=VMEM_LIMIT),
        name="sample_mixers",
    )(*args)


PAGES_PER_STEP = 64
PAGES_PER_BLOCK = MOBA_BLOCK // PAGE_SIZE


def _paged_kmean_kernel(pt_ref, *refs):
    o_ref = refs[-1]
    j = pl.program_id(1)
    nblk = PAGES_PER_STEP // PAGES_PER_BLOCK

    @pl.when(j == 0)
    def _():
        o_ref[...] = jnp.zeros(o_ref.shape, F32)

    lane = lax.broadcasted_iota(I32, o_ref.shape, 1)
    acc = o_ref[...]
    for bb in range(nblk):
        s = refs[bb * PAGES_PER_BLOCK][...]
        for pg in range(1, PAGES_PER_BLOCK):
            s = s + refs[bb * PAGES_PER_BLOCK + pg][...]
        col = jnp.sum(s, axis=1, keepdims=True) * (1.0 / MOBA_BLOCK)
        acc = jnp.where(lane == j * nblk + bb, col, acc)
    o_ref[...] = acc


def _paged_kmean(cache_t, page_table, page_base):
    bs, n_pages = page_table.shape
    nb = n_pages // PAGES_PER_BLOCK
    page = lambda s: pl.BlockSpec((None, GROUP_W, PAGE_SIZE),
                                  lambda b, j, pt, s=s: (page_base + pt[b, j * PAGES_PER_STEP + s], 0, 0))
    grid_spec = pltpu.PrefetchScalarGridSpec(
        num_scalar_prefetch=1,
        grid=(bs, n_pages // PAGES_PER_STEP),
        in_specs=[page(s) for s in range(PAGES_PER_STEP)],
        out_specs=pl.BlockSpec((None, GROUP_W, nb), lambda b, j, pt: (b, 0, 0)))
    return pl.pallas_call(
        _paged_kmean_kernel,
        grid_spec=grid_spec,
        out_shape=jax.ShapeDtypeStruct((bs, GROUP_W, nb), F32),
        compiler_params=_params("parallel", "arbitrary"),
        name="paged_kmean",
    )(page_table, *([cache_t] * PAGES_PER_STEP))


def _sample_select_kernel(q_ref, kmt_ref, o_ref):
    head_row = lax.broadcasted_iota(I32, (8, GROUP_W), 0) == _lane_head((8, GROUP_W), 1)
    for b in range(q_ref.shape[0]):
        q8 = jnp.where(head_row, q_ref[b], 0.0)
        s = jnp.dot(q8, kmt_ref[b], precision=HIGHEST, preferred_element_type=F32)
        blk = lax.broadcasted_iota(I32, s.shape, 1)
        slot = lax.broadcasted_iota(I32, o_ref.shape[1:], 1)
        out = jnp.zeros(o_ref.shape[1:], I32)
        for r in range(MOBA_TOPK):
            mx = jnp.max(s, axis=1, keepdims=True)
            idx = jnp.min(jnp.where(s == mx, blk, s.shape[1]), axis=1, keepdims=True)
            out = jnp.where(slot == r, idx, out)
            s = jnp.where(blk == idx, NEG_PICKED, s)
        o_ref[b] = out


def _sample_select(q3, kmean_t):
    bs, _, nb = kmean_t.shape
    sb = _row_tile(bs, 8)
    return pl.pallas_call(
        _sample_select_kernel,
        grid=(bs // sb,),
        in_specs=[pl.BlockSpec((sb, 1, GROUP_W), lambda b: (b, 0, 0)),
                  pl.BlockSpec((sb, GROUP_W, nb), lambda b: (b, 0, 0))],
        out_specs=pl.BlockSpec((sb, 8, LANE), lambda b: (b, 0, 0)),
        out_shape=jax.ShapeDtypeStruct((bs, 8, LANE), I32),
        compiler_params=_params("parallel"),
        name="sample_select",
    )(q3, kmean_t)


N_SEL_PAGES = MOBA_TOPK * PAGES_PER_BLOCK


def _sample_attn_kernel(ph_ref, q_ref, kn_ref, vn_ref, *refs):
    n_pg = N_HEADS * N_SEL_PAGES
    kp, vp, o_ref = refs[:n_pg], refs[n_pg:2 * n_pg], refs[-1]
    for h in range(N_HEADS):
        hs = slice(h * HEAD_DIM, (h + 1) * HEAD_DIM)
        pages = slice(h * N_SEL_PAGES, (h + 1) * N_SEL_PAGES)
        q = q_ref[hs, :] * (HEAD_DIM ** -0.5)
        own = jnp.sum(kn_ref[hs, :] * q, axis=0, keepdims=True)
        logits = [jnp.sum(r[...] * q, axis=0, keepdims=True) for r in kp[pages]]
        m = own
        for lg in logits:
            m = jnp.maximum(m, jnp.max(lg, axis=1, keepdims=True))
        p_own = jnp.exp(own - m)
        den = p_own
        acc = p_own * vn_ref[hs, :]
        for lg, vr in zip(logits, vp[pages]):
            p = jnp.exp(lg - m)
            den = den + jnp.sum(p, axis=1, keepdims=True)
            acc = acc + jnp.sum(vr[...] * p, axis=1, keepdims=True)
        o_ref[hs, :] = acc / den


def _sample_attn(phys, q_col, kn_col, vn_col, cache_kt, cache_vt, page_base):
    bs = q_col.shape[0]
    col = pl.BlockSpec((None, GROUP_W, 1), lambda b, ph: (b, 0, 0))
    page = lambda h, s: pl.BlockSpec(
        (None, HEAD_DIM, PAGE_SIZE),
        lambda b, ph, h=h, s=s: (page_base + ph[(b * N_HEADS + h) * N_SEL_PAGES + s], h, 0))
    pages = [page(h, s) for h in range(N_HEADS) for s in range(N_SEL_PAGES)]
    grid_spec = pltpu.PrefetchScalarGridSpec(
        num_scalar_prefetch=1,
        grid=(bs,),
        in_specs=[col, col, col] + pages + pages,
        out_specs=col)
    return pl.pallas_call(
        _sample_attn_kernel,
        grid_spec=grid_spec,
        out_shape=jax.ShapeDtypeStruct((bs, GROUP_W, 1), F32),
        compiler_params=_params("parallel"),
        name="sample_attn",
    )(phys, q_col, kn_col, vn_col, *([cache_kt] * len(pages)), *([cache_vt] * len(pages)))


def _prep_layer(w_in, w_out, pool_w, pool_scale, sgu_w, sgu_b, norm_w):
    d = w_in.shape[0]
    g = GROUP_W
    gate0 = 7 * g
    att0 = gate0 + 2 * N_HEADS
    pad = jnp.zeros((d, GATE_W - 2 * N_HEADS), w_in.dtype)
    w_z = jnp.concatenate([w_in[:, :gate0], w_in[:, att0:], w_in[:, gate0:att0], pad], axis=1).astype(BF16)
    wbd = jnp.zeros((g, g), F32)
    for i in range(len(POOL_WINDOWS)):
        sl = slice(i * POOL_GROUP, (i + 1) * POOL_GROUP)
        wbd = wbd.at[sl, sl].set(pool_w[i])
    return dict(
        w_z=w_z, w_out=w_out.astype(BF16), pool_wbd=wbd.astype(BF16),
        pool_scale=pool_scale.reshape(1, g),
        sgu_w=sgu_w, sgu_bias=jnp.repeat(sgu_b.T, HEAD_DIM, axis=1),
        sgu_w0=jnp.repeat(sgu_w[:, 0, 0], HEAD_DIM).reshape(1, g),
        sgu_b0=jnp.repeat(sgu_b[:, 0], HEAD_DIM).reshape(1, g),
        norm_w=norm_w.reshape(1, g))


def _head_cols(a, bs):
    return a.reshape(bs * N_HEADS, HEAD_DIM, 1)


def _head_rows(a, bs):
    return a.reshape(bs * N_HEADS, 1, HEAD_DIM)


def _mixers_prompt(x, batch, seq, nmw, lp, gate_b, rope_tabs, layer, depth, knew, vnew):
    g = GROUP_W
    nb = seq // MOBA_BLOCK
    zm, q, gates, gates_t, kb, vt, kmean, knew, vnew = _inproj_prompt(
        x, nmw, lp["w_z"], rope_tabs, batch, seq, layer, depth, knew, vnew)
    m_ab = _poolsgu(zm, batch, seq, lp["pool_wbd"], lp["pool_scale"], lp["sgu_w"], lp["sgu_bias"])
    m_c, ct, nst, mst = _mlstm(zm, gates, gates_t, batch, seq, gate_b, lp["norm_w"])
    blocks = lambda a: a.reshape(batch, nb, MOBA_BLOCK, g)
    m_d = _moba(q, blocks(kb), blocks(vt), kmean, batch, nb)
    h = _outproj(x, m_ab, m_c, m_d, lp["w_out"])

    c_new = jnp.stack([ct[:, i * HEAD_DIM:(i + 1) * HEAD_DIM, i * HEAD_DIM:(i + 1) * HEAD_DIM]
                       for i in range(N_HEADS)], axis=1)
    state = (zm[:, :g].reshape(batch, seq, g)[:, seq - POOL_BUF:],
             jnp.swapaxes(c_new, -1, -2), nst.reshape(batch, N_HEADS, HEAD_DIM), mst[:, :N_HEADS, 0])
    return h, state, knew, vnew


def _mixers_sample(x, past_len, nmw, lp, gate_b, rope_tabs, pool_state, c_st, n_st, m_st,
                   cache_k, cache_v, page_table, page_base):
    bs = x.shape[0]
    g = GROUP_W
    z = _inproj(x, nmw, lp["w_z"], rope_tabs)
    col = lambda c: z[:, c * g:(c + 1) * g]
    p, sv = col(COL_P), col(COL_SV)
    gates = z[:, 10 * g:10 * g + 2 * N_HEADS] + gate_b.reshape(1, -1)
    bh = bs * N_HEADS
    prefix = jnp.concatenate([jnp.zeros((1, bs, g), F32), jnp.swapaxes(pool_state, 0, 1)], axis=0)
    args = (p, prefix, lp["pool_wbd"], lp["pool_scale"], col(COL_SU), sv, lp["sgu_w0"], lp["sgu_b0"],
            _head_rows(col(COL_MQ), bs), _head_rows(col(COL_MK), bs), _head_cols(col(COL_MV), bs),
            gates[:, :N_HEADS].reshape(bh, 1, 1), gates[:, N_HEADS:].reshape(bh, 1, 1),
            c_st.reshape(bh, HEAD_DIM, HEAD_DIM), n_st.reshape(bh, 1, HEAD_DIM), m_st.reshape(bh, 1, 1),
            _head_cols(col(COL_MO), bs), _head_cols(jnp.tile(lp["norm_w"], (bs, 1)), bs))
    sds = jax.ShapeDtypeStruct
    outs = (sds((bs, g), F32), sds((bs, g), F32), sds((bh, HEAD_DIM, 1), F32),
            sds((bh, HEAD_DIM, HEAD_DIM), F32), sds((bh, 1, HEAD_DIM), F32), sds((bh, 1, 1), F32))
    y_pool, y_sgu, y_ml, c_new, n_new, m_new = _smix(args, outs, min(past_len + 1, POOL_WINDOWS[-1]))

    q_col, k_col, v_col = (col(c).reshape(bs, g, 1) for c in (COL_AQ, COL_AK, COL_AV))
    kmean_t = _paged_kmean(cache_k, page_table, page_base)
    sel = _sample_select(col(COL_AQ).reshape(bs, 1, g), kmean_t)[:, :N_HEADS, :MOBA_TOPK]
    pages = sel[..., None] * PAGES_PER_BLOCK + jnp.arange(PAGES_PER_BLOCK, dtype=I32)
    phys = jnp.take_along_axis(page_table, pages.reshape(bs, -1), axis=1).reshape(-1)
    y_at = _sample_attn(phys, q_col, k_col, v_col, cache_k, cache_v, page_base).reshape(bs, g)

    m_ab = jnp.concatenate([y_pool, y_sgu], axis=1).astype(BF16)
    h = _outproj(x, m_ab, y_ml.reshape(bs, g).astype(BF16), y_at.astype(BF16), lp["w_out"])

    heads = lambda a: a.reshape(bs, 1, N_HEADS, HEAD_DIM)
    state = (heads(col(COL_AK)), heads(col(COL_AV)),
             jnp.concatenate([pool_state[:, 1:], p[:, None, :]], axis=1), sv.reshape(bs, 1, g),
             c_new.reshape(bs, N_HEADS, HEAD_DIM, HEAD_DIM), n_new.reshape(bs, N_HEADS, HEAD_DIM),
             m_new.reshape(bs, N_HEADS))
    return h, state


def kernel(x_prompt, x_sample, cache_k, cache_v, page_table, state_pool, state_mlstm_c, state_mlstm_n,
           state_mlstm_m, norm_mix_w, norm_ffn_w, final_norm_w, w_in, w_out, pool_w, pool_scale, sgu_w,
           sgu_b, mlstm_gate_b, mlstm_norm_w, ffn_w1, ffn_w3, ffn_w2, router_w, router_b,
           moe_w1, moe_w3, moe_w2):
    batch, seq, d = x_prompt.shape
    bs = x_sample.shape[0]
    depth = w_in.shape[0]
    n_pool = cache_k.shape[1]
    n_pages = page_table.shape[1]
    past_len = n_pages * PAGE_SIZE
    n_experts = router_w.shape[-1]
    assert x_sample.shape[1] == 1 and d == N_MIXERS * GROUP_W
    assert seq % MOBA_BLOCK == 0 and seq % (2 * MLSTM_CHUNK) == 0 and seq >= POOL_BUF
    assert past_len % MOBA_BLOCK == 0 and past_len // MOBA_BLOCK >= MOBA_TOPK
    assert n_pages % PAGES_PER_STEP == 0

    rope_p = _rope_tables(jnp.arange(seq, dtype=I32))
    rope_s = _rope_tables(jnp.full((bs,), past_len, I32))
    pages_t = lambda c: jnp.transpose(c, (0, 1, 3, 4, 2)).reshape(depth * n_pool, GROUP_W, PAGE_SIZE)
    ck, cv = pages_t(cache_k), pages_t(cache_v)
    fw = final_norm_w.reshape(1, d)
    tm_moe = 512

    hp = x_prompt.reshape(batch * seq, d)
    hs = x_sample.reshape(bs, d)
    st_p, st_s = [], []
    knew = jnp.zeros((depth, batch, GROUP_W, seq), F32)
    vnew = jnp.zeros((depth, batch, GROUP_W, seq), F32)
    for l in range(depth):
        lp = _prep_layer(w_in[l], w_out[l], pool_w[l], pool_scale[l], sgu_w[l], sgu_b[l], mlstm_norm_w[l])
        nmw = norm_mix_w[l].reshape(1, d)
        nfw = norm_ffn_w[l].reshape(1, d)
        hp, sp, knew, vnew = _mixers_prompt(hp, batch, seq, nmw, lp, mlstm_gate_b[l], rope_p,
                                            l, depth, knew, vnew)
        hs, ss = _mixers_sample(hs, past_len, nmw, lp, mlstm_gate_b[l], rope_s, state_pool[l],
                                state_mlstm_c[l], state_mlstm_n[l], state_mlstm_m[l],
                                ck, cv, page_table, l * n_pool)
        st_p.append(sp)
        st_s.append(ss)
        last = l == depth - 1
        j = l // 2
        if l % 2 == 0:
            w1, w3, w2 = ffn_w1[j].astype(BF16), ffn_w3[j].astype(BF16), ffn_w2[j].astype(BF16)
            hp = _ffn(hp, nfw, w1, w3, w2)
            hs = _ffn(hs, nfw, w1, w3, w2)
            if last:
                hp, hs = _final_norm(hp, fw), _final_norm(hs, fw)
        else:
            w1, w3, w2 = moe_w1[j].astype(BF16), moe_w3[j].astype(BF16), moe_w2[j].astype(BF16)
            rw = jnp.zeros((d, LANE), F32).at[:, :n_experts].set(router_w[j])
            rb = jnp.full((1, LANE), -jnp.inf, F32).at[0, :n_experts].set(router_b[j])
            xp, gates, e_idx = _router(hp, nfw, rw, rb)
            pos, src, tile_expert, nused = _dispatch_plan(e_idx, n_experts, tm_moe)
            o_sorted = _gmm(xp, tile_expert, src, nused, w1, w3, w2, tm_moe)
            hp = _combine(pos, hp, gates, o_sorted, fw, last)
            _, gates_s, e_s = _router(hs, nfw, rw, rb)
            hs = _moe_small(hs, nfw, gates_s, e_s, w1, w3, w2, fw, last)

    stack = lambda sts, i: jnp.stack([s[i] for s in sts])
    kv_leaf = lambda a: jnp.transpose(a.reshape(depth, batch, N_HEADS, HEAD_DIM, seq), (0, 1, 4, 2, 3))
    return (hp.reshape(batch, seq, d), hs.reshape(bs, 1, d),
            kv_leaf(knew), kv_leaf(vnew), stack(st_s, 0), stack(st_s, 1),
            stack(st_p, 0), stack(st_s, 2), stack(st_s, 3),
            stack(st_p, 1), stack(st_p, 2), stack(st_p, 3),
            stack(st_s, 4), stack(st_s, 5), stack(st_s, 6))
```

```python
import functools

import numpy as np
import jax
import jax.numpy as jnp
from jax import lax
from jax.experimental import pallas as pl
from jax.experimental.pallas import tpu as pltpu

F32 = jnp.float32
BF16 = jnp.bfloat16
I32 = jnp.int32
HIGHEST = lax.Precision.HIGHEST

N_MIXERS = 4
HEAD_DIM = 64
N_HEADS = 4
GROUP_W = N_HEADS * HEAD_DIM
POOL_WINDOWS = (2, 4, 8, 16)
POOL_GROUP = GROUP_W // len(POOL_WINDOWS)
POOL_BUF = max(POOL_WINDOWS) - 1
HALO = POOL_BUF + 1
SGU_CHUNK = 128
MLSTM_CHUNK = 64
MOBA_BLOCK = 256
MOBA_TOPK = 3
PAGE_SIZE = 128
ROPE_THETA = 500000.0
ROT_DIM = HEAD_DIM // 4
TOP_K = 2
NORM_EPS = 1e-6
NEG = -1e30
NEG_PICKED = -3e38

LANE = 128
GATE_W = LANE
Z_W = 10 * GROUP_W + GATE_W
Z_TN = 7 * LANE
COL_P, COL_SU, COL_SV, COL_MQ, COL_MK, COL_MV, COL_MO, COL_AQ, COL_AK, COL_AV = range(10)
COL_GATE = (10 * GROUP_W) // GATE_W
VMEM_LIMIT = 56 * 1024 * 1024
ISSUE_UNROLL = 8


def _params(*sem):
    return pltpu.CompilerParams(dimension_semantics=sem, vmem_limit_bytes=VMEM_LIMIT)


def _row_tile(n, pref):
    return pref if n % pref == 0 else n


def _lane_head(shape, dim):
    return lax.broadcasted_iota(I32, shape, dim) // HEAD_DIM


def _by_head(vals, lh):
    out = vals[N_HEADS - 1]
    for h in range(N_HEADS - 2, -1, -1):
        out = jnp.where(lh == h, vals[h], out)
    return out


def _dot_nt(a, b, precision=None):
    return lax.dot_general(a, b, (((1,), (1,)), ((), ())), precision=precision,
                           preferred_element_type=F32)


def _logsig(x):
    return jnp.minimum(x, 0.0) - jnp.log1p(jnp.exp(-jnp.abs(x)))


def _split_bf16(x, passes):
    terms = []
    for _ in range(passes):
        t = x.astype(BF16)
        terms.append(t)
        x = x - t.astype(F32)
    return terms


def _dot_exact01(x, onehot_bf16, passes, x_is_lhs=True):
    acc = None
    for t in _split_bf16(x, passes):
        part = (jnp.dot(t, onehot_bf16, preferred_element_type=F32) if x_is_lhs
                else jnp.dot(onehot_bf16, t, preferred_element_type=F32))
        acc = part if acc is None else acc + part
    return acc


def _inproj_kernel(x_ref, nw_ref, w_ref, cos_ref, sa_ref, sb_ref, z_ref, xn_ref):
    j = pl.program_id(1)

    @pl.when(j == 0)
    def _():
        x = x_ref[...]
        ms = jnp.mean(x * x, axis=-1, keepdims=True)
        xn_ref[...] = (x * lax.rsqrt(ms + NORM_EPS) * nw_ref[...]).astype(BF16)

    z = jnp.dot(xn_ref[...], w_ref[...], preferred_element_type=F32)

    @pl.when(j < 2)
    def _():
        z_ref[...] = z

    @pl.when(j == 2)
    def _():
        c, sa, sb = cos_ref[...], sa_ref[...], sb_ref[...]
        half = ROT_DIM // 2
        for o in (0, GROUP_W):
            t = z[:, o:o + GROUP_W]
            z_ref[:, o:o + GROUP_W] = (t * c + pltpu.roll(t, GROUP_W - half, 1) * sa
                                       + pltpu.roll(t, half, 1) * sb)
        z_ref[:, 2 * GROUP_W:] = z[:, 2 * GROUP_W:]


def _inproj(x, nw, w, rope_tabs):
    n, d = x.shape
    tm = _row_tile(n, 512)
    tab_tiles = rope_tabs[0].shape[0] // tm
    tab_spec = pl.BlockSpec((tm, GROUP_W), lambda i, j: (i % tab_tiles, 0))
    return pl.pallas_call(
        _inproj_kernel,
        grid=(n // tm, Z_W // Z_TN),
        in_specs=[pl.BlockSpec((tm, d), lambda i, j: (i, 0)),
                  pl.BlockSpec((1, d), lambda i, j: (0, 0)),
                  pl.BlockSpec((d, Z_TN), lambda i, j: (0, j)),
                  tab_spec, tab_spec, tab_spec],
        out_specs=pl.BlockSpec((tm, Z_TN), lambda i, j: (i, j)),
        out_shape=jax.ShapeDtypeStruct((n, Z_W), F32),
        scratch_shapes=[pltpu.VMEM((tm, d), BF16)],
        compiler_params=_params("parallel", "arbitrary"),
        name="inproj",
    )(x, nw, w, *rope_tabs)


def _rope(t, c, sa, sb):
    half = ROT_DIM // 2
    return t * c + pltpu.roll(t, GROUP_W - half, 1) * sa + pltpu.roll(t, half, 1) * sb


def _inproj_prompt_kernel(x_ref, nw_ref, w_ref, cos_ref, sa_ref, sb_ref, kprev_hbm, vprev_hbm,
                          zm_ref, q_ref, g_ref, gt_ref, kb_ref, vt_ref, km_ref, knew_ref, vnew_ref,
                          *, tiles_per_seq):
    del kprev_hbm, vprev_hbm
    i = pl.program_id(0)
    tm = x_ref.shape[0]
    g = GROUP_W

    ti = i % tiles_per_seq

    @pl.when(ti == 0)
    def _():
        km_ref[...] = jnp.zeros(km_ref.shape, F32)

    x = x_ref[...]
    ms = jnp.mean(x * x, axis=-1, keepdims=True)
    xn = (x * lax.rsqrt(ms + NORM_EPS) * nw_ref[...]).astype(BF16)
    for jj in range(2):
        zm_ref[:, jj * Z_TN:(jj + 1) * Z_TN] = jnp.dot(xn, w_ref[:, jj * Z_TN:(jj + 1) * Z_TN],
                                                       preferred_element_type=F32)
    z = jnp.dot(xn, w_ref[:, 2 * Z_TN:], preferred_element_type=F32)

    c, sa, sb = cos_ref[...], sa_ref[...], sb_ref[...]
    q_ref[...] = _rope(z[:, 0:g], c, sa, sb)
    k = _rope(z[:, g:2 * g], c, sa, sb)
    v = z[:, 2 * g:3 * g]
    gates = z[:, 3 * g:]
    g_ref[...] = gates
    gt_ref[...] = gates.T[0:2 * N_HEADS, :]
    nblk = tm // MOBA_BLOCK
    kb_ref[...] = k.astype(BF16).reshape(nblk, MOBA_BLOCK, g)
    knew_ref[...] = k.T
    v_t = v.T
    vnew_ref[...] = v_t
    km = km_ref[...]
    row = lax.broadcasted_iota(I32, km.shape, 0)
    for b in range(nblk):
        cols = slice(b * MOBA_BLOCK, (b + 1) * MOBA_BLOCK)
        vt_ref[b] = v_t[:, cols].astype(BF16)
        mean = jnp.sum(k[cols, :], axis=0, keepdims=True) * (1.0 / MOBA_BLOCK)
        km = jnp.where(row == ti * nblk + b, mean, km)
    km_ref[...] = km


def _inproj_prompt(x, nw, w, rope_tabs, batch, seq, layer, depth, knew, vnew):
    n, d = x.shape
    tm = 2 * MOBA_BLOCK
    assert seq % tm == 0
    tps = seq // tm
    nb = seq // MOBA_BLOCK
    nblk = tm // MOBA_BLOCK
    g = GROUP_W
    tab_spec = pl.BlockSpec((tm, g), lambda i: (i % tps, 0))
    slab = pl.BlockSpec((None, None, g, tm), lambda i: (layer, i // tps, 0, i % tps))
    blocks = pl.BlockSpec((nblk, MOBA_BLOCK, g), lambda i: (i, 0, 0))
    rows = lambda c: pl.BlockSpec((tm, c), lambda i: (i, 0))
    sds = jax.ShapeDtypeStruct
    slab_shape = sds((depth, batch, g, seq), F32)
    any_spec = pl.BlockSpec(memory_space=pl.ANY)
    return pl.pallas_call(
        functools.partial(_inproj_prompt_kernel, tiles_per_seq=tps),
        grid=(n // tm,),
        in_specs=[rows(d),
                  pl.BlockSpec((1, d), lambda i: (0, 0)),
                  pl.BlockSpec((d, Z_W), lambda i: (0, 0)),
                  tab_spec, tab_spec, tab_spec, any_spec, any_spec],
        out_specs=[rows(2 * Z_TN), rows(g), rows(GATE_W),
                   pl.BlockSpec((2 * N_HEADS, tm), lambda i: (0, i)),
                   blocks, blocks,
                   pl.BlockSpec((None, nb, g), lambda i: (i // tps, 0, 0)),
                   slab, slab],
        out_shape=[sds((n, 2 * Z_TN), F32), sds((n, g), F32), sds((n, GATE_W), F32),
                   sds((2 * N_HEADS, n), F32),
                   sds((batch * nb, MOBA_BLOCK, g), BF16), sds((batch * nb, g, MOBA_BLOCK), BF16),
                   sds((batch, nb, g), F32), slab_shape, slab_shape],
        input_output_aliases={6: 7, 7: 8},
        compiler_params=_params("arbitrary"),
        name="inproj_prompt",
    )(x, nw, w, *rope_tabs, knew, vnew)


def _rope_tables(pos):
    half = ROT_DIM // 2
    inv = 1.0 / (ROPE_THETA ** (jnp.arange(half, dtype=F32) / half))
    ang = pos.astype(F32)[:, None] * inv[None, :]
    cos, sin = jnp.cos(ang), jnp.sin(ang)
    t = pos.shape[0]
    rest = HEAD_DIM - ROT_DIM
    one = jnp.ones((t, rest), F32)
    zero = jnp.zeros((t, rest), F32)
    zh = jnp.zeros((t, half), F32)
    c = jnp.concatenate([cos, cos, one], axis=1)
    sa = jnp.concatenate([-sin, zh, zero], axis=1)
    sb = jnp.concatenate([zh, sin, zero], axis=1)
    tile = lambda a: jnp.tile(a, (1, N_HEADS))
    return tile(c), tile(sa), tile(sb)


def _window_of_lane(shape, dim, lo):
    g = (lax.broadcasted_iota(I32, shape, dim) + lo) // POOL_GROUP
    w = jnp.full(shape, POOL_WINDOWS[-1], I32)
    for i in range(len(POOL_WINDOWS) - 2, -1, -1):
        w = jnp.where(g == i, POOL_WINDOWS[i], w)
    return w


def _poolsgu_kernel(p_ref, su_ref, sv_ref, pw_ref, ps_ref, sw_ref, sb_ref, o_ref, ebuf):
    t = pl.program_id(1)
    tt = p_ref.shape[0]

    @pl.when(t == 0)
    def _():
        ebuf[0:HALO, :] = jnp.zeros((HALO, GROUP_W), F32)

    @pl.when(t > 0)
    def _():
        ebuf[0:HALO, :] = ebuf[tt:tt + HALO, :]

    ebuf[HALO:HALO + tt, :] = p_ref[...]

    pos1 = lax.broadcasted_iota(I32, (tt, LANE), 0) + t * tt + 1
    halves = []
    for half in range(2):
        lo = half * LANE
        wsmall, wbig = POOL_WINDOWS[2 * half], POOL_WINDOWS[2 * half + 1]
        e0 = ebuf[HALO:HALO + tt, lo:lo + LANE]
        acc = e0
        for off in range(1, wsmall):
            acc = acc + ebuf[HALO - off:HALO - off + tt, lo:lo + LANE]
        small = acc
        for off in range(wsmall, wbig):
            acc = acc + ebuf[HALO - off:HALO - off + tt, lo:lo + LANE]
        first = lax.broadcasted_iota(I32, (tt, LANE), 1) < POOL_GROUP
        wsum = jnp.where(first, small, acc)
        cnt = jnp.minimum(pos1, _window_of_lane((tt, LANE), 1, lo)).astype(F32)
        halves.append(wsum / cnt - e0)
    d = jnp.concatenate(halves, axis=1).astype(BF16)
    y_pool = jnp.dot(d, pw_ref[...], preferred_element_type=F32) * ps_ref[...]
    o_ref[:, 0:GROUP_W] = y_pool.astype(o_ref.dtype)

    row = lax.broadcasted_iota(I32, (SGU_CHUNK, SGU_CHUNK), 0)
    col = lax.broadcasted_iota(I32, (SGU_CHUNK, SGU_CHUNK), 1)
    ws = [jnp.where(row >= col, sw_ref[h], 0.0).astype(BF16) for h in range(N_HEADS)]
    lh = _lane_head((SGU_CHUNK, GROUP_W), 1)
    for c in range(tt // SGU_CHUNK):
        rows = slice(c * SGU_CHUNK, (c + 1) * SGU_CHUNK)
        v = sv_ref[rows, :].astype(BF16)
        mixed = _by_head([jnp.dot(ws[h], v, preferred_element_type=F32) for h in range(N_HEADS)], lh)
        y = su_ref[rows, :] * (mixed + sb_ref[...])
        o_ref[rows, GROUP_W:2 * GROUP_W] = y.astype(o_ref.dtype)


def _poolsgu(z, batch, seq, pool_wbd, pool_scale, sgu_w, sgu_bias):
    tt = _row_tile(seq, 512)
    tps = seq // tt
    zspec = lambda c: pl.BlockSpec((tt, GROUP_W), lambda b, t, c=c: (b * tps + t, c))
    return pl.pallas_call(
        _poolsgu_kernel,
        grid=(batch, tps),
        in_specs=[zspec(COL_P), zspec(COL_SU), zspec(COL_SV),
                  pl.BlockSpec((GROUP_W, GROUP_W), lambda b, t: (0, 0)),
                  pl.BlockSpec((1, GROUP_W), lambda b, t: (0, 0)),
                  pl.BlockSpec((N_HEADS, SGU_CHUNK, SGU_CHUNK), lambda b, t: (0, 0, 0)),
                  pl.BlockSpec((SGU_CHUNK, GROUP_W), lambda b, t: (0, 0))],
        out_specs=pl.BlockSpec((tt, 2 * GROUP_W), lambda b, t: (b * tps + t, 0)),
        out_shape=jax.ShapeDtypeStruct((batch * seq, 2 * GROUP_W), BF16),
        scratch_shapes=[pltpu.VMEM((HALO + tt, GROUP_W), F32)],
        compiler_params=_params("parallel", "arbitrary"),
        name="pool_sgu",
    )(z, z, z, pool_wbd, pool_scale, sgu_w, sgu_bias)


def _mlstm_kernel(q_ref, k_ref, v_ref, o_ref, g_ref, *rest):
    L = MLSTM_CHUNK
    batch = q_ref.shape[0]
    gt_refs = rest[:batch]
    gbc_ref, gbr_ref, nw_ref, y_ref, ct_ref, n_ref, m_ref = rest[batch:]

    @pl.when(pl.program_id(0) == 0)
    def _():
        ct_ref[...] = jnp.zeros(ct_ref.shape, F32)
        n_ref[...] = jnp.zeros(n_ref.shape, F32)
        m_ref[...] = jnp.zeros(m_ref.shape, F32)

    r_i = lax.broadcasted_iota(I32, (L, L), 0)
    c_i = lax.broadcasted_iota(I32, (L, L), 1)
    causal = r_i >= c_i
    tri_l = jnp.where(causal, 1.0, 0.0).astype(BF16)
    tri_u = jnp.where(r_i <= c_i, 1.0, 0.0).astype(BF16)
    lh = _lane_head((L, GROUP_W), 1)
    bd = _lane_head((GROUP_W, GROUP_W), 0) == _lane_head((GROUP_W, GROUP_W), 1)
    same_head = jnp.where(bd, 1.0, 0.0).astype(BF16)

    ct_b = [ct_ref[b] for b in range(batch)]
    n_b = [n_ref[b] for b in range(batch)]
    m_b = [[m_ref[b, h:h + 1, 0:1] for h in range(N_HEADS)] for b in range(batch)]

    def chunk(c, b):
        ct, nrow, m_heads = ct_b[b], n_b[b], m_b[b]
        rows = slice(c * L, (c + 1) * L)
        q = q_ref[b, rows, :]
        ks = k_ref[b, rows, :] * (HEAD_DIM ** -0.5)
        v = v_ref[b, rows, :]
        g = g_ref[b, rows, :] + gbc_ref[...]
        gt = gt_refs[b][:, rows] + gbr_ref[...]
        bcol = _dot_exact01(_logsig(g), tri_l, 3, x_is_lhs=False)
        brow = _dot_exact01(_logsig(gt), tri_u, 3)
        qb, kb, vb = q.astype(BF16), ks.astype(BF16), v.astype(BF16)

        s_l, inter_l, deni_l, emt_l, wsrc_l, dec_l, mnew_l = [], [], [], [], [], [], []
        for h in range(N_HEADS):
            bc = bcol[:, N_HEADS + h:N_HEADS + h + 1]
            ic = g[:, h:h + 1]
            br = brow[N_HEADS + h:N_HEADS + h + 1, :]
            ir = gt[h:h + 1, :]
            dm = jnp.where(causal, bc - br + ir, NEG)
            a = bc + m_heads[h]
            mt = jnp.maximum(a, jnp.max(dm, axis=1, keepdims=True))
            qm = jnp.where(lh == h, q, 0.0).astype(BF16)
            s = _dot_nt(qm, kb) * jnp.exp(dm - mt)
            m_new = mt[L - 1:L, :]
            s_l.append(s)
            inter_l.append(jnp.exp(a - mt))
            deni_l.append(jnp.sum(s, axis=1, keepdims=True))
            emt_l.append(jnp.exp(-mt))
            wsrc_l.append(jnp.exp(bc[L - 1:L, :] - bc + ic - m_new))
            dec_l.append(jnp.exp(a[L - 1:L, :] - m_new))
            mnew_l.append(m_new)

        r = jnp.dot(jnp.concatenate(s_l, axis=0).astype(BF16), vb, preferred_element_type=F32)
        num_i = _by_head([r[h * L:(h + 1) * L, :] for h in range(N_HEADS)], lh)
        q_c = jnp.dot(qb, ct.astype(BF16), preferred_element_type=F32)
        q_n = _dot_exact01(q * nrow, same_head, 2)
        inter = _by_head(inter_l, lh)
        num = num_i + inter * q_c
        den = _by_head(deni_l, lh) + inter * q_n
        hval = num / jnp.maximum(jnp.abs(den), _by_head(emt_l, lh))
        ms = _dot_exact01(hval * hval, same_head, 2) * (1.0 / HEAD_DIM)
        y = hval * lax.rsqrt(ms + NORM_EPS) * nw_ref[...]
        y_ref[b, rows, :] = (jax.nn.sigmoid(o_ref[b, rows, :]) * y).astype(y_ref.dtype)

        kw = ks * _by_head(wsrc_l, lh)
        dec = _by_head(dec_l, lh[0:1, :])
        upd = jnp.dot(kw.T.astype(BF16), vb, preferred_element_type=F32)
        ct_b[b] = ct * dec + jnp.where(bd, upd, 0.0)
        n_b[b] = nrow * dec + jnp.sum(kw, axis=0, keepdims=True)
        m_b[b] = mnew_l

    for c in range(q_ref.shape[1] // L):
        for b in range(batch):
            chunk(c, b)

    for b in range(batch):
        ct_ref[b] = ct_b[b]
        n_ref[b] = n_b[b]
        for h in range(N_HEADS):
            m_ref[b, h:h + 1, :] = jnp.broadcast_to(m_b[b][h], (1, m_ref.shape[2]))


def _mlstm(z, gates, gates_t, batch, seq, gate_b, norm_w):
    rows = _row_tile(seq, 2 * MLSTM_CHUNK)
    tps = seq // rows
    z3 = z.reshape(batch, seq, z.shape[1])
    zspec = lambda c: pl.BlockSpec((batch, rows, GROUP_W), lambda t, c=c: (0, t, c))
    gbc = jnp.zeros((1, GATE_W), F32).at[0, :2 * N_HEADS].set(gate_b.reshape(-1))
    gbr = gate_b.reshape(2 * N_HEADS, 1)
    state = lambda r, c: (pl.BlockSpec((batch, r, c), lambda t: (0, 0, 0)),
                          jax.ShapeDtypeStruct((batch, r, c), F32))
    (cs, csh), (ns, nsh), (msp, msh) = state(GROUP_W, GROUP_W), state(1, GROUP_W), state(8, LANE)
    gt_specs = [pl.BlockSpec((2 * N_HEADS, rows), lambda t, b=b: (0, b * tps + t)) for b in range(batch)]
    y, ct, nst, mst = pl.pallas_call(
        _mlstm_kernel,
        grid=(tps,),
        in_specs=[zspec(COL_MQ), zspec(COL_MK), zspec(COL_MV), zspec(COL_MO),
                  pl.BlockSpec((batch, rows, GATE_W), lambda t: (0, t, 0))] + gt_specs + [
                  pl.BlockSpec((1, GATE_W), lambda t: (0, 0)),
                  pl.BlockSpec((2 * N_HEADS, 1), lambda t: (0, 0)),
                  pl.BlockSpec((1, GROUP_W), lambda t: (0, 0))],
        out_specs=[pl.BlockSpec((batch, rows, GROUP_W), lambda t: (0, t, 0)), cs, ns, msp],
        out_shape=[jax.ShapeDtypeStruct((batch, seq, GROUP_W), BF16), csh, nsh, msh],
        compiler_params=_params("arbitrary"),
        name="mlstm",
    )(z3, z3, z3, z3, gates.reshape(batch, seq, GATE_W), *([gates_t] * batch), gbc, gbr, norm_w)
    return y.reshape(batch * seq, GROUP_W), ct, nst, mst


def _pick_top_blocks(s, n_valid_mask):
    blk = lax.broadcasted_iota(I32, s.shape, 0)
    s = jnp.where(n_valid_mask, s, NEG)
    bias = jnp.full(s.shape, NEG, F32)
    for _ in range(MOBA_TOPK):
        mx = jnp.max(s, axis=0, keepdims=True)
        idx = jnp.min(jnp.where(s == mx, blk, s.shape[0]), axis=0, keepdims=True)
        pick = blk == idx
        bias = jnp.where(jnp.logical_and(pick, mx > 0.5 * NEG), 0.0, bias)
        s = jnp.where(pick, NEG_PICKED, s)
    return bias


MOBA_KEY_CHUNK = 64
LOG2E = 1.4426950408889634


def _moba_logits(k_blk, qbd_sc, lt_sc, mx_sc, buf, key_le_query=None):
    B = MOBA_BLOCK
    lt = jnp.dot(k_blk[...], qbd_sc[...], preferred_element_type=F32)
    if key_le_query is not None:
        lt = jnp.where(jnp.concatenate([key_le_query] * N_HEADS, axis=1), lt, NEG)
    lt_sc[buf] = lt
    mx_sc[buf] = jnp.max(lt.reshape(B // 8, 8, N_HEADS * B), axis=0)


def _moba_softmax_pv(h, buf, vt_blk, lt_sc, mx_sc, p_sc, m_sc, l_sc, acc_sc, bias):
    B = MOBA_BLOCK
    ch = MOBA_KEY_CHUNK
    groups = ch // 8
    first = bias is None
    cols = slice(h * B, (h + 1) * B)
    cand = jnp.max(mx_sc[buf, :, cols], axis=0, keepdims=True)
    if first:
        m_new = shift = cand
    else:
        m_old = m_sc[h:h + 1, :]
        m_new = jnp.maximum(m_old, cand + bias)
        alpha = jnp.exp2(m_old - m_new)
        shift = m_new - bias
    lsum = None
    for c in range(B // ch):
        rows = slice(c * ch, (c + 1) * ch)
        p = jnp.exp2(lt_sc[buf, rows, cols] - shift)
        part = jnp.sum(p.reshape(groups, 8, B), axis=0)
        lsum = part if lsum is None else lsum + part
        p_sc[h, rows, :] = p.astype(BF16)
    lnew = jnp.sum(lsum, axis=0, keepdims=True)
    hs = slice(h * HEAD_DIM, (h + 1) * HEAD_DIM)
    pv = jnp.dot(vt_blk[hs, :], p_sc[h], preferred_element_type=F32)
    if first:
        l_sc[h:h + 1, :] = lnew
        acc_sc[hs, :] = pv
    else:
        l_sc[h:h + 1, :] = alpha * l_sc[h:h + 1, :] + lnew
        acc_sc[hs, :] = alpha * acc_sc[hs, :] + pv
    m_sc[h:h + 1, :] = m_new


def _moba_kernel(q_ref, kb_ref, vt_ref, km_ref, o_ref, qbd_sc, sel_sc, m_sc, l_sc, acc_sc, lt_sc, mx_sc, p_sc):
    i = pl.program_id(1)
    nb = km_ref.shape[0]
    nbp = sel_sc.shape[0] // N_HEADS
    B = MOBA_BLOCK
    q = q_ref[...]
    km = km_ref[...]
    past =lax.broadcasted_iota(I32, (nb, B), 0) < i
    key_le_query = lax.broadcasted_iota(I32, (B, B), 0) <= lax.broadcasted_iota(I32, (B, B), 1)
    stats = (lt_sc, mx_sc, p_sc, m_sc, l_sc, acc_sc)

    q_t = q.T
    q_ts = q_t * (HEAD_DIM ** -0.5 * LOG2E)
    row_head = _lane_head((GROUP_W, B), 0)
    lh_nb = _lane_head((nb, GROUP_W), 1)
    km_heads = jnp.concatenate([jnp.where(lh_nb == h, km, 0.0) for h in range(N_HEADS)], axis=0)
    s_all = jnp.dot(km_heads, q_t, precision=HIGHEST, preferred_element_type=F32)
    for h in range(N_HEADS):
        sel_sc[h * nbp:h * nbp + nb, :] = _pick_top_blocks(s_all[h * nb:(h + 1) * nb, :], past)
        qbd_sc[:, h * B:(h + 1) * B] = jnp.where(row_head == h, q_ts, 0.0).astype(BF16)

    _moba_logits(kb_ref.at[i], qbd_sc, lt_sc, mx_sc, 1, key_le_query)
    _moba_logits(kb_ref.at[0], qbd_sc, lt_sc, mx_sc, 0)
    for h in range(N_HEADS):
        _moba_softmax_pv(h, 1, vt_ref.at[i], *stats, None)

    def step(n, buf):
        _moba_logits(kb_ref.at[jnp.minimum(n + 1, nb - 1)], qbd_sc, lt_sc, mx_sc, 1 - buf)
        for h in range(N_HEADS):
            bias = sel_sc[pl.ds(h * nbp + n, 1), :]
            _moba_softmax_pv(h, buf, vt_ref.at[n], *stats, bias)

    def body(k, carry):
        step(2 * k, 0)
        step(2 * k + 1, 1)
        return carry

    lax.fori_loop(0, (i + 1) // 2, body, 0)

    for h in range(N_HEADS):
        hs = slice(h * HEAD_DIM, (h + 1) * HEAD_DIM)
        acc_sc[hs, :] = acc_sc[hs, :] / l_sc[h:h + 1, :]
    o_ref[...] = acc_sc[...].T.astype(o_ref.dtype)


def _moba(q, kb4, vt4, kmean, batch, nb):
    nbp = -(-nb // 8) * 8
    B = MOBA_BLOCK
    blk4 = pl.BlockSpec((None, nb, B, GROUP_W), lambda b, i: (b, 0, 0, 0))
    return pl.pallas_call(
        _moba_kernel,
        grid=(batch, nb),
        in_specs=[pl.BlockSpec((B, GROUP_W), lambda b, i: (b * nb + i, 0)),
                  blk4, blk4,
                  pl.BlockSpec((None, nb, GROUP_W), lambda b, i: (b, 0, 0))],
        out_specs=pl.BlockSpec((B, GROUP_W), lambda b, i: (b * nb + i, 0)),
        out_shape=jax.ShapeDtypeStruct((batch * nb * B, GROUP_W), BF16),
        scratch_shapes=[pltpu.VMEM((GROUP_W, N_HEADS * B), BF16),
                        pltpu.VMEM((N_HEADS * nbp, B), F32),
                        pltpu.VMEM((8, B), F32), pltpu.VMEM((8, B), F32),
                        pltpu.VMEM((GROUP_W, B), F32),
                        pltpu.VMEM((2, B, N_HEADS * B), F32),
                        pltpu.VMEM((2, 8, N_HEADS * B), F32),
                        pltpu.VMEM((N_HEADS, B, B), BF16)],
        compiler_params=_params("parallel", "arbitrary"),
        name="moba",
    )(q, kb4, vt4, kmean)


def _outproj_kernel(x_ref, ma_ref, mc_ref, md_ref, w_ref, o_ref):
    g2 = 2 * GROUP_W
    acc = jnp.dot(ma_ref[...], w_ref[0:g2, :], preferred_element_type=F32)
    acc += jnp.dot(mc_ref[...], w_ref[g2:g2 + GROUP_W, :], preferred_element_type=F32)
    acc += jnp.dot(md_ref[...], w_ref[g2 + GROUP_W:, :], preferred_element_type=F32)
    o_ref[...] = x_ref[...] + acc


def _outproj(x, ma, mc, md, w):
    n, d = x.shape
    tm = _row_tile(n, 512)
    rows = lambda c: pl.BlockSpec((tm, c), lambda i: (i, 0))
    return pl.pallas_call(
        _outproj_kernel,
        grid=(n // tm,),
        in_specs=[rows(d), rows(2 * GROUP_W), rows(GROUP_W), rows(GROUP_W),
                  pl.BlockSpec(w.shape, lambda i: (0, 0))],
        out_specs=rows(d),
        out_shape=jax.ShapeDtypeStruct((n, d), F32),
        compiler_params=_params("parallel"),
        name="outproj",
    )(x, ma, mc, md, w)


def _ff_tile(d_ff):
    return d_ff // 2 if (d_ff // 2) % LANE == 0 else d_ff


def _swiglu_partial(xn, w1_ref, w3_ref, w2_ref):
    a = jnp.dot(xn, w1_ref[...], preferred_element_type=F32)
    b = jnp.dot(xn, w3_ref[...], preferred_element_type=F32)
    hmid = (a * jax.nn.sigmoid(a) * b).astype(BF16)
    return jnp.dot(hmid, w2_ref[...], preferred_element_type=F32)


def _swiglu_full(xn, w1_ref, w3_ref, w2_ref):
    d_ff = w1_ref.shape[1]
    tf = _ff_tile(d_ff)
    acc = None
    for jj in range(d_ff // tf):
        sl = slice(jj * tf, (jj + 1) * tf)
        part = _swiglu_partial(xn, w1_ref.at[:, sl], w3_ref.at[:, sl], w2_ref.at[sl, :])
        acc = part if acc is None else acc + part
    return acc


def _ffn_kernel(x_ref, nw_ref, w1_ref, w3_ref, w2_ref, o_ref):
    x = x_ref[...]
    ms = jnp.mean(x * x, axis=-1, keepdims=True)
    xn = (x * lax.rsqrt(ms + NORM_EPS) * nw_ref[...]).astype(BF16)
    o_ref[...] = x + _swiglu_full(xn, w1_ref, w3_ref, w2_ref)


def _ffn(x, nw, w1, w3, w2):
    n, d = x.shape
    tm = _row_tile(n, 512)
    resident = lambda a: pl.BlockSpec(a.shape, lambda i: (0, 0), pipeline_mode=pl.Buffered(1))
    return pl.pallas_call(
        _ffn_kernel,
        grid=(n // tm,),
        in_specs=[pl.BlockSpec((tm, d), lambda i: (i, 0)),
                  pl.BlockSpec((1, d), lambda i: (0, 0)),
                  resident(w1), resident(w3), resident(w2)],
        out_specs=pl.BlockSpec((tm, d), lambda i: (i, 0)),
        out_shape=jax.ShapeDtypeStruct((n, d), F32),
        compiler_params=_params("parallel"),
        name="ffn_dense",
    )(x, nw, w1, w3, w2)


def _router_kernel(x_ref, nw_ref, rw_ref, rb_ref, xp_ref, g_ref, e_ref):
    x = x_ref[...]
    ms = jnp.mean(x * x, axis=-1, keepdims=True)
    xn = x * lax.rsqrt(ms + NORM_EPS) * nw_ref[...]
    tm, d = x.shape
    half = d // 2
    x_hi = xn.astype(BF16)
    bits = pltpu.bitcast(x_hi.astype(F32), jnp.uint32)
    packed = jnp.bitwise_or(jnp.right_shift(bits[:, :half], jnp.uint32(16)), bits[:, half:])
    nc = half // LANE
    for c in range(nc):
        xp_ref[pl.ds(c, tm, stride=nc), :] = packed[:, c * LANE:(c + 1) * LANE]

    rw = rw_ref[...]
    w_hi = rw.astype(BF16)
    w_lo = (rw - w_hi.astype(F32)).astype(BF16)
    x_lo = (xn - x_hi.astype(F32)).astype(BF16)
    dot = lambda a, b: jnp.dot(a, b, preferred_element_type=F32)
    logits = dot(x_hi, w_hi) + (dot(x_hi, w_lo) + dot(x_lo, w_hi)) + rb_ref[...]
    lane = lax.broadcasted_iota(I32, logits.shape, 1)
    mx1 = jnp.max(logits, axis=1, keepdims=True)
    i1 = jnp.min(jnp.where(logits == mx1, lane, LANE), axis=1, keepdims=True)
    rest = jnp.where(lane == i1, -jnp.inf, logits)
    mx2 = jnp.max(rest, axis=1, keepdims=True)
    i2 = jnp.min(jnp.where(rest == mx2, lane, LANE), axis=1, keepdims=True)
    e2 = jnp.exp(mx2 - mx1)
    g1 = 1.0 / (1.0 + e2)
    g_ref[...] = jnp.where(lane == 0, g1, jnp.where(lane == 1, e2 * g1, 0.0))
    e_ref[...] = jnp.where(lane == 0, i1, jnp.where(lane == 1, i2, 0))


def _router(x, nw, rw, rb):
    n, d = x.shape
    tm = _row_tile(n, 512)
    rows = lambda c: pl.BlockSpec((tm, c), lambda i: (i, 0))
    full = lambda a: pl.BlockSpec(a.shape, lambda i: (0, 0))
    return pl.pallas_call(
        _router_kernel,
        grid=(n // tm,),
        in_specs=[rows(d), full(nw), full(rw), full(rb)],
        out_specs=[pl.BlockSpec((tm * (d // 2 // LANE), LANE), lambda i: (i, 0)), rows(LANE), rows(LANE)],
        out_shape=[jax.ShapeDtypeStruct((n * (d // 2 // LANE), LANE), jnp.uint32),
                   jax.ShapeDtypeStruct((n, LANE), F32),
                   jax.ShapeDtypeStruct((n, LANE), I32)],
        compiler_params=_params("parallel"),
        name="router",
    )(x, nw, rw, rb)


def _unpack_rows(words):
    lo = pltpu.bitcast(jnp.left_shift(words, jnp.uint32(16)), F32)
    hi = pltpu.bitcast(jnp.bitwise_and(words, jnp.uint32(0xFFFF0000)), F32)
    return jnp.concatenate([lo, hi], axis=1).astype(BF16)


def _gmm_kernel(te_ref, src_ref, nused_ref, xp_hbm, w1_ref, w3_ref, w2_ref, o_ref, xbuf, sem):
    i = pl.program_id(0)
    ntiles = pl.num_programs(0)
    d = w1_ref.shape[0]
    oc = d // LANE
    tm = o_ref.shape[0] // oc
    nc = xbuf.shape[1] // tm
    slot = i % 2

    def start_row(tile, sl, r, prio):
        tok = src_ref[tile * tm + r]
        pltpu.make_async_copy(xp_hbm.at[pl.ds(pl.multiple_of(tok * nc, nc), nc), :],
                              xbuf.at[sl, pl.ds(pl.multiple_of(r * nc, nc), nc), :],
                              sem.at[sl]).start(priority=prio)

    def issue(tile, sl):
        def body(g, carry):
            for u in range(ISSUE_UNROLL):
                start_row(tile, sl, g * ISSUE_UNROLL + u, u % 2)
            return carry
        lax.fori_loop(0, tm // ISSUE_UNROLL, body, 0)

    def wait_tile(sl):
        pltpu.make_async_copy(xbuf.at[sl], xbuf.at[sl], sem.at[sl]).wait()

    nxt = jnp.minimum(i + 1, ntiles - 1)

    @pl.when(i == 0)
    def _():
        issue(0, 0)

    wait_tile(slot)

    @pl.when(i < nused_ref[0])
    def _():
        words = jnp.concatenate([xbuf[slot, pl.ds(c, tm, stride=nc), :] for c in range(nc)], axis=1)
        xn = _unpack_rows(words)
        for r in range(tm):
            start_row(nxt, 1 - slot, r, r % 2)
        acc = _swiglu_full(xn, w1_ref, w3_ref, w2_ref)
        for c in range(oc):
            o_ref[pl.ds(c, tm, stride=oc), :] = acc[:, c * LANE:(c + 1) * LANE]

    @pl.when(i >= nused_ref[0])
    def _():
        issue(nxt, 1 - slot)
        o_ref[...] = jnp.zeros(o_ref.shape, F32)

    @pl.when(i == ntiles - 1)
    def _():
        wait_tile(1 - slot)


def _gmm(xp, tile_expert, src, nused, w1, w3, w2, tm):
    ncap = src.shape[0]
    d = w1.shape[1]
    half = d // 2
    d_ff = w1.shape[2]
    oc = d // LANE
    expert = lambda r, c: pl.BlockSpec((None, r, c), lambda i, te, s, nu: (te[i], 0, 0))
    grid_spec = pltpu.PrefetchScalarGridSpec(
        num_scalar_prefetch=3,
        grid=(ncap // tm,),
        in_specs=[pl.BlockSpec(memory_space=pl.ANY), expert(d, d_ff), expert(d, d_ff), expert(d_ff, d)],
        out_specs=pl.BlockSpec((tm * oc, LANE), lambda i, te, s, nu: (i, 0)),
        scratch_shapes=[pltpu.VMEM((2, tm * (half // LANE), LANE), jnp.uint32),
                        pltpu.SemaphoreType.DMA((2,))])
    return pl.pallas_call(
        _gmm_kernel,
        grid_spec=grid_spec,
        out_shape=jax.ShapeDtypeStruct((ncap * oc, LANE), F32),
        compiler_params=_params("arbitrary"),
        name="moe_gmm",
    )(tile_expert, src, nused, xp, w1, w3, w2)


def _dispatch_plan(e_idx, n_experts, tm):
    n = e_idx.shape[0]
    ef = e_idx[:, :TOP_K].T.reshape(-1)
    onehot = (ef[:, None] == jnp.arange(n_experts, dtype=I32)[None, :]).astype(I32)
    rank = jnp.sum((jnp.cumsum(onehot, axis=0) - 1) * onehot, axis=1)
    counts = jnp.sum(onehot, axis=0)
    padded = ((counts + tm - 1) // tm) * tm
    ends = jnp.cumsum(padded)
    pos = (ends - padded)[ef] + rank
    ncap = TOP_K * n + n_experts * tm
    src = jnp.zeros((ncap,), I32).at[pos].set(jnp.arange(TOP_K * n, dtype=I32) % n,
                                              unique_indices=True, mode="promise_in_bounds")
    tile_start = jnp.arange(ncap // tm, dtype=I32) * tm
    tile_expert = jnp.minimum(jnp.searchsorted(ends, tile_start, side="right"), n_experts - 1).astype(I32)
    nused = (ends[-1] // tm).astype(I32).reshape(1)
    return pos.astype(I32), src, tile_expert, nused


def _combine_kernel(pos_ref, h_ref, g_ref, o_hbm, fw_ref, y_ref, obuf, sem, *, final_norm):
    i = pl.program_id(0)
    ntiles = pl.num_programs(0)
    tc, d = h_ref.shape
    oc = d // LANE
    n = ntiles * tc
    slot = i % 2

    def issue(tile, sl):
        def body(g, carry):
            for u in range(ISSUE_UNROLL):
                r = g * ISSUE_UNROLL + u
                for k in range(TOP_K):
                    p = pos_ref[k * n + tile * tc + r]
                    pltpu.make_async_copy(o_hbm.at[pl.ds(pl.multiple_of(p * oc, oc), oc), :],
                                          obuf.at[sl, k, pl.ds(pl.multiple_of(r * oc, oc), oc), :],
                                          sem.at[sl]).start(priority=k)
            return carry
        lax.fori_loop(0, tc // ISSUE_UNROLL, body, 0)

    @pl.when(i == 0)
    def _():
        issue(0, 0)

    @pl.when(i + 1 < ntiles)
    def _():
        issue(i + 1, 1 - slot)

    pltpu.make_async_copy(obuf.at[slot], obuf.at[slot], sem.at[slot]).wait()
    g = g_ref[...]
    rows = lambda k: jnp.concatenate([obuf[slot, k, pl.ds(c, tc, stride=oc), :] for c in range(oc)], axis=1)
    y = h_ref[...] + g[:, 0:1] * rows(0) + g[:, 1:2] * rows(1)
    if final_norm:
        ms = jnp.mean(y * y, axis=-1, keepdims=True)
        y = y * lax.rsqrt(ms + NORM_EPS) * fw_ref[...]
    y_ref[...] = y


def _combine(pos, h, gates, o_sorted, fw, final_norm):
    n, d = h.shape
    tc = _row_tile(n, 256)
    grid_spec = pltpu.PrefetchScalarGridSpec(
        num_scalar_prefetch=1,
        grid=(n // tc,),
        in_specs=[pl.BlockSpec((tc, d), lambda i, p: (i, 0)),
                  pl.BlockSpec((tc, LANE), lambda i, p: (i, 0)),
                  pl.BlockSpec(memory_space=pl.ANY),
                  pl.BlockSpec((1, d), lambda i, p: (0, 0))],
        out_specs=pl.BlockSpec((tc, d), lambda i, p: (i, 0)),
        scratch_shapes=[pltpu.VMEM((2, TOP_K, tc * (d // LANE), LANE), F32), pltpu.SemaphoreType.DMA((2,))])
    return pl.pallas_call(
        functools.partial(_combine_kernel, final_norm=final_norm),
        grid_spec=grid_spec,
        out_shape=jax.ShapeDtypeStruct((n, d), F32),
        compiler_params=_params("arbitrary"),
        name="moe_combine",
    )(pos, h, gates, o_sorted, fw)


def _moe_small_kernel(x_ref, nw_ref, g_ref, e_ref, w1_ref, w3_ref, w2_ref, fw_ref, y_ref,
                      xn_ref, acc_ref, *, final_norm):
    e = pl.program_id(0)
    j = pl.program_id(1)

    @pl.when(jnp.logical_and(e == 0, j == 0))
    def _():
        x = x_ref[...]
        ms = jnp.mean(x * x, axis=-1, keepdims=True)
        xn_ref[...] = (x * lax.rsqrt(ms + NORM_EPS) * nw_ref[...]).astype(BF16)
        acc_ref[...] = jnp.zeros(acc_ref.shape, F32)

    g = g_ref[...]
    idx = e_ref[...]
    gate = (jnp.where(idx[:, 0:1] == e, g[:, 0:1], 0.0) + jnp.where(idx[:, 1:2] == e, g[:, 1:2], 0.0))
    acc_ref[...] += gate * _swiglu_partial(xn_ref[...], w1_ref, w3_ref, w2_ref)

    @pl.when(jnp.logical_and(e == pl.num_programs(0) - 1, j == pl.num_programs(1) - 1))
    def _():
        y = x_ref[...] + acc_ref[...]
        if final_norm:
            ms = jnp.mean(y * y, axis=-1, keepdims=True)
            y = y * lax.rsqrt(ms + NORM_EPS) * fw_ref[...]
        y_ref[...] = y


def _moe_small(x, nw, gates, e_idx, w1, w3, w2, fw, final_norm):
    n, d = x.shape
    n_experts, _, d_ff = w1.shape
    tf = _ff_tile(d_ff)
    full = lambda a: pl.BlockSpec(a.shape, lambda e, j: (0, 0))
    return pl.pallas_call(
        functools.partial(_moe_small_kernel, final_norm=final_norm),
        grid=(n_experts, d_ff // tf),
        in_specs=[full(x), full(nw), full(gates), full(e_idx),
                  pl.BlockSpec((None, d, tf), lambda e, j: (e, 0, j)),
                  pl.BlockSpec((None, d, tf), lambda e, j: (e, 0, j)),
                  pl.BlockSpec((None, tf, d), lambda e, j: (e, j, 0)),
                  full(fw)],
        out_specs=full(x),
        out_shape=jax.ShapeDtypeStruct((n, d), F32),
        scratch_shapes=[pltpu.VMEM((n, d), BF16), pltpu.VMEM((n, d), F32)],
        compiler_params=_params("arbitrary", "arbitrary"),
        name="moe_small",
    )(x, nw, gates, e_idx, w1, w3, w2, fw)


def _norm_kernel(x_ref, w_ref, o_ref):
    x = x_ref[...]
    ms = jnp.mean(x * x, axis=-1, keepdims=True)
    o_ref[...] = x * lax.rsqrt(ms + NORM_EPS) * w_ref[...]


def _final_norm(x, w):
    n, d = x.shape
    tm = _row_tile(n, 512)
    return pl.pallas_call(
        _norm_kernel,
        grid=(n // tm,),
        in_specs=[pl.BlockSpec((tm, d), lambda i: (i, 0)), pl.BlockSpec((1, d), lambda i: (0, 0))],
        out_specs=pl.BlockSpec((tm, d), lambda i: (i, 0)),
        out_shape=jax.ShapeDtypeStruct((n, d), F32),
        compiler_params=_params("parallel"),
        name="final_norm",
    )(x, w)


def _smix_kernel(p_ref, pf_ref, pw_ref, ps_ref, su_ref, sv_ref, sw0_ref, sb0_ref,
                 q_ref, k_ref, v_ref, ig_ref, fg_ref, c_ref, n_ref, m_ref, mo_ref, nw_ref,
                 ypool_ref, ysgu_ref, yml_ref, cn_ref, nn_ref, mn_ref, *, cnt):
    p = p_ref[...]
    lane_w = _window_of_lane(p.shape, 1, 0)
    acc = p
    wsum = jnp.zeros(p.shape, F32)
    for off in range(1, POOL_WINDOWS[-1] + 1):
        if off in POOL_WINDOWS:
            wsum = jnp.where(lane_w == off, acc, wsum)
        if off <= POOL_BUF:
            acc = acc + pf_ref[HALO - off]
    count = jnp.minimum(lane_w, cnt).astype(F32)
    d = (wsum / count - p).astype(BF16)
    ypool_ref[...] = jnp.dot(d, pw_ref[...], preferred_element_type=F32) * ps_ref[...]

    ysgu_ref[...] = su_ref[...] * (sw0_ref[...] * sv_ref[...] + sb0_ref[...])

    q = q_ref[...]
    ks = k_ref[...] * (HEAD_DIM ** -0.5)
    v = v_ref[...]
    ig = ig_ref[...]
    c = c_ref[...]
    nrow = n_ref[...]
    a = _logsig(fg_ref[...]) + m_ref[...]
    mt = jnp.maximum(a, ig)
    inter = jnp.exp(a - mt)
    e_i = jnp.exp(ig - mt)
    s = jnp.sum(q * ks, axis=-1, keepdims=True) * e_i
    c_q = jnp.sum(c * q, axis=-1, keepdims=True)
    n_q = jnp.sum(nrow * q, axis=-1, keepdims=True)
    num = s * v + inter * c_q
    den = s + inter * n_q
    hval = num / jnp.maximum(jnp.abs(den), jnp.exp(-mt))
    cn_ref[...] = inter * c + (e_i * v) * ks
    nn_ref[...] = inter * nrow + e_i * ks
    mn_ref[...] = mt
    ms = jnp.mean(hval * hval, axis=1, keepdims=True)
    y = hval * lax.rsqrt(ms + NORM_EPS) * nw_ref[...]
    yml_ref[...] = jax.nn.sigmoid(mo_ref[...]) * y


def _smix(args, out_shapes, cnt):
    return pl.pallas_call(
        functools.partial(_smix_kernel, cnt=cnt),
        out_shape=out_shapes,
        compiler_params=pltpu.CompilerParams(vmem_limit_bytes=VMEM_LIMIT),
        name="sample_mixers",
    )(*args)


PAGES_PER_STEP = 64
PAGES_PER_BLOCK = MOBA_BLOCK // PAGE_SIZE


def _paged_kmean_kernel(pt_ref, *refs):
    o_ref = refs[-1]
    j = pl.program_id(1)
    nblk = PAGES_PER_STEP // PAGES_PER_BLOCK

    @pl.when(j == 0)
    def _():
        o_ref[...] = jnp.zeros(o_ref.shape, F32)

    lane = lax.broadcasted_iota(I32, o_ref.shape, 1)
    acc = o_ref[...]
    for bb in range(nblk):
        s = refs[bb * PAGES_PER_BLOCK][...]
        for pg in range(1, PAGES_PER_BLOCK):
            s = s + refs[bb * PAGES_PER_BLOCK + pg][...]
        col = jnp.sum(s, axis=1, keepdims=True) * (1.0 / MOBA_BLOCK)
        acc = jnp.where(lane == j * nblk + bb, col, acc)
    o_ref[...] = acc


def _paged_kmean(cache_t, page_table, page_base):
    bs, n_pages = page_table.shape
    nb = n_pages // PAGES_PER_BLOCK
    page = lambda s: pl.BlockSpec((None, GROUP_W, PAGE_SIZE),
                                  lambda b, j, pt, s=s: (page_base + pt[b, j * PAGES_PER_STEP + s], 0, 0))
    grid_spec = pltpu.PrefetchScalarGridSpec(
        num_scalar_prefetch=1,
        grid=(bs, n_pages // PAGES_PER_STEP),
        in_specs=[page(s) for s in range(PAGES_PER_STEP)],
        out_specs=pl.BlockSpec((None, GROUP_W, nb), lambda b, j, pt: (b, 0, 0)))
    return pl.pallas_call(
        _paged_kmean_kernel,
        grid_spec=grid_spec,
        out_shape=jax.ShapeDtypeStruct((bs, GROUP_W, nb), F32),
        compiler_params=_params("parallel", "arbitrary"),
        name="paged_kmean",
    )(page_table, *([cache_t] * PAGES_PER_STEP))


def _sample_select_kernel(q_ref, kmt_ref, o_ref):
    head_row = lax.broadcasted_iota(I32, (8, GROUP_W), 0) == _lane_head((8, GROUP_W), 1)
    for b in range(q_ref.shape[0]):
        q8 = jnp.where(head_row, q_ref[b], 0.0)
        s = jnp.dot(q8, kmt_ref[b], precision=HIGHEST, preferred_element_type=F32)
        blk = lax.broadcasted_iota(I32, s.shape, 1)
        slot = lax.broadcasted_iota(I32, o_ref.shape[1:], 1)
        out = jnp.zeros(o_ref.shape[1:], I32)
        for r in range(MOBA_TOPK):
            mx = jnp.max(s, axis=1, keepdims=True)
            idx = jnp.min(jnp.where(s == mx, blk, s.shape[1]), axis=1, keepdims=True)
            out = jnp.where(slot == r, idx, out)
            s = jnp.where(blk == idx, NEG_PICKED, s)
        o_ref[b] = out


def _sample_select(q3, kmean_t):
    bs, _, nb = kmean_t.shape
    sb = _row_tile(bs, 8)
    return pl.pallas_call(
        _sample_select_kernel,
        grid=(bs // sb,),
        in_specs=[pl.BlockSpec((sb, 1, GROUP_W), lambda b: (b, 0, 0)),
                  pl.BlockSpec((sb, GROUP_W, nb), lambda b: (b, 0, 0))],
        out_specs=pl.BlockSpec((sb, 8, LANE), lambda b: (b, 0, 0)),
        out_shape=jax.ShapeDtypeStruct((bs, 8, LANE), I32),
        compiler_params=_params("parallel"),
        name="sample_select",
    )(q3, kmean_t)


N_SEL_PAGES = MOBA_TOPK * PAGES_PER_BLOCK


def _sample_attn_kernel(ph_ref, q_ref, kn_ref, vn_ref, *refs):
    n_pg = N_HEADS * N_SEL_PAGES
    kp, vp, o_ref = refs[:n_pg], refs[n_pg:2 * n_pg], refs[-1]
    for h in range(N_HEADS):
        hs = slice(h * HEAD_DIM, (h + 1) * HEAD_DIM)
        pages = slice(h * N_SEL_PAGES, (h + 1) * N_SEL_PAGES)
        q = q_ref[hs, :] * (HEAD_DIM ** -0.5)
        own = jnp.sum(kn_ref[hs, :] * q, axis=0, keepdims=True)
        logits = [jnp.sum(r[...] * q, axis=0, keepdims=True) for r in kp[pages]]
        m = own
        for lg in logits:
            m = jnp.maximum(m, jnp.max(lg, axis=1, keepdims=True))
        p_own = jnp.exp(own - m)
        den = p_own
        acc = p_own * vn_ref[hs, :]
        for lg, vr in zip(logits, vp[pages]):
            p = jnp.exp(lg - m)
            den = den + jnp.sum(p, axis=1, keepdims=True)
            acc = acc + jnp.sum(vr[...] * p, axis=1, keepdims=True)
        o_ref[hs, :] = acc / den


def _sample_attn(phys, q_col, kn_col, vn_col, cache_kt, cache_vt, page_base):
    bs = q_col.shape[0]
    col = pl.BlockSpec((None, GROUP_W, 1), lambda b, ph: (b, 0, 0))
    page = lambda h, s: pl.BlockSpec(
        (None, HEAD_DIM, PAGE_SIZE),
        lambda b, ph, h=h, s=s: (page_base + ph[(b * N_HEADS + h) * N_SEL_PAGES + s], h, 0))
    pages = [page(h, s) for h in range(N_HEADS) for s in range(N_SEL_PAGES)]
    grid_spec = pltpu.PrefetchScalarGridSpec(
        num_scalar_prefetch=1,
        grid=(bs,),
        in_specs=[col, col, col] + pages + pages,
        out_specs=col)
    return pl.pallas_call(
        _sample_attn_kernel,
        grid_spec=grid_spec,
        out_shape=jax.ShapeDtypeStruct((bs, GROUP_W, 1), F32),
        compiler_params=_params("parallel"),
        name="sample_attn",
    )(phys, q_col, kn_col, vn_col, *([cache_kt] * len(pages)), *([cache_vt] * len(pages)))


def _prep_layer(w_in, w_out, pool_w, pool_scale, sgu_w, sgu_b, norm_w):
    d = w_in.shape[0]
    g = GROUP_W
    gate0 = 7 * g
    att0 = gate0 + 2 * N_HEADS
    pad = jnp.zeros((d, GATE_W - 2 * N_HEADS), w_in.dtype)
    w_z = jnp.concatenate([w_in[:, :gate0], w_in[:, att0:], w_in[:, gate0:att0], pad], axis=1).astype(BF16)
    wbd = jnp.zeros((g, g), F32)
    for i in range(len(POOL_WINDOWS)):
        sl = slice(i * POOL_GROUP, (i + 1) * POOL_GROUP)
        wbd = wbd.at[sl, sl].set(pool_w[i])
    return dict(
        w_z=w_z, w_out=w_out.astype(BF16), pool_wbd=wbd.astype(BF16),
        pool_scale=pool_scale.reshape(1, g),
        sgu_w=sgu_w, sgu_bias=jnp.repeat(sgu_b.T, HEAD_DIM, axis=1),
        sgu_w0=jnp.repeat(sgu_w[:, 0, 0], HEAD_DIM).reshape(1, g),
        sgu_b0=jnp.repeat(sgu_b[:, 0], HEAD_DIM).reshape(1, g),
        norm_w=norm_w.reshape(1, g))


def _head_cols(a, bs):
    return a.reshape(bs * N_HEADS, HEAD_DIM, 1)


def _head_rows(a, bs):
    return a.reshape(bs * N_HEADS, 1, HEAD_DIM)


def _mixers_prompt(x, batch, seq, nmw, lp, gate_b, rope_tabs, layer, depth, knew, vnew):
    g = GROUP_W
    nb = seq // MOBA_BLOCK
    zm, q, gates, gates_t, kb, vt, kmean, knew, vnew = _inproj_prompt(
        x, nmw, lp["w_z"], rope_tabs, batch, seq, layer, depth, knew, vnew)
    m_ab = _poolsgu(zm, batch, seq, lp["pool_wbd"], lp["pool_scale"], lp["sgu_w"], lp["sgu_bias"])
    m_c, ct, nst, mst = _mlstm(zm, gates, gates_t, batch, seq, gate_b, lp["norm_w"])
    blocks = lambda a: a.reshape(batch, nb, MOBA_BLOCK, g)
    m_d = _moba(q, blocks(kb), blocks(vt), kmean, batch, nb)
    h = _outproj(x, m_ab, m_c, m_d, lp["w_out"])

    c_new = jnp.stack([ct[:, i * HEAD_DIM:(i + 1) * HEAD_DIM, i * HEAD_DIM:(i + 1) * HEAD_DIM]
                       for i in range(N_HEADS)], axis=1)
    state = (zm[:, :g].reshape(batch, seq, g)[:, seq - POOL_BUF:],
             jnp.swapaxes(c_new, -1, -2), nst.reshape(batch, N_HEADS, HEAD_DIM), mst[:, :N_HEADS, 0])
    return h, state, knew, vnew


def _mixers_sample(x, past_len, nmw, lp, gate_b, rope_tabs, pool_state, c_st, n_st, m_st,
                   cache_k, cache_v, page_table, page_base):
    bs = x.shape[0]
    g = GROUP_W
    z = _inproj(x, nmw, lp["w_z"], rope_tabs)
    col = lambda c: z[:, c * g:(c + 1) * g]
    p, sv = col(COL_P), col(COL_SV)
    gates = z[:, 10 * g:10 * g + 2 * N_HEADS] + gate_b.reshape(1, -1)
    bh = bs * N_HEADS
    prefix = jnp.concatenate([jnp.zeros((1, bs, g), F32), jnp.swapaxes(pool_state, 0, 1)], axis=0)
    args = (p, prefix, lp["pool_wbd"], lp["pool_scale"], col(COL_SU), sv, lp["sgu_w0"], lp["sgu_b0"],
            _head_rows(col(COL_MQ), bs), _head_rows(col(COL_MK), bs), _head_cols(col(COL_MV), bs),
            gates[:, :N_HEADS].reshape(bh, 1, 1), gates[:, N_HEADS:].reshape(bh, 1, 1),
            c_st.reshape(bh, HEAD_DIM, HEAD_DIM), n_st.reshape(bh, 1, HEAD_DIM), m_st.reshape(bh, 1, 1),
            _head_cols(col(COL_MO), bs), _head_cols(jnp.tile(lp["norm_w"], (bs, 1)), bs))
    sds = jax.ShapeDtypeStruct
    outs = (sds((bs, g), F32), sds((bs, g), F32), sds((bh, HEAD_DIM, 1), F32),
            sds((bh, HEAD_DIM, HEAD_DIM), F32), sds((bh, 1, HEAD_DIM), F32), sds((bh, 1, 1), F32))
    y_pool, y_sgu, y_ml, c_new, n_new, m_new = _smix(args, outs, min(past_len + 1, POOL_WINDOWS[-1]))

    q_col, k_col, v_col = (col(c).reshape(bs, g, 1) for c in (COL_AQ, COL_AK, COL_AV))
    kmean_t = _paged_kmean(cache_k, page_table, page_base)
    sel = _sample_select(col(COL_AQ).reshape(bs, 1, g), kmean_t)[:, :N_HEADS, :MOBA_TOPK]
    pages = sel[..., None] * PAGES_PER_BLOCK + jnp.arange(PAGES_PER_BLOCK, dtype=I32)
    phys = jnp.take_along_axis(page_table, pages.reshape(bs, -1), axis=1).reshape(-1)
    y_at = _sample_attn(phys, q_col, k_col, v_col, cache_k, cache_v, page_base).reshape(bs, g)

    m_ab = jnp.concatenate([y_pool, y_sgu], axis=1).astype(BF16)
    h = _outproj(x, m_ab, y_ml.reshape(bs, g).astype(BF16), y_at.astype(BF16), lp["w_out"])

    heads = lambda a: a.reshape(bs, 1, N_HEADS, HEAD_DIM)
    state = (heads(col(COL_AK)), heads(col(COL_AV)),
             jnp.concatenate([pool_state[:, 1:], p[:, None, :]], axis=1), sv.reshape(bs, 1, g),
             c_new.reshape(bs, N_HEADS, HEAD_DIM, HEAD_DIM), n_new.reshape(bs, N_HEADS, HEAD_DIM),
             m_new.reshape(bs, N_HEADS))
    return h, state


def kernel(x_prompt, x_sample, cache_k, cache_v, page_table, state_pool, state_mlstm_c, state_mlstm_n,
           state_mlstm_m, norm_mix_w, norm_ffn_w, final_norm_w, w_in, w_out, pool_w, pool_scale, sgu_w,
           sgu_b, mlstm_gate_b, mlstm_norm_w, ffn_w1, ffn_w3, ffn_w2, router_w, router_b,
           moe_w1, moe_w3, moe_w2):
    batch, seq, d = x_prompt.shape
    bs = x_sample.shape[0]
    depth = w_in.shape[0]
    n_pool = cache_k.shape[1]
    n_pages = page_table.shape[1]
    past_len = n_pages * PAGE_SIZE
    n_experts = router_w.shape[-1]
    assert x_sample.shape[1] == 1 and d == N_MIXERS * GROUP_W
    assert seq % MOBA_BLOCK == 0 and seq % (2 * MLSTM_CHUNK) == 0 and seq >= POOL_BUF
    assert past_len % MOBA_BLOCK == 0 and past_len // MOBA_BLOCK >= MOBA_TOPK
    assert n_pages % PAGES_PER_STEP == 0

    rope_p = _rope_tables(jnp.arange(seq, dtype=I32))
    rope_s = _rope_tables(jnp.full((bs,), past_len, I32))
    pages_t = lambda c: jnp.transpose(c, (0, 1, 3, 4, 2)).reshape(depth * n_pool, GROUP_W, PAGE_SIZE)
    ck, cv = pages_t(cache_k), pages_t(cache_v)
    fw = final_norm_w.reshape(1, d)
    tm_moe = 512

    hp = x_prompt.reshape(batch * seq, d)
    hs = x_sample.reshape(bs, d)
    st_p, st_s = [], []
    knew = jnp.zeros((depth, batch, GROUP_W, seq), F32)
    vnew = jnp.zeros((depth, batch, GROUP_W, seq), F32)
    for l in range(depth):
        lp = _prep_layer(w_in[l], w_out[l], pool_w[l], pool_scale[l], sgu_w[l], sgu_b[l], mlstm_norm_w[l])
        nmw = norm_mix_w[l].reshape(1, d)
        nfw = norm_ffn_w[l].reshape(1, d)
        hp, sp, knew, vnew = _mixers_prompt(hp, batch, seq, nmw, lp, mlstm_gate_b[l], rope_p,
                                            l, depth, knew, vnew)
        hs, ss = _mixers_sample(hs, past_len, nmw, lp, mlstm_gate_b[l], rope_s, state_pool[l],
                                state_mlstm_c[l], state_mlstm_n[l], state_mlstm_m[l],
                                ck, cv, page_table, l * n_pool)
        st_p.append(sp)
        st_s.append(ss)
        last = l == depth - 1
        j = l // 2
        if l % 2 == 0:
            w1, w3, w2 = ffn_w1[j].astype(BF16), ffn_w3[j].astype(BF16), ffn_w2[j].astype(BF16)
            hp = _ffn(hp, nfw, w1, w3, w2)
            hs = _ffn(hs, nfw, w1, w3, w2)
            if last:
                hp, hs = _final_norm(hp, fw), _final_norm(hs, fw)
        else:
            w1, w3, w2 = moe_w1[j].astype(BF16), moe_w3[j].astype(BF16), moe_w2[j].astype(BF16)
            rw = jnp.zeros((d, LANE), F32).at[:, :n_experts].set(router_w[j])
            rb = jnp.full((1, LANE), -jnp.inf, F32).at[0, :n_experts].set(router_b[j])
            xp, gates, e_idx = _router(hp, nfw, rw, rb)
            pos, src, tile_expert, nused = _dispatch_plan(e_idx, n_experts, tm_moe)
            o_sorted = _gmm(xp, tile_expert, src, nused, w1, w3, w2, tm_moe)
            hp = _combine(pos, hp, gates, o_sorted, fw, last)
            _, gates_s, e_s = _router(hs, nfw, rw, rb)
            hs = _moe_small(hs, nfw, gates_s, e_s, w1, w3, w2, fw, last)

    stack = lambda sts, i: jnp.stack([s[i] for s in sts])
    kv_leaf = lambda a: jnp.transpose(a.reshape(depth, batch, N_HEADS, HEAD_DIM, seq), (0, 1, 4, 2, 3))
    return (hp.reshape(batch, seq, d), hs.reshape(bs, 1, d),
            kv_leaf(knew), kv_leaf(vnew), stack(st_s, 0), stack(st_s, 1),
            stack(st_p, 0), stack(st_s, 2), stack(st_s, 3),
            stack(st_p, 1), stack(st_p, 2), stack(st_p, 3),
            stack(st_s, 4), stack(st_s, 5), stack(st_s, 6))
```
